```python
import jax, jax.numpy as jnp
from jax import lax
import numpy as np

D_MODEL = 1024
BATCH = 16
SEQ = 2048
DEPTH = 1

MLA_HEADS = 8
MLA_Q_RANK = 384
MLA_KV_RANK = 256
MLA_NOPE = 64
MLA_ROPE = 32
MLA_V = 64
MLA_WIDTH = MLA_HEADS * MLA_V
ROPE_THETA = 10000.0
Q_BLOCK = 128

GLA_HEADS = 4
GLA_HEAD_K = 64
GLA_HEAD_V = 128
GLA_DK = GLA_HEADS * GLA_HEAD_K
GLA_DV = GLA_HEADS * GLA_HEAD_V
GLA_GATE_RANK = 16
GLA_GATE_NORMALIZER = 16.0
GLA_CHUNK = 64

PLE_DIM = 256

EPS = 1e-6

IN_SPLITS = (
    MLA_Q_RANK,
    MLA_KV_RANK,
    MLA_ROPE,
    MLA_WIDTH,
    GLA_DK,
    GLA_DK,
    GLA_DV,
    GLA_GATE_RANK,
    GLA_DV,
    D_MODEL,
    D_MODEL,
)
IN_WIDTH = sum(IN_SPLITS)

kernel_name = 'hybrid_mla_gla_gated_merge_ple'


def rmsnorm(x, g):
    xf = x.astype(jnp.float32)
    y = xf * lax.rsqrt(jnp.mean(xf * xf, axis=-1, keepdims=True) + EPS)
    return (y * g.astype(jnp.float32)).astype(x.dtype)


def rope(x, pos):
    r = x.shape[-1]
    inv_freq = 1.0 / (ROPE_THETA ** (jnp.arange(0, r, 2, dtype=jnp.float32) / r))
    ang = pos.astype(jnp.float32)[..., None] * inv_freq
    cos = jnp.cos(ang)[:, :, None, :]
    sin = jnp.sin(ang)[:, :, None, :]
    xf = x.astype(jnp.float32)
    x1, x2 = jnp.split(xf, 2, axis=-1)
    out = jnp.concatenate([x1 * cos - x2 * sin, x2 * cos + x1 * sin], axis=-1)
    return out.astype(x.dtype)


def mla_attention(q_nope, q_rope, k_nope, k_rope, v):
    b, s, h, _ = q_nope.shape
    nb = s // Q_BLOCK
    scale = (MLA_NOPE + MLA_ROPE) ** -0.5
    qn = q_nope.reshape(b, nb, Q_BLOCK, h, MLA_NOPE).transpose(1, 0, 2, 3, 4)
    qr = q_rope.reshape(b, nb, Q_BLOCK, h, MLA_ROPE).transpose(1, 0, 2, 3, 4)
    kpos = jnp.arange(s)

    def block(args):
        qn_i, qr_i, i = args
        sc = (jnp.einsum('bqhd,bkhd->bhqk', qn_i, k_nope).astype(jnp.float32)
              + jnp.einsum('bqhr,bkr->bhqk', qr_i, k_rope).astype(jnp.float32)) * scale
        qpos = i * Q_BLOCK + jnp.arange(Q_BLOCK)
        mask = kpos[None, :] <= qpos[:, None]
        sc = jnp.where(mask[None, None], sc, -jnp.inf)
        pr = jax.nn.softmax(sc, axis=-1).astype(v.dtype)
        return jnp.einsum('bhqk,bkhd->bqhd', pr, v)

    o = lax.map(block, (qn, qr, jnp.arange(nb)))
    return o.transpose(1, 0, 2, 3, 4).reshape(b, s, h, MLA_V)


def gla_chunked(q, k, v, g):
    b, s, h, dk = q.shape
    dv = v.shape[-1]
    c = GLA_CHUNK
    n = s // c

    def to_chunks(t):
        return t.astype(jnp.float32).reshape(b, n, c, h, t.shape[-1]).transpose(1, 0, 3, 2, 4)

    qc = to_chunks(q) * (dk ** -0.5)
    kc = to_chunks(k)
    vc = to_chunks(v)
    bc = jnp.cumsum(to_chunks(g), axis=3)
    causal = jnp.tril(jnp.ones((c, c), dtype=bool))

    def step(state, inp):
        qi, ki, vi, bi = inp
        o_inter = jnp.einsum('bhcd,bhde->bhce', qi * jnp.exp(bi), state)
        diff = bi[:, :, :, None, :] - bi[:, :, None, :, :]
        decay = jnp.exp(jnp.where(causal[None, None, :, :, None], diff, -jnp.inf))
        attn = jnp.einsum('bhid,bhjd,bhijd->bhij', qi, ki, decay)
        o = o_inter + jnp.einsum('bhij,bhje->bhie', attn, vi)
        b_last = bi[:, :, -1:, :]
        k_dec = ki * jnp.exp(b_last - bi)
        state = (state * jnp.exp(b_last[:, :, 0, :])[..., None]
                 + jnp.einsum('bhcd,bhce->bhde', k_dec, vi))
        return state, o

    s0 = jnp.zeros((b, h, dk, dv), jnp.float32)
    _, o = lax.scan(step, s0, (qc, kc, vc, bc))
    return o.transpose(1, 0, 3, 2, 4).reshape(b, s, h, dv)


def split_in(proj):
    offsets = [int(o) for o in np.cumsum(IN_SPLITS)[:-1]]
    return jnp.split(proj, offsets, axis=-1)


def setup_inputs(seed: int = 0) -> dict:
    key = jax.random.key(seed)
    ks = jax.random.split(key, 24)
    f32 = jnp.float32

    def w(k, shape, fan_in):
        return jax.random.normal(k, shape, f32) * (fan_in ** -0.5)

    def gain(k, shape):
        return 1.0 + 0.02 * jax.random.normal(k, shape, f32)

    x = jax.random.normal(ks[0], (BATCH, SEQ, D_MODEL), f32)
    p = jax.random.normal(ks[1], (DEPTH, BATCH, SEQ, PLE_DIM), f32)
    offs = jax.random.randint(ks[2], (BATCH, 1), 0, 4096, dtype=jnp.int32)
    positions = offs + jnp.arange(SEQ, dtype=jnp.int32)[None, :]
    return {
        'x': x,
        'p': p,
        'positions': positions,
        'norm_in_g': gain(ks[3], (DEPTH, D_MODEL)),
        'w_in': w(ks[4], (DEPTH, D_MODEL, IN_WIDTH), D_MODEL),
        'q_norm_g': gain(ks[5], (DEPTH, MLA_Q_RANK)),
        'w_uq': w(ks[6], (DEPTH, MLA_Q_RANK, MLA_HEADS * (MLA_NOPE + MLA_ROPE)), MLA_Q_RANK),
        'kv_norm_g': gain(ks[7], (DEPTH, MLA_KV_RANK)),
        'w_ukv': w(ks[8], (DEPTH, MLA_KV_RANK, MLA_HEADS * (MLA_NOPE + MLA_V)), MLA_KV_RANK),
        'w_gk_up': w(ks[9], (DEPTH, GLA_GATE_RANK, GLA_DK), GLA_GATE_RANK),
        'b_gk': 0.1 * jax.random.normal(ks[10], (DEPTH, GLA_DK), f32),
        'gla_norm_g': gain(ks[11], (DEPTH, GLA_HEAD_V)),
        'w_mla_br': w(ks[12], (DEPTH, MLA_WIDTH, D_MODEL), MLA_WIDTH),
        'w_gla_br': w(ks[13], (DEPTH, GLA_DV, D_MODEL), GLA_DV),
        'w_out': w(ks[14], (DEPTH, D_MODEL, D_MODEL), D_MODEL),
        'w_ple': w(ks[15], (DEPTH, PLE_DIM, D_MODEL), PLE_DIM),
        'ple_norm_g': gain(ks[16], (DEPTH, D_MODEL)),
        'ple_gate_norm_g': gain(ks[17], (DEPTH, D_MODEL)),
        'w_ple_gate': w(ks[18], (DEPTH, D_MODEL, D_MODEL), D_MODEL),
        'final_norm_g': gain(ks[19], (D_MODEL,)),
    }


def reference(x, p, positions, norm_in_g, w_in, q_norm_g, w_uq, kv_norm_g, w_ukv,
              w_gk_up, b_gk, gla_norm_g, w_mla_br, w_gla_br, w_out, w_ple, ple_norm_g,
              ple_gate_norm_g, w_ple_gate, final_norm_g):
    b, s, _ = x.shape
    for l in range(DEPTH):
        h = rmsnorm(x, norm_in_g[l])
        proj = h @ w_in[l]
        (cq, ckv, kr, gate_mla, gq, gk, gv, gk_low, gate_gla,
         merge_a, merge_b) = split_in(proj)

        cq = rmsnorm(cq, q_norm_g[l])
        q = (cq @ w_uq[l]).reshape(b, s, MLA_HEADS, MLA_NOPE + MLA_ROPE)
        q_nope, q_rope = q[..., :MLA_NOPE], q[..., MLA_NOPE:]
        q_rope = rope(q_rope, positions)
        ckv = rmsnorm(ckv, kv_norm_g[l])
        kv = (ckv @ w_ukv[l]).reshape(b, s, MLA_HEADS, MLA_NOPE + MLA_V)
        k_nope, v_mla = kv[..., :MLA_NOPE], kv[..., MLA_NOPE:]
        k_rope = rope(kr[:, :, None, :], positions)[:, :, 0, :]
        o_mla = mla_attention(q_nope, q_rope, k_nope, k_rope, v_mla).reshape(b, s, MLA_WIDTH)
        y_a = (o_mla * jax.nn.silu(gate_mla)) @ w_mla_br[l]

        qg = gq.reshape(b, s, GLA_HEADS, GLA_HEAD_K)
        kg = gk.reshape(b, s, GLA_HEADS, GLA_HEAD_K)
        vg = gv.reshape(b, s, GLA_HEADS, GLA_HEAD_V)
        log_a = jax.nn.log_sigmoid((gk_low @ w_gk_up[l] + b_gk[l]).astype(jnp.float32)) / GLA_GATE_NORMALIZER
        log_a = log_a.reshape(b, s, GLA_HEADS, GLA_HEAD_K)
        o_gla = gla_chunked(qg, kg, vg, log_a).astype(x.dtype)
        o_gla = rmsnorm(o_gla, gla_norm_g[l]).reshape(b, s, GLA_DV)
        y_b = (o_gla * jax.nn.silu(gate_gla)) @ w_gla_br[l]

        merged = jax.nn.sigmoid(merge_a) * y_a + jax.nn.sigmoid(merge_b) * y_b
        x = x + merged @ w_out[l]

        e = rmsnorm(p[l] @ w_ple[l], ple_norm_g[l])
        g_ple = jax.nn.sigmoid(rmsnorm(x, ple_gate_norm_g[l]) @ w_ple_gate[l])
        x = x + g_ple * e
    return rmsnorm(x, final_norm_g)
```

```python
import functools

import numpy as np
import jax
import jax.numpy as jnp
from jax import lax
from jax.experimental import pallas as pl
from jax.experimental.pallas import tpu as pltpu

F32 = jnp.float32
BF16 = jnp.bfloat16

MLA_HEADS = 8
MLA_Q_RANK = 384
MLA_KV_RANK = 256
MLA_NOPE = 64
MLA_ROPE = 32
MLA_V = 64
MLA_WIDTH = MLA_HEADS * MLA_V
ROPE_THETA = 10000.0
GLA_HEADS = 4
GLA_HEAD_K = 64
GLA_HEAD_V = 128
GLA_DK = GLA_HEADS * GLA_HEAD_K
GLA_DV = GLA_HEADS * GLA_HEAD_V
GLA_GATE_RANK = 16
GLA_GATE_NORMALIZER = 16.0
EPS = 1e-6

HEAD_LANES = 128
TILE = 256
PROJ_ROWS = 512
VMEM_LIMIT = 56 * 1024 * 1024

GLA_LEVELS = tuple(TILE >> (k + 1) for k in range(8))
GLA_NFACT = 2 + len(GLA_LEVELS)
GLA_DIAG = len(GLA_LEVELS)

NT = (((1,), (1,)), ((), ()))
TN = (((0,), (0,)), ((), ()))


def _dot(a, b):
    return jnp.dot(a, b, preferred_element_type=F32)


def _dot_nt(a, b):
    return lax.dot_general(a, b, NT, preferred_element_type=F32)


def _dot_tn(a, b):
    return lax.dot_general(a, b, TN, preferred_element_type=F32)


def _rms_rows(x, g):
    return x * lax.rsqrt(jnp.mean(x * x, axis=-1, keepdims=True) + EPS) * g


def _sigmoid(x):
    return 1.0 / (1.0 + jnp.exp(-x))


def _proj_kernel(x_ref, pos_ref, freq_ref, phase_ref, gin_ref, wstd_ref, wt_ref, gq_ref, wq_ref,
                 gkv_ref, wk_ref, wvt_ref,
                 q_out, k_out, vt_out, gmla_out, gq_out, gk_out, gv_out, ggla_out, glow_out,
                 ma_out, mb_out):
    h = _rms_rows(x_ref[0], gin_ref[...]).astype(BF16)

    pos = pos_ref[0].astype(F32)
    table = jnp.cos(pos * freq_ref[...] - phase_ref[...])

    cq = _rms_rows(_dot(h, wstd_ref[:, 0:384]), gq_ref[...]).astype(BF16)
    q = _dot(cq, wq_ref[...])
    qscale = table * ((MLA_NOPE + MLA_ROPE) ** -0.5)
    for hh in range(MLA_HEADS):
        sl = slice(hh * HEAD_LANES, (hh + 1) * HEAD_LANES)
        q_out[0, :, sl] = (q[:, sl] * qscale).astype(BF16)

    ckv = _rms_rows(_dot(h, wstd_ref[:, 384:640]), gkv_ref[...]).astype(BF16)
    u = _dot(h, wstd_ref[:, 640:768]) * table
    lane = lax.broadcasted_iota(jnp.int32, u.shape, 1)
    krot = jnp.where(lane >= MLA_NOPE, u + pltpu.roll(u, 32, 1) + pltpu.roll(u, 96, 1), 0.0)
    kk = _dot(ckv, wk_ref[...])
    for hh in range(MLA_HEADS):
        sl = slice(hh * HEAD_LANES, (hh + 1) * HEAD_LANES)
        k_out[0, :, sl] = (kk[:, sl] + krot).astype(BF16)

    def put_t(out, val):
        for c in range(PROJ_ROWS // TILE):
            out[0, c] = val[:, c * TILE:(c + 1) * TILE].astype(BF16)

    put_t(vt_out, _dot_nt(wvt_ref[...], ckv))

    ma_out[0] = _dot(h, wstd_ref[:, 768:1792]).astype(BF16)
    mb_out[0] = _dot(h, wstd_ref[:, 1792:2816]).astype(BF16)

    for out, lo, hi in ((gmla_out, 0, 512), (gq_out, 512, 768), (gk_out, 768, 1024),
                        (gv_out, 1024, 1536), (ggla_out, 1536, 2048), (glow_out, 2048, 2064)):
        put_t(out, _dot_nt(wt_ref[lo:hi, :], h))


def _projections(x, pos3, freq, phase, gin, wstd, wt, gq, wq, gkv, wk, wvt):
    b, s, d = x.shape
    nt = s // TILE
    rows = PROJ_ROWS
    tiles = rows // TILE

    def full(a):
        return pl.BlockSpec(a.shape, lambda i, j: (0,) * a.ndim)

    def tok(width):
        return pl.BlockSpec((1, rows, width), lambda i, j: (i, j, 0))

    def tr(n):
        return pl.BlockSpec((1, tiles, n, TILE), lambda i, j: (i, j, 0, 0))

    def tr_shape(n):
        return jax.ShapeDtypeStruct((b, nt, n, TILE), BF16)

    out_shape = (
        jax.ShapeDtypeStruct((b, s, MLA_HEADS * HEAD_LANES), BF16),
        jax.ShapeDtypeStruct((b, s, MLA_HEADS * HEAD_LANES), BF16),
        tr_shape(MLA_WIDTH),
        tr_shape(MLA_WIDTH),
        tr_shape(GLA_DK),
        tr_shape(GLA_DK),
        tr_shape(GLA_DV),
        tr_shape(GLA_DV),
        tr_shape(GLA_GATE_RANK),
        jax.ShapeDtypeStruct((b, s, d), BF16),
        jax.ShapeDtypeStruct((b, s, d), BF16),
    )
    out_specs = (tok(MLA_HEADS * HEAD_LANES), tok(MLA_HEADS * HEAD_LANES), tr(MLA_WIDTH), tr(MLA_WIDTH),
                 tr(GLA_DK), tr(GLA_DK), tr(GLA_DV), tr(GLA_DV), tr(GLA_GATE_RANK), tok(d), tok(d))
    in_specs = [tok(d), tok(1), full(freq), full(phase), full(gin), full(wstd), full(wt), full(gq),
                full(wq), full(gkv), full(wk), full(wvt)]
    return pl.pallas_call(
        _proj_kernel,
        grid=(b, s // rows),
        in_specs=in_specs,
        out_specs=out_specs,
        out_shape=out_shape,
        compiler_params=pltpu.CompilerParams(
            dimension_semantics=("parallel", "parallel"), vmem_limit_bytes=VMEM_LIMIT),
        name="proj",
    )(x, pos3, freq, phase, gin, wstd, wt, gq, wq, gkv, wk, wvt)


def _mla_kernel(q_ref, k_ref, vt_ref, o_ref):
    nt = vt_ref.shape[1]
    row = lax.broadcasted_iota(jnp.int32, (TILE, TILE), 0)
    col = lax.broadcasted_iota(jnp.int32, (TILE, TILE), 1)
    causal = row <= col
    for qi in range(nt):
        qt = q_ref[0, qi * TILE:(qi + 1) * TILE, :]
        m = jnp.full((1, TILE), -jnp.inf, F32)
        l = jnp.zeros((1, TILE), F32)
        acc = jnp.zeros((MLA_V, TILE), F32)
        for kj in range(qi + 1):
            s = _dot_nt(k_ref[0, kj * TILE:(kj + 1) * TILE, :], qt)
            if kj == qi:
                s = jnp.where(causal, s, -jnp.inf)
            m_new = jnp.maximum(m, jnp.max(s, axis=0, keepdims=True))
            alpha = jnp.exp(m - m_new)
            p = jnp.exp(s - m_new)
            l = alpha * l + jnp.sum(p, axis=0, keepdims=True)
            acc = alpha * acc + _dot(vt_ref[0, kj], p.astype(BF16))
            m = m_new
        o_ref[0, qi] = (acc / l).astype(BF16)


def _mla_attention(q, k, vt):
    b, s, _ = q.shape
    nt = s // TILE
    tok = pl.BlockSpec((1, s, HEAD_LANES), lambda i, h: (i, 0, h))
    tr = pl.BlockSpec((1, nt, MLA_V, TILE), lambda i, h: (i, 0, h, 0))
    return pl.pallas_call(
        _mla_kernel,
        grid=(b, MLA_HEADS),
        in_specs=[tok, tok, tr],
        out_specs=tr,
        out_shape=jax.ShapeDtypeStruct((b, nt, MLA_WIDTH, TILE), BF16),
        compiler_params=pltpu.CompilerParams(
            dimension_semantics=("parallel", "parallel"), vmem_limit_bytes=VMEM_LIMIT),
        name="mla",
    )(q, k, vt)


def _gla_constants():
    u = np.arange(TILE)[:, None]
    t = np.arange(TILE)[None, :]
    mats = [(u <= t), (u > t)]
    for m in GLA_LEVELS:
        pos = t % (2 * m)
        mid = t - pos + m - 1
        upper = pos >= m
        mats.append(np.where(upper, (u > mid) & (u <= t), (u > t) & (u <= mid)))
    seg = np.concatenate([mm.astype(np.float32) for mm in mats], axis=1)
    seg = np.concatenate([seg, seg, seg], axis=0)
    j = np.arange(TILE)[:, None]
    i = np.arange(TILE)[None, :]
    level = np.full((TILE, TILE), -1, np.int32)
    level[j == i] = GLA_DIAG
    for idx, m in enumerate(GLA_LEVELS):
        same = (j // (2 * m)) == (i // (2 * m))
        split = ((j % (2 * m)) < m) & ((i % (2 * m)) >= m)
        level[same & split & (j < i)] = idx
    return seg, level


def _gla_kernel(qt_ref, kt_ref, vt_ref, glow_ref, wg_ref, bg_ref, seg_ref, level_ref, gn_ref,
                o_ref, state_ref, fact_ref):
    @pl.when(pl.program_id(1) == 0)
    def _():
        state_ref[...] = jnp.zeros_like(state_ref)

    z = _dot(wg_ref[...], glow_ref[0, 0]) + bg_ref[...]
    g = -(jnp.maximum(-z, 0.0) + jnp.log1p(jnp.exp(-jnp.abs(z)))) * (1.0 / GLA_GATE_NORMALIZER)
    g_hi = g.astype(BF16)
    r = g - g_hi.astype(F32)
    g_mid = r.astype(BF16)
    g_lo = (r - g_mid.astype(F32)).astype(BF16)
    g3 = jnp.concatenate([g_hi, g_mid, g_lo], axis=1)
    for n in range(GLA_NFACT):
        fact_ref[n] = jnp.exp(_dot(g3, seg_ref[:, n * TILE:(n + 1) * TILE]))

    level = level_ref[...]
    lane = lax.broadcasted_iota(jnp.int32, (GLA_HEAD_K, TILE), 1)
    for hh in range(GLA_HEADS):
        rk = slice(hh * GLA_HEAD_K, (hh + 1) * GLA_HEAD_K)
        rv = slice(hh * GLA_HEAD_V, (hh + 1) * GLA_HEAD_V)
        q16 = qt_ref[0, 0, rk, :]
        k16 = kt_ref[0, 0, rk, :]
        qf = q16.astype(F32)
        kf = k16.astype(F32)
        vt = vt_ref[0, 0, rv, :]
        f_in = fact_ref[0, rk, :]
        q_in = (qf * f_in).astype(BF16)
        k_out = (kf * fact_ref[1, rk, :]).astype(BF16)
        state = state_ref[hh]
        o = _dot_tn(state.astype(BF16), q_in)

        at = jnp.where(level == GLA_DIAG, _dot_tn(k16, q16), 0.0)
        for idx, m in enumerate(GLA_LEVELS):
            f = fact_ref[2 + idx, rk, :]
            upper = (lane // m) % 2 == 1
            q_up = jnp.where(upper, qf * f, 0.0).astype(BF16)
            k_lo = jnp.where(upper, 0.0, kf * f).astype(BF16)
            at = at + jnp.where(level == idx, _dot_tn(k_lo, q_up), 0.0)
        o = o + _dot(vt, at.astype(BF16))

        ms = jnp.mean(o * o, axis=0, keepdims=True)
        o_ref[0, 0, rv, :] = (o * lax.rsqrt(ms + EPS) * gn_ref[...]).astype(BF16)

        state_ref[hh] = state * f_in[:, TILE - 1:TILE] + _dot_nt(k_out, vt)


def _gla(qt, kt, vt, glow, wg, bg, seg, level, gn):
    b, nt = qt.shape[0], qt.shape[1]

    def full(a):
        return pl.BlockSpec(a.shape, lambda i, j: (0,) * a.ndim)

    def tr(n):
        return pl.BlockSpec((1, 1, n, TILE), lambda i, j: (i, j, 0, 0))

    return pl.pallas_call(
        _gla_kernel,
        grid=(b, nt),
        in_specs=[tr(GLA_DK), tr(GLA_DK), tr(GLA_DV), tr(GLA_GATE_RANK), full(wg), full(bg),
                  full(seg), full(level), full(gn)],
        out_specs=tr(GLA_DV),
        out_shape=jax.ShapeDtypeStruct((b, nt, GLA_DV, TILE), BF16),
        scratch_shapes=[pltpu.VMEM((GLA_HEADS, GLA_HEAD_K, GLA_HEAD_V), F32),
                        pltpu.VMEM((GLA_NFACT, GLA_DK, TILE), F32)],
        compiler_params=pltpu.CompilerParams(
            dimension_semantics=("parallel", "arbitrary"), vmem_limit_bytes=VMEM_LIMIT),
        name="gla",
    )(qt, kt, vt, glow, wg, bg, seg, level, gn)


def _out_kernel(x_ref, p_ref, om_ref, gm_ref, og_ref, gg_ref, ma_ref, mb_ref, wa_ref, wb_ref,
                wout_ref, wple_ref, gple_ref, ggate_ref, wgate_ref, gfin_ref, o_ref, *, final):
    def branch(o_t_ref, g_t_ref, w_ref):
        parts = []
        for c in range(PROJ_ROWS // TILE):
            gate = g_t_ref[0, c].astype(F32)
            zt = (o_t_ref[0, c].astype(F32) * (gate * _sigmoid(gate))).astype(BF16)
            parts.append(_dot_tn(zt, w_ref[...]))
        return jnp.concatenate(parts, axis=0)

    ya = branch(om_ref, gm_ref, wa_ref)
    yb = branch(og_ref, gg_ref, wb_ref)
    merged = _sigmoid(ma_ref[0].astype(F32)) * ya + _sigmoid(mb_ref[0].astype(F32)) * yb
    x1 = x_ref[0] + _dot(merged.astype(BF16), wout_ref[...])
    e = _rms_rows(_dot(p_ref[0].astype(BF16), wple_ref[...]), gple_ref[...])
    gate = _sigmoid(_dot(_rms_rows(x1, ggate_ref[...]).astype(BF16), wgate_ref[...]))
    x2 = x1 + gate * e
    o_ref[0] = _rms_rows(x2, gfin_ref[...]) if final else x2


def _output(x, p, om, gm, og, gg, ma, mb, wa, wb, wout, wple, gple, ggate, wgate, gfin, final):
    b, s, d = x.shape
    rows = PROJ_ROWS
    tiles = rows // TILE

    def full(a):
        return pl.BlockSpec(a.shape, lambda i, j: (0,) * a.ndim)

    def tok(width):
        return pl.BlockSpec((1, rows, width), lambda i, j: (i, j, 0))

    def tr(n):
        return pl.BlockSpec((1, tiles, n, TILE), lambda i, j: (i, j, 0, 0))

    in_specs = [tok(d), tok(p.shape[-1]), tr(MLA_WIDTH), tr(MLA_WIDTH), tr(GLA_DV), tr(GLA_DV),
                tok(d), tok(d), full(wa), full(wb), full(wout), full(wple), full(gple), full(ggate),
                full(wgate), full(gfin)]
    return pl.pallas_call(
        functools.partial(_out_kernel, final=final),
        grid=(b, s // rows),
        in_specs=in_specs,
        out_specs=tok(d),
        out_shape=jax.ShapeDtypeStruct((b, s, d), F32),
        compiler_params=pltpu.CompilerParams(
            dimension_semantics=("parallel", "parallel"), vmem_limit_bytes=VMEM_LIMIT),
        name="out",
    )(x, p, om, gm, og, gg, ma, mb, wa, wb, wout, wple, gple, ggate, wgate, gfin)


def _swap_halves(w):
    half = w.shape[-1] // 2
    return jnp.concatenate([-w[..., half:], w[..., :half]], axis=-1)


def _rope_rows():
    inv_freq = 1.0 / (ROPE_THETA ** (jnp.arange(0, MLA_ROPE, 2, dtype=F32) / MLA_ROPE))
    zeros = jnp.zeros((MLA_NOPE,), F32)
    freq = jnp.concatenate([zeros, inv_freq, inv_freq, inv_freq, inv_freq])
    phase = jnp.concatenate([zeros, jnp.zeros((MLA_ROPE,), F32), jnp.full((MLA_ROPE,), np.pi / 2, F32)])
    return freq[None, :], phase[None, :]


def kernel(x, p, positions, norm_in_g, w_in, q_norm_g, w_uq, kv_norm_g, w_ukv, w_gk_up, b_gk,
           gla_norm_g, w_mla_br, w_gla_br, w_out, w_ple, ple_norm_g, ple_gate_norm_g, w_ple_gate,
           final_norm_g):
    b, s, d = x.shape
    depth = w_in.shape[0]
    assert s % PROJ_ROWS == 0 and PROJ_ROWS % TILE == 0
    pos3 = positions.reshape(b, s, 1)
    freq, phase = _rope_rows()
    seg_np, level_np = _gla_constants()
    seg = jnp.asarray(seg_np, BF16)
    level = jnp.asarray(level_np)

    sizes = (MLA_Q_RANK, MLA_KV_RANK, MLA_ROPE, MLA_WIDTH, GLA_DK, GLA_DK, GLA_DV, GLA_GATE_RANK,
             GLA_DV, d, d)
    offs = np.concatenate([[0], np.cumsum(sizes)])

    def cols(w, idx):
        return w[:, int(offs[idx]):int(offs[idx + 1])]

    for l in range(depth):
        w = w_in[l]
        wkr = cols(w, 2)
        wstd = jnp.concatenate(
            [cols(w, 0), cols(w, 1), jnp.zeros((d, MLA_NOPE), F32), wkr, _swap_halves(wkr),
             cols(w, 9), cols(w, 10)], axis=1).astype(BF16)
        wt = jnp.concatenate(
            [cols(w, 3), cols(w, 4) * (GLA_HEAD_K ** -0.5), cols(w, 5), cols(w, 6), cols(w, 8),
             cols(w, 7)], axis=1).T.astype(BF16)
        uq = w_uq[l].reshape(MLA_Q_RANK, MLA_HEADS, MLA_NOPE + MLA_ROPE)
        uq_r = uq[..., MLA_NOPE:]
        wq = jnp.concatenate([uq[..., :MLA_NOPE], uq_r, _swap_halves(uq_r)], axis=-1)
        wq = wq.reshape(MLA_Q_RANK, MLA_HEADS * HEAD_LANES).astype(BF16)
        ukv = w_ukv[l].reshape(MLA_KV_RANK, MLA_HEADS, MLA_NOPE + MLA_V)
        wk = jnp.concatenate([ukv[..., :MLA_NOPE], jnp.zeros_like(ukv[..., :MLA_NOPE])], axis=-1)
        wk = wk.reshape(MLA_KV_RANK, MLA_HEADS * HEAD_LANES).astype(BF16)
        wvt = ukv[..., MLA_NOPE:].reshape(MLA_KV_RANK, MLA_WIDTH).T.astype(BF16)

        (q, k, vt, gmla, gq, gk, gv, ggla, glow, ma, mb) = _projections(
            x, pos3, freq, phase, norm_in_g[l][None, :], wstd, wt, q_norm_g[l][None, :], wq,
            kv_norm_g[l][None, :], wk, wvt)

        o_mla = _mla_attention(q, k, vt)
        o_gla = _gla(gq, gk, gv, glow, w_gk_up[l].T.astype(BF16), b_gk[l][:, None], seg, level,
                     gla_norm_g[l][:, None])

        x = _output(x, p[l], o_mla, gmla, o_gla, ggla, ma, mb, w_mla_br[l].astype(BF16),
                    w_gla_br[l].astype(BF16), w_out[l].astype(BF16), w_ple[l].astype(BF16),
                    ple_norm_g[l][None, :], ple_gate_norm_g[l][None, :], w_ple_gate[l].astype(BF16),
                    final_norm_g[None, :], l == depth - 1)
    return x
```

```python
import functools

import numpy as np
import jax
import jax.numpy as jnp
from jax import lax
from jax.experimental import pallas as pl
from jax.experimental.pallas import tpu as pltpu

F32 = jnp.float32
BF16 = jnp.bfloat16

MLA_HEADS = 8
MLA_Q_RANK = 384
MLA_KV_RANK = 256
MLA_NOPE = 64
MLA_ROPE = 32
MLA_V = 64
MLA_WIDTH = MLA_HEADS * MLA_V
ROPE_THETA = 10000.0
GLA_HEADS = 4
GLA_HEAD_K = 64
GLA_HEAD_V = 128
GLA_DK = GLA_HEADS * GLA_HEAD_K
GLA_DV = GLA_HEADS * GLA_HEAD_V
GLA_GATE_RANK = 16
GLA_GATE_NORMALIZER = 16.0
EPS = 1e-6
LOG2E = 1.4426950408889634

HEAD_LANES = 128
TILE = 256
PROJ_ROWS = 512
VMEM_LIMIT = 56 * 1024 * 1024

GLA_LEVELS = tuple(TILE >> (k + 1) for k in range(8))
GLA_NFACT = 2 + len(GLA_LEVELS)
GLA_DIAG = len(GLA_LEVELS)

NT = (((1,), (1,)), ((), ()))
TN = (((0,), (0,)), ((), ()))


def _dot(a, b):
    return jnp.dot(a, b, preferred_element_type=F32)


def _dot_nt(a, b):
    return lax.dot_general(a, b, NT, preferred_element_type=F32)


def _dot_tn(a, b):
    return lax.dot_general(a, b, TN, preferred_element_type=F32)


def _rms_rows(x, g):
    return x * lax.rsqrt(jnp.mean(x * x, axis=-1, keepdims=True) + EPS) * g


def _sigmoid(x):
    return 1.0 / (1.0 + jnp.exp(-x))


def _proj_kernel(x_ref, pos_ref, freq_ref, phase_ref, gin_ref, wstd_ref, wt_ref, gq_ref, wq_ref,
                 gkv_ref, wk_ref, wvt_ref,
                 q_out, k_out, vt_out, gmla_out, gq_out, gk_out, gv_out, ggla_out, glow_out,
                 ma_out, mb_out):
    h = _rms_rows(x_ref[0], gin_ref[...]).astype(BF16)

    pos = pos_ref[0].astype(F32)
    table = jnp.cos(pos * freq_ref[...] - phase_ref[...])

    cq = _rms_rows(_dot(h, wstd_ref[:, 0:384]), gq_ref[...]).astype(BF16)
    q = _dot(cq, wq_ref[...])
    qscale = table * ((MLA_NOPE + MLA_ROPE) ** -0.5 * LOG2E)
    for hh in range(MLA_HEADS):
        q_out[0, hh] = (q[:, hh * HEAD_LANES:(hh + 1) * HEAD_LANES] * qscale).astype(BF16)

    ckv = _rms_rows(_dot(h, wstd_ref[:, 384:640]), gkv_ref[...]).astype(BF16)
    u = _dot(h, wstd_ref[:, 640:768]) * table
    lane = lax.broadcasted_iota(jnp.int32, u.shape, 1)
    krot = jnp.where(lane >= MLA_NOPE, u + pltpu.roll(u, 32, 1) + pltpu.roll(u, 96, 1), 0.0)
    kk = _dot(ckv, wk_ref[...])
    for hh in range(MLA_HEADS):
        k_out[0, hh] = (kk[:, hh * HEAD_LANES:(hh + 1) * HEAD_LANES] + krot).astype(BF16)

    vt = _dot_nt(wvt_ref[...], ckv)
    for hh in range(MLA_HEADS):
        vt_out[0, hh] = vt[hh * MLA_V:(hh + 1) * MLA_V, :].astype(BF16)

    ma_out[0] = _dot(h, wstd_ref[:, 768:1792]).astype(BF16)
    mb_out[0] = _dot(h, wstd_ref[:, 1792:2816]).astype(BF16)

    for out, lo, hi in ((gmla_out, 0, 512), (gq_out, 512, 768), (gk_out, 768, 1024),
                        (gv_out, 1024, 1536), (ggla_out, 1536, 2048), (glow_out, 2048, 2064)):
        out[0] = _dot_nt(wt_ref[lo:hi, :], h).astype(BF16)


def _projections(x, pos3, freq, phase, gin, wstd, wt, gq, wq, gkv, wk, wvt):
    b, s, d = x.shape
    rows = PROJ_ROWS

    def full(a):
        return pl.BlockSpec(a.shape, lambda i, j: (0,) * a.ndim)

    def tok(width):
        return pl.BlockSpec((1, rows, width), lambda i, j: (i, j, 0))

    def tr(n):
        return pl.BlockSpec((1, n, rows), lambda i, j: (i, 0, j))

    def tr_shape(n):
        return jax.ShapeDtypeStruct((b, n, s), BF16)

    heads_tok = pl.BlockSpec((1, MLA_HEADS, rows, HEAD_LANES), lambda i, j: (i, 0, j, 0))
    heads_tr = pl.BlockSpec((1, MLA_HEADS, MLA_V, rows), lambda i, j: (i, 0, 0, j))
    out_shape = (
        jax.ShapeDtypeStruct((b, MLA_HEADS, s, HEAD_LANES), BF16),
        jax.ShapeDtypeStruct((b, MLA_HEADS, s, HEAD_LANES), BF16),
        jax.ShapeDtypeStruct((b, MLA_HEADS, MLA_V, s), BF16),
        tr_shape(MLA_WIDTH),
        tr_shape(GLA_DK),
        tr_shape(GLA_DK),
        tr_shape(GLA_DV),
        tr_shape(GLA_DV),
        tr_shape(GLA_GATE_RANK),
        jax.ShapeDtypeStruct((b, s, d), BF16),
        jax.ShapeDtypeStruct((b, s, d), BF16),
    )
    out_specs = (heads_tok, heads_tok, heads_tr, tr(MLA_WIDTH),
                 tr(GLA_DK), tr(GLA_DK), tr(GLA_DV), tr(GLA_DV), tr(GLA_GATE_RANK), tok(d), tok(d))
    in_specs = [tok(d), tok(1), full(freq), full(phase), full(gin), full(wstd), full(wt), full(gq),
                full(wq), full(gkv), full(wk), full(wvt)]
    return pl.pallas_call(
        _proj_kernel,
        grid=(b, s // rows),
        in_specs=in_specs,
        out_specs=out_specs,
        out_shape=out_shape,
        compiler_params=pltpu.CompilerParams(
            dimension_semantics=("parallel", "parallel"), vmem_limit_bytes=VMEM_LIMIT),
        name="proj",
    )(x, pos3, freq, phase, gin, wstd, wt, gq, wq, gkv, wk, wvt)


def _mla_kernel(q_ref, k_ref, vt_ref, o_ref, s_ref, p_ref):
    seq = q_ref.shape[2]
    row = lax.broadcasted_iota(jnp.int32, (TILE, TILE), 0)
    col = lax.broadcasted_iota(jnp.int32, (TILE, TILE), 1)
    causal = row <= col
    for qi in range(seq // TILE):
        lo, hi = qi * TILE, (qi + 1) * TILE
        buf = qi % 2
        qt = q_ref[0, 0, lo:hi, :]
        if qi:
            s_ref[buf, 0:lo, :] = _dot_nt(k_ref[0, 0, 0:lo, :], qt)
        s_ref[buf, lo:hi, :] = jnp.where(causal, _dot_nt(k_ref[0, 0, lo:hi, :], qt), -jnp.inf)
        m = jnp.full((1, TILE), -jnp.inf, F32)
        for kj in range(qi + 1):
            m = jnp.maximum(m, jnp.max(s_ref[buf, kj * TILE:(kj + 1) * TILE, :], axis=0, keepdims=True))
        l = jnp.zeros((1, TILE), F32)
        for kj in range(qi + 1):
            blk = slice(kj * TILE, (kj + 1) * TILE)
            p = jnp.exp2(s_ref[buf, blk, :] - m)
            l = l + jnp.sum(p, axis=0, keepdims=True)
            p_ref[buf, blk, :] = p.astype(BF16)
        acc = _dot(vt_ref[0, 0, :, 0:hi], p_ref[buf, 0:hi, :])
        o_ref[0, 0, :, lo:hi] = (acc / l).astype(BF16)


def _mla_attention(q, k, vt):
    b, heads, s, _ = q.shape
    tok = pl.BlockSpec((1, 1, s, HEAD_LANES), lambda i, h: (i, h, 0, 0))
    tr = pl.BlockSpec((1, 1, MLA_V, s), lambda i, h: (i, h, 0, 0))
    return pl.pallas_call(
        _mla_kernel,
        grid=(b, heads),
        in_specs=[tok, tok, tr],
        out_specs=tr,
        out_shape=jax.ShapeDtypeStruct((b, heads, MLA_V, s), BF16),
        scratch_shapes=[pltpu.VMEM((2, s, TILE), F32), pltpu.VMEM((2, s, TILE), BF16)],
        compiler_params=pltpu.CompilerParams(
            dimension_semantics=("parallel", "parallel"), vmem_limit_bytes=VMEM_LIMIT),
        name="mla",
    )(q, k, vt)


def _gla_constants():
    u = np.arange(TILE)[:, None]
    t = np.arange(TILE)[None, :]
    mats = [(u <= t), (u > t)]
    for m in GLA_LEVELS:
        pos = t % (2 * m)
        mid = t - pos + m - 1
        upper = pos >= m
        mats.append(np.where(upper, (u > mid) & (u <= t), (u > t) & (u <= mid)))
    seg = np.concatenate([mm.astype(np.float32) for mm in mats], axis=1)
    seg = np.concatenate([seg, seg, seg], axis=0)
    j = np.arange(TILE)[:, None]
    i = np.arange(TILE)[None, :]
    level = np.full((TILE, TILE), -1, np.int32)
    level[j == i] = GLA_DIAG
    for idx, m in enumerate(GLA_LEVELS):
        same = (j // (2 * m)) == (i // (2 * m))
        split = ((j % (2 * m)) < m) & ((i % (2 * m)) >= m)
        level[same & split & (j < i)] = idx
    return seg, level


def _gla_kernel(qt_ref, kt_ref, vt_ref, glow_ref, wg_ref, bg_ref, seg_ref, level_ref, gn_ref,
                o_ref, state_ref, fact_ref):
    @pl.when(pl.program_id(1) == 0)
    def _():
        state_ref[...] = jnp.zeros_like(state_ref)

    z = _dot(wg_ref[...], glow_ref[0]) + bg_ref[...]
    g = -(jnp.maximum(-z, 0.0) + jnp.log1p(jnp.exp(-jnp.abs(z)))) * (1.0 / GLA_GATE_NORMALIZER)
    g_hi = g.astype(BF16)
    r = g - g_hi.astype(F32)
    g_mid = r.astype(BF16)
    g_lo = (r - g_mid.astype(F32)).astype(BF16)
    g3 = jnp.concatenate([g_hi, g_mid, g_lo], axis=1)
    for n in range(GLA_NFACT):
        fact_ref[n] = jnp.exp(_dot(g3, seg_ref[:, n * TILE:(n + 1) * TILE]))

    level = level_ref[...]
    lane = lax.broadcasted_iota(jnp.int32, (GLA_HEAD_K, TILE), 1)
    for hh in range(GLA_HEADS):
        rk = slice(hh * GLA_HEAD_K, (hh + 1) * GLA_HEAD_K)
        rv = slice(hh * GLA_HEAD_V, (hh + 1) * GLA_HEAD_V)
        q16 = qt_ref[0, rk, :]
        k16 = kt_ref[0, rk, :]
        qf = q16.astype(F32)
        kf = k16.astype(F32)
        vt = vt_ref[0, rv, :]
        f_in = fact_ref[0, rk, :]
        q_in = (qf * f_in).astype(BF16)
        k_out = (kf * fact_ref[1, rk, :]).astype(BF16)
        state = state_ref[hh]
        o = _dot_tn(state.astype(BF16), q_in)

        at = jnp.where(level == GLA_DIAG, _dot_tn(k16, q16), 0.0)
        for idx, m in enumerate(GLA_LEVELS):
            f = fact_ref[2 + idx, rk, :]
            upper = (lane // m) % 2 == 1
            q_up = jnp.where(upper, qf * f, 0.0).astype(BF16)
            k_lo = jnp.where(upper, 0.0, kf * f).astype(BF16)
            at = at + jnp.where(level == idx, _dot_tn(k_lo, q_up), 0.0)
        o = o + _dot(vt, at.astype(BF16))

        ms = jnp.mean(o * o, axis=0, keepdims=True)
        o_ref[0, rv, :] = (o * lax.rsqrt(ms + EPS) * gn_ref[...]).astype(BF16)

        state_ref[hh] = state * f_in[:, TILE - 1:TILE] + _dot_nt(k_out, vt)


def _gla(qt, kt, vt, glow, wg, bg, seg, level, gn):
    b, _, s = qt.shape

    def full(a):
        return pl.BlockSpec(a.shape, lambda i, j: (0,) * a.ndim)

    def tr(n):
        return pl.BlockSpec((1, n, TILE), lambda i, j: (i, 0, j))

    return pl.pallas_call(
        _gla_kernel,
        grid=(b, s // TILE),
        in_specs=[tr(GLA_DK), tr(GLA_DK), tr(GLA_DV), tr(GLA_GATE_RANK), full(wg), full(bg),
                  full(seg), full(level), full(gn)],
        out_specs=tr(GLA_DV),
        out_shape=jax.ShapeDtypeStruct((b, GLA_DV, s), BF16),
        scratch_shapes=[pltpu.VMEM((GLA_HEADS, GLA_HEAD_K, GLA_HEAD_V), F32),
                        pltpu.VMEM((GLA_NFACT, GLA_DK, TILE), F32)],
        compiler_params=pltpu.CompilerParams(
            dimension_semantics=("parallel", "arbitrary"), vmem_limit_bytes=VMEM_LIMIT),
        name="gla",
    )(qt, kt, vt, glow, wg, bg, seg, level, gn)


def _out_kernel(x_ref, p_ref, om_ref, gm_ref, og_ref, gg_ref, ma_ref, mb_ref, wa_ref, wb_ref,
                wout_ref, wple_ref, gple_ref, ggate_ref, wgate_ref, gfin_ref, o_ref, *, final):
    def branch(o_t, g_t_ref, w_ref):
        gate = g_t_ref[0].astype(F32)
        zt = (o_t.astype(F32) * (gate * _sigmoid(gate))).astype(BF16)
        return _dot_tn(zt, w_ref[...])

    ya = branch(om_ref[0].reshape(MLA_WIDTH, PROJ_ROWS), gm_ref, wa_ref)
    yb = branch(og_ref[0], gg_ref, wb_ref)
    merged = _sigmoid(ma_ref[0].astype(F32)) * ya + _sigmoid(mb_ref[0].astype(F32)) * yb
    x1 = x_ref[0] + _dot(merged.astype(BF16), wout_ref[...])
    e = _rms_rows(_dot(p_ref[0].astype(BF16), wple_ref[...]), gple_ref[...])
    gate = _sigmoid(_dot(_rms_rows(x1, ggate_ref[...]).astype(BF16), wgate_ref[...]))
    x2 = x1 + gate * e
    o_ref[0] = _rms_rows(x2, gfin_ref[...]) if final else x2


def _output(x, p, om, gm, og, gg, ma, mb, wa, wb, wout, wple, gple, ggate, wgate, gfin, final):
    b, s, d = x.shape
    rows = PROJ_ROWS

    def full(a):
        return pl.BlockSpec(a.shape, lambda i, j: (0,) * a.ndim)

    def tok(width):
        return pl.BlockSpec((1, rows, width), lambda i, j: (i, j, 0))

    def tr(n):
        return pl.BlockSpec((1, n, rows), lambda i, j: (i, 0, j))

    heads_tr = pl.BlockSpec((1, MLA_HEADS, MLA_V, rows), lambda i, j: (i, 0, 0, j))
    in_specs = [tok(d), tok(p.shape[-1]), heads_tr, tr(MLA_WIDTH), tr(GLA_DV), tr(GLA_DV),
                tok(d), tok(d), full(wa), full(wb), full(wout), full(wple), full(gple), full(ggate),
                full(wgate), full(gfin)]
    return pl.pallas_call(
        functools.partial(_out_kernel, final=final),
        grid=(b, s // rows),
        in_specs=in_specs,
        out_specs=tok(d),
        out_shape=jax.ShapeDtypeStruct((b, s, d), F32),
        compiler_params=pltpu.CompilerParams(
            dimension_semantics=("parallel", "parallel"), vmem_limit_bytes=VMEM_LIMIT),
        name="out",
    )(x, p, om, gm, og, gg, ma, mb, wa, wb, wout, wple, gple, ggate, wgate, gfin)


def _swap_halves(w):
    half = w.shape[-1] // 2
    return jnp.concatenate([-w[..., half:], w[..., :half]], axis=-1)


def _rope_rows():
    inv_freq = 1.0 / (ROPE_THETA ** (jnp.arange(0, MLA_ROPE, 2, dtype=F32) / MLA_ROPE))
    zeros = jnp.zeros((MLA_NOPE,), F32)
    freq = jnp.concatenate([zeros, inv_freq, inv_freq, inv_freq, inv_freq])
    phase = jnp.concatenate([zeros, jnp.zeros((MLA_ROPE,), F32), jnp.full((MLA_ROPE,), np.pi / 2, F32)])
    return freq[None, :], phase[None, :]


def kernel(x, p, positions, norm_in_g, w_in, q_norm_g, w_uq, kv_norm_g, w_ukv, w_gk_up, b_gk,
           gla_norm_g, w_mla_br, w_gla_br, w_out, w_ple, ple_norm_g, ple_gate_norm_g, w_ple_gate,
           final_norm_g):
    b, s, d = x.shape
    depth = w_in.shape[0]
    assert s % PROJ_ROWS == 0 and PROJ_ROWS % TILE == 0
    pos3 = positions.reshape(b, s, 1)
    freq, phase = _rope_rows()
    seg_np, level_np = _gla_constants()
    seg = jnp.asarray(seg_np, BF16)
    level = jnp.asarray(level_np)

    sizes = (MLA_Q_RANK, MLA_KV_RANK, MLA_ROPE, MLA_WIDTH, GLA_DK, GLA_DK, GLA_DV, GLA_GATE_RANK,
             GLA_DV, d, d)
    offs = np.concatenate([[0], np.cumsum(sizes)])

    def cols(w, idx):
        return w[:, int(offs[idx]):int(offs[idx + 1])]

    for l in range(depth):
        w = w_in[l]
        wkr = cols(w, 2)
        wstd = jnp.concatenate(
            [cols(w, 0), cols(w, 1), jnp.zeros((d, MLA_NOPE), F32), wkr, _swap_halves(wkr),
             cols(w, 9), cols(w, 10)], axis=1).astype(BF16)
        wt = jnp.concatenate(
            [cols(w, 3), cols(w, 4) * (GLA_HEAD_K ** -0.5), cols(w, 5), cols(w, 6), cols(w, 8),
             cols(w, 7)], axis=1).T.astype(BF16)
        uq = w_uq[l].reshape(MLA_Q_RANK, MLA_HEADS, MLA_NOPE + MLA_ROPE)
        uq_r = uq[..., MLA_NOPE:]
        wq = jnp.concatenate([uq[..., :MLA_NOPE], uq_r, _swap_halves(uq_r)], axis=-1)
        wq = wq.reshape(MLA_Q_RANK, MLA_HEADS * HEAD_LANES).astype(BF16)
        ukv = w_ukv[l].reshape(MLA_KV_RANK, MLA_HEADS, MLA_NOPE + MLA_V)
        wk = jnp.concatenate([ukv[..., :MLA_NOPE], jnp.zeros_like(ukv[..., :MLA_NOPE])], axis=-1)
        wk = wk.reshape(MLA_KV_RANK, MLA_HEADS * HEAD_LANES).astype(BF16)
        wvt = ukv[..., MLA_NOPE:].reshape(MLA_KV_RANK, MLA_WIDTH).T.astype(BF16)

        (q, k, vt, gmla, gq, gk, gv, ggla, glow, ma, mb) = _projections(
            x, pos3, freq, phase, norm_in_g[l][None, :], wstd, wt, q_norm_g[l][None, :], wq,
            kv_norm_g[l][None, :], wk, wvt)

        o_mla = _mla_attention(q, k, vt)
        o_gla = _gla(gq, gk, gv, glow, w_gk_up[l].T.astype(BF16), b_gk[l][:, None], seg, level,
                     gla_norm_g[l][:, None])

        x = _output(x, p[l], o_mla, gmla, o_gla, ggla, ma, mb, w_mla_br[l].astype(BF16),
                    w_gla_br[l].astype(BF16), w_out[l].astype(BF16), w_ple[l].astype(BF16),
                    ple_norm_g[l][None, :], ple_gate_norm_g[l][None, :], w_ple_gate[l].astype(BF16),
                    final_norm_g[None, :], l == depth - 1)
    return x
```

```python
import functools

import numpy as np
import jax
import jax.numpy as jnp
from jax import lax
from jax.experimental import pallas as pl
from jax.experimental.pallas import tpu as pltpu

F32 = jnp.float32
BF16 = jnp.bfloat16

MLA_HEADS = 8
MLA_Q_RANK = 384
MLA_KV_RANK = 256
MLA_NOPE = 64
MLA_ROPE = 32
MLA_V = 64
MLA_WIDTH = MLA_HEADS * MLA_V
ROPE_THETA = 10000.0
GLA_HEADS = 4
GLA_HEAD_K = 64
GLA_HEAD_V = 128
GLA_DK = GLA_HEADS * GLA_HEAD_K
GLA_DV = GLA_HEADS * GLA_HEAD_V
GLA_GATE_RANK = 16
GLA_GATE_NORMALIZER = 16.0
EPS = 1e-6
LOG2E = 1.4426950408889634

HEAD_LANES = 128
TILE = 256
PROJ_ROWS = 512
MLA_HEADS_PER_STEP = 2
MLA_VX = MLA_V + 16
VMEM_LIMIT = 56 * 1024 * 1024

GLA_LEVELS = tuple(TILE >> (k + 1) for k in range(8))
GLA_NFACT = 2 + len(GLA_LEVELS)
GLA_DIAG = len(GLA_LEVELS)

NT = (((1,), (1,)), ((), ()))
TN = (((0,), (0,)), ((), ()))


def _dot(a, b):
    return jnp.dot(a, b, preferred_element_type=F32)


def _dot_nt(a, b):
    return lax.dot_general(a, b, NT, preferred_element_type=F32)


def _dot_tn(a, b):
    return lax.dot_general(a, b, TN, preferred_element_type=F32)


def _rms_rows(x, g):
    return x * lax.rsqrt(jnp.mean(x * x, axis=-1, keepdims=True) + EPS) * g


def _sigmoid(x):
    return 0.5 * jnp.tanh(0.5 * x) + 0.5


def _proj_kernel(x_ref, pos_ref, freq_ref, phase_ref, gin_ref, wstd_ref, wt_ref, gq_ref, wq_ref,
                 gkv_ref, wk_ref, wvt_ref,
                 q_out, k_out, vt_out, gmla_out, gq_out, gk_out, gv_out, ggla_out, glow_out,
                 ma_out, mb_out):
    h = _rms_rows(x_ref[0], gin_ref[...]).astype(BF16)

    pos = pos_ref[0].astype(F32)
    table = jnp.cos(pos * freq_ref[...] - phase_ref[...])

    cq = _rms_rows(_dot(h, wstd_ref[:, 0:384]), gq_ref[...]).astype(BF16)
    q = _dot(cq, wq_ref[...])
    qscale = table * ((MLA_NOPE + MLA_ROPE) ** -0.5 * LOG2E)
    for hh in range(MLA_HEADS):
        q_out[0, hh] = (q[:, hh * HEAD_LANES:(hh + 1) * HEAD_LANES] * qscale).astype(BF16)

    ckv = _rms_rows(_dot(h, wstd_ref[:, 384:640]), gkv_ref[...]).astype(BF16)
    u = _dot(h, wstd_ref[:, 640:768]) * table
    lane = lax.broadcasted_iota(jnp.int32, u.shape, 1)
    krot = jnp.where(lane >= MLA_NOPE, u + pltpu.roll(u, 32, 1) + pltpu.roll(u, 96, 1), 0.0)
    kk = _dot(ckv, wk_ref[...])
    for hh in range(MLA_HEADS):
        k_out[0, hh] = (kk[:, hh * HEAD_LANES:(hh + 1) * HEAD_LANES] + krot).astype(BF16)

    vt = _dot_nt(wvt_ref[...], ckv)
    for hh in range(MLA_HEADS):
        vt_out[0, hh] = vt[hh * MLA_V:(hh + 1) * MLA_V, :].astype(BF16)

    ma_out[0] = _dot(h, wstd_ref[:, 768:1792]).astype(BF16)
    mb_out[0] = _dot(h, wstd_ref[:, 1792:2816]).astype(BF16)

    for out, lo, hi in ((gmla_out, 0, 512), (gq_out, 512, 768), (gk_out, 768, 1024),
                        (gv_out, 1024, 1536), (ggla_out, 1536, 2048), (glow_out, 2048, 2064)):
        out[0] = _dot_nt(wt_ref[lo:hi, :], h).astype(BF16)


def _projections(x, pos3, freq, phase, gin, wstd, wt, gq, wq, gkv, wk, wvt):
    b, s, d = x.shape
    rows = PROJ_ROWS

    def full(a):
        return pl.BlockSpec(a.shape, lambda i, j: (0,) * a.ndim)

    def tok(width):
        return pl.BlockSpec((1, rows, width), lambda i, j: (i, j, 0))

    def tr(n):
        return pl.BlockSpec((1, n, rows), lambda i, j: (i, 0, j))

    def tr_shape(n):
        return jax.ShapeDtypeStruct((b, n, s), BF16)

    heads_tok = pl.BlockSpec((1, MLA_HEADS, rows, HEAD_LANES), lambda i, j: (i, 0, j, 0))
    heads_tr = pl.BlockSpec((1, MLA_HEADS, MLA_V, rows), lambda i, j: (i, 0, 0, j))
    out_shape = (
        jax.ShapeDtypeStruct((b, MLA_HEADS, s, HEAD_LANES), BF16),
        jax.ShapeDtypeStruct((b, MLA_HEADS, s, HEAD_LANES), BF16),
        jax.ShapeDtypeStruct((b, MLA_HEADS, MLA_V, s), BF16),
        tr_shape(MLA_WIDTH),
        tr_shape(GLA_DK),
        tr_shape(GLA_DK),
        tr_shape(GLA_DV),
        tr_shape(GLA_DV),
        tr_shape(GLA_GATE_RANK),
        jax.ShapeDtypeStruct((b, s, d), BF16),
        jax.ShapeDtypeStruct((b, s, d), BF16),
    )
    out_specs = (heads_tok, heads_tok, heads_tr, tr(MLA_WIDTH),
                 tr(GLA_DK), tr(GLA_DK), tr(GLA_DV), tr(GLA_DV), tr(GLA_GATE_RANK), tok(d), tok(d))
    in_specs = [tok(d), tok(1), full(freq), full(phase), full(gin), full(wstd), full(wt), full(gq),
                full(wq), full(gkv), full(wk), full(wvt)]
    return pl.pallas_call(
        _proj_kernel,
        grid=(b, s // rows),
        in_specs=in_specs,
        out_specs=out_specs,
        out_shape=out_shape,
        compiler_params=pltpu.CompilerParams(
            dimension_semantics=("parallel", "parallel"), vmem_limit_bytes=VMEM_LIMIT),
        name="proj",
    )(x, pos3, freq, phase, gin, wstd, wt, gq, wq, gkv, wk, wvt)


def _mla_kernel(q_ref, k_ref, vt_ref, o_ref, s_ref, p_ref, vx_ref):
    heads, seq = q_ref.shape[1], q_ref.shape[2]
    row = lax.broadcasted_iota(jnp.int32, (TILE, TILE), 0)
    col = lax.broadcasted_iota(jnp.int32, (TILE, TILE), 1)
    causal = row <= col
    tasks = [(hh, qi) for hh in range(heads) for qi in range(seq // TILE)]

    for hh in range(heads):
        vx_ref[hh, 0:MLA_V, :] = vt_ref[0, hh]
        vx_ref[hh, MLA_V:, :] = jnp.ones((MLA_VX - MLA_V, seq), BF16)

    def scores(t):
        hh, qi = tasks[t]
        lo, hi = qi * TILE, (qi + 1) * TILE
        qt = q_ref[0, hh, lo:hi, :]
        if qi:
            s_ref[t % 2, 0:lo, :] = _dot_nt(k_ref[0, hh, 0:lo, :], qt)
        s_ref[t % 2, lo:hi, :] = jnp.where(causal, _dot_nt(k_ref[0, hh, lo:hi, :], qt), -jnp.inf)

    def softmax(t):
        _, qi = tasks[t]
        m = jnp.full((1, TILE), -jnp.inf, F32)
        for kj in range(qi + 1):
            m = jnp.maximum(m, jnp.max(s_ref[t % 2, kj * TILE:(kj + 1) * TILE, :], axis=0, keepdims=True))
        for kj in range(qi + 1):
            blk = slice(kj * TILE, (kj + 1) * TILE)
            p_ref[t % 2, blk, :] = jnp.exp2(s_ref[t % 2, blk, :] - m).astype(BF16)

    def weighted_values(t):
        hh, qi = tasks[t]
        lo, hi = qi * TILE, (qi + 1) * TILE
        acc = _dot(vx_ref[hh, :, 0:hi], p_ref[t % 2, 0:hi, :])
        o_ref[0, hh, :, lo:hi] = (acc[0:MLA_V] / acc[MLA_V:MLA_V + 1]).astype(BF16)

    scores(0)
    for t in range(len(tasks)):
        if t + 1 < len(tasks):
            scores(t + 1)
        if t:
            weighted_values(t - 1)
        softmax(t)
    weighted_values(len(tasks) - 1)


def _mla_attention(q, k, vt):
    b, heads, s, _ = q.shape
    hp = MLA_HEADS_PER_STEP
    tok = pl.BlockSpec((1, hp, s, HEAD_LANES), lambda i, h: (i, h, 0, 0))
    tr = pl.BlockSpec((1, hp, MLA_V, s), lambda i, h: (i, h, 0, 0))
    return pl.pallas_call(
        _mla_kernel,
        grid=(b, heads // hp),
        in_specs=[tok, tok, tr],
        out_specs=tr,
        out_shape=jax.ShapeDtypeStruct((b, heads, MLA_V, s), BF16),
        scratch_shapes=[pltpu.VMEM((2, s, TILE), F32), pltpu.VMEM((2, s, TILE), BF16),
                        pltpu.VMEM((hp, MLA_VX, s), BF16)],
        compiler_params=pltpu.CompilerParams(
            dimension_semantics=("parallel", "parallel"), vmem_limit_bytes=VMEM_LIMIT),
        name="mla",
    )(q, k, vt)


def _gla_constants():
    u = np.arange(TILE)[:, None]
    t = np.arange(TILE)[None, :]
    mats = [(u <= t), (u > t)]
    for m in GLA_LEVELS:
        pos = t % (2 * m)
        mid = t - pos + m - 1
        upper = pos >= m
        mats.append(np.where(upper, (u > mid) & (u <= t), (u > t) & (u <= mid)))
    seg = np.concatenate([mm.astype(np.float32) for mm in mats], axis=1)
    seg = np.concatenate([seg, seg, seg], axis=0)
    j = np.arange(TILE)[:, None]
    i = np.arange(TILE)[None, :]
    level = np.full((TILE, TILE), -1, np.int32)
    level[j == i] = GLA_DIAG
    for idx, m in enumerate(GLA_LEVELS):
        same = (j // (2 * m)) == (i // (2 * m))
        split = ((j % (2 * m)) < m) & ((i % (2 * m)) >= m)
        level[same & split & (j < i)] = idx
    return seg, level


def _gla_kernel(qt_ref, kt_ref, vt_ref, glow_ref, wg_ref, bg_ref, seg_ref, level_ref, gn_ref,
                o_ref, state_ref, fact_ref):
    @pl.when(pl.program_id(1) == 0)
    def _():
        state_ref[...] = jnp.zeros_like(state_ref)

    z = _dot(wg_ref[...], glow_ref[0]) + bg_ref[...]
    g = -(jnp.maximum(-z, 0.0) + jnp.log1p(jnp.exp(-jnp.abs(z)))) * (1.0 / GLA_GATE_NORMALIZER)
    g_hi = g.astype(BF16)
    r = g - g_hi.astype(F32)
    g_mid = r.astype(BF16)
    g_lo = (r - g_mid.astype(F32)).astype(BF16)
    g3 = jnp.concatenate([g_hi, g_mid, g_lo], axis=1)
    for n in range(GLA_NFACT):
        fact_ref[n] = jnp.exp(_dot(g3, seg_ref[:, n * TILE:(n + 1) * TILE]))

    level = level_ref[...]
    lane = lax.broadcasted_iota(jnp.int32, (GLA_HEAD_K, TILE), 1)
    for hh in range(GLA_HEADS):
        rk = slice(hh * GLA_HEAD_K, (hh + 1) * GLA_HEAD_K)
        rv = slice(hh * GLA_HEAD_V, (hh + 1) * GLA_HEAD_V)
        q16 = qt_ref[0, rk, :]
        k16 = kt_ref[0, rk, :]
        qf = q16.astype(F32)
        kf = k16.astype(F32)
        vt = vt_ref[0, rv, :]
        f_in = fact_ref[0, rk, :]
        q_in = (qf * f_in).astype(BF16)
        k_out = (kf * fact_ref[1, rk, :]).astype(BF16)
        state = state_ref[hh]
        o = _dot_tn(state.astype(BF16), q_in)

        at = jnp.where(level == GLA_DIAG, _dot_tn(k16, q16), 0.0)
        for idx, m in enumerate(GLA_LEVELS):
            f = fact_ref[2 + idx, rk, :]
            upper = (lane // m) % 2 == 1
            q_up = jnp.where(upper, qf * f, 0.0).astype(BF16)
            k_lo = jnp.where(upper, 0.0, kf * f).astype(BF16)
            at = at + jnp.where(level == idx, _dot_tn(k_lo, q_up), 0.0)
        o = o + _dot(vt, at.astype(BF16))

        ms = jnp.mean(o * o, axis=0, keepdims=True)
        o_ref[0, rv, :] = (o * lax.rsqrt(ms + EPS) * gn_ref[...]).astype(BF16)

        state_ref[hh] = state * f_in[:, TILE - 1:TILE] + _dot_nt(k_out, vt)


def _gla(qt, kt, vt, glow, wg, bg, seg, level, gn):
    b, _, s = qt.shape

    def full(a):
        return pl.BlockSpec(a.shape, lambda i, j: (0,) * a.ndim)

    def tr(n):
        return pl.BlockSpec((1, n, TILE), lambda i, j: (i, 0, j))

    return pl.pallas_call(
        _gla_kernel,
        grid=(b, s // TILE),
        in_specs=[tr(GLA_DK), tr(GLA_DK), tr(GLA_DV), tr(GLA_GATE_RANK), full(wg), full(bg),
                  full(seg), full(level), full(gn)],
        out_specs=tr(GLA_DV),
        out_shape=jax.ShapeDtypeStruct((b, GLA_DV, s), BF16),
        scratch_shapes=[pltpu.VMEM((GLA_HEADS, GLA_HEAD_K, GLA_HEAD_V), F32),
                        pltpu.VMEM((GLA_NFACT, GLA_DK, TILE), F32)],
        compiler_params=pltpu.CompilerParams(
            dimension_semantics=("parallel", "arbitrary"), vmem_limit_bytes=VMEM_LIMIT),
        name="gla",
    )(qt, kt, vt, glow, wg, bg, seg, level, gn)


def _out_kernel(x_ref, p_ref, om_ref, gm_ref, og_ref, gg_ref, ma_ref, mb_ref, wa_ref, wb_ref,
                wout_ref, wple_ref, gple_ref, ggate_ref, wgate_ref, gfin_ref, o_ref, *, final):
    def branch(o_t, g_t_ref, w_ref):
        gate = g_t_ref[0].astype(F32)
        zt = (o_t.astype(F32) * (gate * _sigmoid(gate))).astype(BF16)
        return _dot_tn(zt, w_ref[...])

    ya = branch(om_ref[0].reshape(MLA_WIDTH, PROJ_ROWS), gm_ref, wa_ref)
    yb = branch(og_ref[0], gg_ref, wb_ref)
    merged = _sigmoid(ma_ref[0].astype(F32)) * ya + _sigmoid(mb_ref[0].astype(F32)) * yb
    x1 = x_ref[0] + _dot(merged.astype(BF16), wout_ref[...])
    e = _rms_rows(_dot(p_ref[0].astype(BF16), wple_ref[...]), gple_ref[...])
    gate = _sigmoid(_dot(_rms_rows(x1, ggate_ref[...]).astype(BF16), wgate_ref[...]))
    x2 = x1 + gate * e
    o_ref[0] = _rms_rows(x2, gfin_ref[...]) if final else x2


def _output(x, p, om, gm, og, gg, ma, mb, wa, wb, wout, wple, gple, ggate, wgate, gfin, final):
    b, s, d = x.shape
    rows = PROJ_ROWS

    def full(a):
        return pl.BlockSpec(a.shape, lambda i, j: (0,) * a.ndim)

    def tok(width):
        return pl.BlockSpec((1, rows, width), lambda i, j: (i, j, 0))

    def tr(n):
        return pl.BlockSpec((1, n, rows), lambda i, j: (i, 0, j))

    heads_tr = pl.BlockSpec((1, MLA_HEADS, MLA_V, rows), lambda i, j: (i, 0, 0, j))
    in_specs = [tok(d), tok(p.shape[-1]), heads_tr, tr(MLA_WIDTH), tr(GLA_DV), tr(GLA_DV),
                tok(d), tok(d), full(wa), full(wb), full(wout), full(wple), full(gple), full(ggate),
                full(wgate), full(gfin)]
    return pl.pallas_call(
        functools.partial(_out_kernel, final=final),
        grid=(b, s // rows),
        in_specs=in_specs,
        out_specs=tok(d),
        out_shape=jax.ShapeDtypeStruct((b, s, d), F32),
        compiler_params=pltpu.CompilerParams(
            dimension_semantics=("parallel", "parallel"), vmem_limit_bytes=VMEM_LIMIT),
        name="out",
    )(x, p, om, gm, og, gg, ma, mb, wa, wb, wout, wple, gple, ggate, wgate, gfin)


def _swap_halves(w):
    half = w.shape[-1] // 2
    return jnp.concatenate([-w[..., half:], w[..., :half]], axis=-1)


def _rope_rows():
    inv_freq = 1.0 / (ROPE_THETA ** (jnp.arange(0, MLA_ROPE, 2, dtype=F32) / MLA_ROPE))
    zeros = jnp.zeros((MLA_NOPE,), F32)
    freq = jnp.concatenate([zeros, inv_freq, inv_freq, inv_freq, inv_freq])
    phase = jnp.concatenate([zeros, jnp.zeros((MLA_ROPE,), F32), jnp.full((MLA_ROPE,), np.pi / 2, F32)])
    return freq[None, :], phase[None, :]


def kernel(x, p, positions, norm_in_g, w_in, q_norm_g, w_uq, kv_norm_g, w_ukv, w_gk_up, b_gk,
           gla_norm_g, w_mla_br, w_gla_br, w_out, w_ple, ple_norm_g, ple_gate_norm_g, w_ple_gate,
           final_norm_g):
    b, s, d = x.shape
    depth = w_in.shape[0]
    assert s % PROJ_ROWS == 0 and PROJ_ROWS % TILE == 0
    pos3 = positions.reshape(b, s, 1)
    freq, phase = _rope_rows()
    seg_np, level_np = _gla_constants()
    seg = jnp.asarray(seg_np, BF16)
    level = jnp.asarray(level_np)

    sizes = (MLA_Q_RANK, MLA_KV_RANK, MLA_ROPE, MLA_WIDTH, GLA_DK, GLA_DK, GLA_DV, GLA_GATE_RANK,
             GLA_DV, d, d)
    offs = np.concatenate([[0], np.cumsum(sizes)])

    def cols(w, idx):
        return w[:, int(offs[idx]):int(offs[idx + 1])]

    for l in range(depth):
        w = w_in[l]
        wkr = cols(w, 2)
        wstd = jnp.concatenate(
            [cols(w, 0), cols(w, 1), jnp.zeros((d, MLA_NOPE), F32), wkr, _swap_halves(wkr),
             cols(w, 9), cols(w, 10)], axis=1).astype(BF16)
        wt = jnp.concatenate(
            [cols(w, 3), cols(w, 4) * (GLA_HEAD_K ** -0.5), cols(w, 5), cols(w, 6), cols(w, 8),
             cols(w, 7)], axis=1).T.astype(BF16)
        uq = w_uq[l].reshape(MLA_Q_RANK, MLA_HEADS, MLA_NOPE + MLA_ROPE)
        uq_r = uq[..., MLA_NOPE:]
        wq = jnp.concatenate([uq[..., :MLA_NOPE], uq_r, _swap_halves(uq_r)], axis=-1)
        wq = wq.reshape(MLA_Q_RANK, MLA_HEADS * HEAD_LANES).astype(BF16)
        ukv = w_ukv[l].reshape(MLA_KV_RANK, MLA_HEADS, MLA_NOPE + MLA_V)
        wk = jnp.concatenate([ukv[..., :MLA_NOPE], jnp.zeros_like(ukv[..., :MLA_NOPE])], axis=-1)
        wk = wk.reshape(MLA_KV_RANK, MLA_HEADS * HEAD_LANES).astype(BF16)
        wvt = ukv[..., MLA_NOPE:].reshape(MLA_KV_RANK, MLA_WIDTH).T.astype(BF16)

        (q, k, vt, gmla, gq, gk, gv, ggla, glow, ma, mb) = _projections(
            x, pos3, freq, phase, norm_in_g[l][None, :], wstd, wt, q_norm_g[l][None, :], wq,
            kv_norm_g[l][None, :], wk, wvt)

        o_mla = _mla_attention(q, k, vt)
        o_gla = _gla(gq, gk, gv, glow, w_gk_up[l].T.astype(BF16), b_gk[l][:, None], seg, level,
                     gla_norm_g[l][:, None])

        x = _output(x, p[l], o_mla, gmla, o_gla, ggla, ma, mb, w_mla_br[l].astype(BF16),
                    w_gla_br[l].astype(BF16), w_out[l].astype(BF16), w_ple[l].astype(BF16),
                    ple_norm_g[l][None, :], ple_gate_norm_g[l][None, :], w_ple_gate[l].astype(BF16),
                    final_norm_g[None, :], l == depth - 1)
    return x
```

```python
import functools

import numpy as np
import jax
import jax.numpy as jnp
from jax import lax
from jax.experimental import pallas as pl
from jax.experimental.pallas import tpu as pltpu

F32 = jnp.float32
BF16 = jnp.bfloat16

MLA_HEADS = 8
MLA_Q_RANK = 384
MLA_KV_RANK = 256
MLA_NOPE = 64
MLA_ROPE = 32
MLA_V = 64
MLA_WIDTH = MLA_HEADS * MLA_V
ROPE_THETA = 10000.0
GLA_HEADS = 4
GLA_HEAD_K = 64
GLA_HEAD_V = 128
GLA_DK = GLA_HEADS * GLA_HEAD_K
GLA_DV = GLA_HEADS * GLA_HEAD_V
GLA_GATE_RANK = 16
GLA_GATE_NORMALIZER = 16.0
EPS = 1e-6
LOG2E = 1.4426950408889634

HEAD_LANES = 128
TILE = 256
PROJ_ROWS = 512
MLA_HEADS_PER_STEP = 2
MLA_VX = MLA_V + 16
VMEM_LIMIT = 56 * 1024 * 1024

GLA_BLOCK = 16
GLA_LEVELS = (128, 64, 32, 16)
GLA_NFACT = 2 + len(GLA_LEVELS)
GLA_BAND = len(GLA_LEVELS)

NT = (((1,), (1,)), ((), ()))
TN = (((0,), (0,)), ((), ()))


def _dot(a, b):
    return jnp.dot(a, b, preferred_element_type=F32)


def _dot_nt(a, b):
    return lax.dot_general(a, b, NT, preferred_element_type=F32)


def _dot_tn(a, b):
    return lax.dot_general(a, b, TN, preferred_element_type=F32)


def _rms_rows(x, g):
    return x * lax.rsqrt(jnp.mean(x * x, axis=-1, keepdims=True) + EPS) * g


def _sigmoid(x):
    return 0.5 * jnp.tanh(0.5 * x) + 0.5


def _proj_kernel(x_ref, pos_ref, freq_ref, phase_ref, gin_ref, wstd_ref, wt_ref, gq_ref, wq_ref,
                 gkv_ref, wk_ref, wvt_ref,
                 q_out, k_out, vt_out, gmla_out, gq_out, gk_out, gv_out, ggla_out, glow_out,
                 ma_out, mb_out):
    h = _rms_rows(x_ref[0], gin_ref[...]).astype(BF16)

    pos = pos_ref[0].astype(F32)
    table = jnp.cos(pos * freq_ref[...] - phase_ref[...])

    cq = _rms_rows(_dot(h, wstd_ref[:, 0:384]), gq_ref[...]).astype(BF16)
    q = _dot(cq, wq_ref[...])
    qscale = table * ((MLA_NOPE + MLA_ROPE) ** -0.5 * LOG2E)
    for hh in range(MLA_HEADS):
        q_out[0, hh] = (q[:, hh * HEAD_LANES:(hh + 1) * HEAD_LANES] * qscale).astype(BF16)

    ckv = _rms_rows(_dot(h, wstd_ref[:, 384:640]), gkv_ref[...]).astype(BF16)
    u = _dot(h, wstd_ref[:, 640:768]) * table
    lane = lax.broadcasted_iota(jnp.int32, u.shape, 1)
    krot = jnp.where(lane >= MLA_NOPE, u + pltpu.roll(u, 32, 1) + pltpu.roll(u, 96, 1), 0.0)
    kk = _dot(ckv, wk_ref[...])
    for hh in range(MLA_HEADS):
        k_out[0, hh] = (kk[:, hh * HEAD_LANES:(hh + 1) * HEAD_LANES] + krot).astype(BF16)

    vt = _dot_nt(wvt_ref[...], ckv)
    for hh in range(MLA_HEADS):
        vt_out[0, hh] = vt[hh * MLA_V:(hh + 1) * MLA_V, :].astype(BF16)

    ma_out[0] = _dot(h, wstd_ref[:, 768:1792]).astype(BF16)
    mb_out[0] = _dot(h, wstd_ref[:, 1792:2816]).astype(BF16)

    for out, lo, hi in ((gmla_out, 0, 512), (gq_out, 512, 768), (gk_out, 768, 1024),
                        (gv_out, 1024, 1536), (ggla_out, 1536, 2048), (glow_out, 2048, 2064)):
        out[0] = _dot_nt(wt_ref[lo:hi, :], h).astype(BF16)


def _projections(x, pos3, freq, phase, gin, wstd, wt, gq, wq, gkv, wk, wvt):
    b, s, d = x.shape
    rows = PROJ_ROWS

    def full(a):
        return pl.BlockSpec(a.shape, lambda i, j: (0,) * a.ndim)

    def tok(width):
        return pl.BlockSpec((1, rows, width), lambda i, j: (i, j, 0))

    def tr(n):
        return pl.BlockSpec((1, n, rows), lambda i, j: (i, 0, j))

    def tr_shape(n):
        return jax.ShapeDtypeStruct((b, n, s), BF16)

    heads_tok = pl.BlockSpec((1, MLA_HEADS, rows, HEAD_LANES), lambda i, j: (i, 0, j, 0))
    heads_tr = pl.BlockSpec((1, MLA_HEADS, MLA_V, rows), lambda i, j: (i, 0, 0, j))
    out_shape = (
        jax.ShapeDtypeStruct((b, MLA_HEADS, s, HEAD_LANES), BF16),
        jax.ShapeDtypeStruct((b, MLA_HEADS, s, HEAD_LANES), BF16),
        jax.ShapeDtypeStruct((b, MLA_HEADS, MLA_V, s), BF16),
        tr_shape(MLA_WIDTH),
        tr_shape(GLA_DK),
        tr_shape(GLA_DK),
        tr_shape(GLA_DV),
        tr_shape(GLA_DV),
        tr_shape(GLA_GATE_RANK),
        jax.ShapeDtypeStruct((b, s, d), BF16),
        jax.ShapeDtypeStruct((b, s, d), BF16),
    )
    out_specs = (heads_tok, heads_tok, heads_tr, tr(MLA_WIDTH),
                 tr(GLA_DK), tr(GLA_DK), tr(GLA_DV), tr(GLA_DV), tr(GLA_GATE_RANK), tok(d), tok(d))
    in_specs = [tok(d), tok(1), full(freq), full(phase), full(gin), full(wstd), full(wt), full(gq),
                full(wq), full(gkv), full(wk), full(wvt)]
    return pl.pallas_call(
        _proj_kernel,
        grid=(b, s // rows),
        in_specs=in_specs,
        out_specs=out_specs,
        out_shape=out_shape,
        compiler_params=pltpu.CompilerParams(
            dimension_semantics=("parallel", "parallel"), vmem_limit_bytes=VMEM_LIMIT),
        name="proj",
    )(x, pos3, freq, phase, gin, wstd, wt, gq, wq, gkv, wk, wvt)


def _mla_kernel(q_ref, k_ref, vt_ref, o_ref, s_ref, p_ref, vx_ref):
    heads, seq = q_ref.shape[1], q_ref.shape[2]
    row = lax.broadcasted_iota(jnp.int32, (TILE, TILE), 0)
    col = lax.broadcasted_iota(jnp.int32, (TILE, TILE), 1)
    causal = row <= col
    tasks = [(hh, qi) for hh in range(heads) for qi in range(seq // TILE)]

    for hh in range(heads):
        vx_ref[hh, 0:MLA_V, :] = vt_ref[0, hh]
        vx_ref[hh, MLA_V:, :] = jnp.ones((MLA_VX - MLA_V, seq), BF16)

    def scores(t):
        hh, qi = tasks[t]
        lo, hi = qi * TILE, (qi + 1) * TILE
        qt = q_ref[0, hh, lo:hi, :]
        if qi:
            s_ref[t % 2, 0:lo, :] = _dot_nt(k_ref[0, hh, 0:lo, :], qt)
        s_ref[t % 2, lo:hi, :] = jnp.where(causal, _dot_nt(k_ref[0, hh, lo:hi, :], qt), -jnp.inf)

    def softmax(t):
        _, qi = tasks[t]
        m = jnp.full((1, TILE), -jnp.inf, F32)
        for kj in range(qi + 1):
            m = jnp.maximum(m, jnp.max(s_ref[t % 2, kj * TILE:(kj + 1) * TILE, :], axis=0, keepdims=True))
        for kj in range(qi + 1):
            blk = slice(kj * TILE, (kj + 1) * TILE)
            p_ref[t % 2, blk, :] = jnp.exp2(s_ref[t % 2, blk, :] - m).astype(BF16)

    def weighted_values(t):
        hh, qi = tasks[t]
        lo, hi = qi * TILE, (qi + 1) * TILE
        acc = _dot(vx_ref[hh, :, 0:hi], p_ref[t % 2, 0:hi, :])
        o_ref[0, hh, :, lo:hi] = (acc[0:MLA_V] / acc[MLA_V:MLA_V + 1]).astype(BF16)

    scores(0)
    for t in range(len(tasks)):
        if t + 1 < len(tasks):
            scores(t + 1)
        if t:
            weighted_values(t - 1)
        softmax(t)
    weighted_values(len(tasks) - 1)


def _mla_attention(q, k, vt):
    b, heads, s, _ = q.shape
    hp = MLA_HEADS_PER_STEP
    tok = pl.BlockSpec((1, hp, s, HEAD_LANES), lambda i, h: (i, h, 0, 0))
    tr = pl.BlockSpec((1, hp, MLA_V, s), lambda i, h: (i, h, 0, 0))
    return pl.pallas_call(
        _mla_kernel,
        grid=(b, heads // hp),
        in_specs=[tok, tok, tr],
        out_specs=tr,
        out_shape=jax.ShapeDtypeStruct((b, heads, MLA_V, s), BF16),
        scratch_shapes=[pltpu.VMEM((2, s, TILE), F32), pltpu.VMEM((2, s, TILE), BF16),
                        pltpu.VMEM((hp, MLA_VX, s), BF16)],
        compiler_params=pltpu.CompilerParams(
            dimension_semantics=("parallel", "parallel"), vmem_limit_bytes=VMEM_LIMIT),
        name="mla",
    )(q, k, vt)


def _gla_constants():
    u = np.arange(TILE)[:, None]
    t = np.arange(TILE)[None, :]
    mats = [(u <= t), (u > t)]
    for m in GLA_LEVELS:
        pos = t % (2 * m)
        mid = t - pos + m - 1
        upper = pos >= m
        mats.append(np.where(upper, (u > mid) & (u <= t), (u > t) & (u <= mid)))
    seg = np.concatenate([mm.astype(np.float32) for mm in mats], axis=1)
    seg = np.concatenate([seg, seg, seg], axis=0)
    j = np.arange(TILE)[:, None]
    i = np.arange(TILE)[None, :]
    level = np.full((TILE, TILE), -1, np.int32)
    level[(j // GLA_BLOCK) == (i // GLA_BLOCK)] = GLA_BAND
    for idx, m in enumerate(GLA_LEVELS):
        same = (j // (2 * m)) == (i // (2 * m))
        split = ((j % (2 * m)) < m) & ((i % (2 * m)) >= m)
        level[same & split & (j < i)] = idx
    return seg, level


def _gla_band(qf, kf, decay):
    rel = (lax.broadcasted_iota(jnp.int32, (GLA_BLOCK, TILE), 1) % GLA_BLOCK
           - lax.broadcasted_iota(jnp.int32, (GLA_BLOCK, TILE), 0))
    w = [kf]
    span = decay
    step = 1
    while step < GLA_BLOCK:
        w += [span * pltpu.roll(w[d], step, 1) for d in range(step)]
        span = span * pltpu.roll(span, step, 1)
        step *= 2
    bands = [jnp.zeros((GLA_BLOCK, TILE), F32) for _ in range(GLA_HEADS)]
    for d in range(GLA_BLOCK):
        prod = qf * w[d]
        for hh in range(GLA_HEADS):
            diag = jnp.sum(prod[hh * GLA_HEAD_K:(hh + 1) * GLA_HEAD_K], axis=0, keepdims=True)
            bands[hh] = jnp.where(rel == d, diag, bands[hh])
    return bands


def _gla_kernel(qt_ref, kt_ref, vt_ref, glow_ref, wg_ref, bg_ref, seg_ref, level_ref, gn_ref,
                o_ref, state_ref, fact_ref):
    @pl.when(pl.program_id(1) == 0)
    def _():
        state_ref[...] = jnp.zeros_like(state_ref)

    z = _dot(wg_ref[...], glow_ref[0]) + bg_ref[...]
    g = -(jnp.maximum(-z, 0.0) + jnp.log1p(jnp.exp(-jnp.abs(z)))) * (1.0 / GLA_GATE_NORMALIZER)
    g_hi = g.astype(BF16)
    r = g - g_hi.astype(F32)
    g_mid = r.astype(BF16)
    g_lo = (r - g_mid.astype(F32)).astype(BF16)
    g3 = jnp.concatenate([g_hi, g_mid, g_lo], axis=1)
    for n in range(GLA_NFACT):
        fact_ref[n] = jnp.exp(_dot(g3, seg_ref[:, n * TILE:(n + 1) * TILE]))
    bands = _gla_band(qt_ref[0].astype(F32), kt_ref[0].astype(F32), jnp.exp(g))

    level = level_ref[...]
    lane = lax.broadcasted_iota(jnp.int32, (GLA_HEAD_K, TILE), 1)
    for hh in range(GLA_HEADS):
        rk = slice(hh * GLA_HEAD_K, (hh + 1) * GLA_HEAD_K)
        rv = slice(hh * GLA_HEAD_V, (hh + 1) * GLA_HEAD_V)
        qf = qt_ref[0, rk, :].astype(F32)
        kf = kt_ref[0, rk, :].astype(F32)
        vt = vt_ref[0, rv, :]
        f_in = fact_ref[0, rk, :]
        q_in = (qf * f_in).astype(BF16)
        k_out = (kf * fact_ref[1, rk, :]).astype(BF16)
        state = state_ref[hh]
        o = _dot_tn(state.astype(BF16), q_in)

        at = jnp.where(level == GLA_BAND, jnp.tile(bands[hh], (TILE // GLA_BLOCK, 1)), 0.0)
        for idx, m in enumerate(GLA_LEVELS):
            f = fact_ref[2 + idx, rk, :]
            upper = (lane // m) % 2 == 1
            q_up = jnp.where(upper, qf * f, 0.0).astype(BF16)
            k_lo = jnp.where(upper, 0.0, kf * f).astype(BF16)
            at = jnp.where(level == idx, _dot_tn(k_lo, q_up), at)
        o = o + _dot(vt, at.astype(BF16))

        ms = jnp.mean(o * o, axis=0, keepdims=True)
        o_ref[0, rv, :] = (o * lax.rsqrt(ms + EPS) * gn_ref[...]).astype(BF16)

        state_ref[hh] = state * f_in[:, TILE - 1:TILE] + _dot_nt(k_out, vt)


def _gla(qt, kt, vt, glow, wg, bg, seg, level, gn):
    b, _, s = qt.shape

    def full(a):
        return pl.BlockSpec(a.shape, lambda i, j: (0,) * a.ndim)

    def tr(n):
        return pl.BlockSpec((1, n, TILE), lambda i, j: (i, 0, j))

    return pl.pallas_call(
        _gla_kernel,
        grid=(b, s // TILE),
        in_specs=[tr(GLA_DK), tr(GLA_DK), tr(GLA_DV), tr(GLA_GATE_RANK), full(wg), full(bg),
                  full(seg), full(level), full(gn)],
        out_specs=tr(GLA_DV),
        out_shape=jax.ShapeDtypeStruct((b, GLA_DV, s), BF16),
        scratch_shapes=[pltpu.VMEM((GLA_HEADS, GLA_HEAD_K, GLA_HEAD_V), F32),
                        pltpu.VMEM((GLA_NFACT, GLA_DK, TILE), F32)],
        compiler_params=pltpu.CompilerParams(
            dimension_semantics=("parallel", "arbitrary"), vmem_limit_bytes=VMEM_LIMIT),
        name="gla",
    )(qt, kt, vt, glow, wg, bg, seg, level, gn)


def _out_kernel(x_ref, p_ref, om_ref, gm_ref, og_ref, gg_ref, ma_ref, mb_ref, wa_ref, wb_ref,
                wout_ref, wple_ref, gple_ref, ggate_ref, wgate_ref, gfin_ref, o_ref, *, final):
    def branch(o_t, g_t_ref, w_ref):
        gate = g_t_ref[0].astype(F32)
        zt = (o_t.astype(F32) * (gate * _sigmoid(gate))).astype(BF16)
        return _dot_tn(zt, w_ref[...])

    ya = branch(om_ref[0].reshape(MLA_WIDTH, PROJ_ROWS), gm_ref, wa_ref)
    yb = branch(og_ref[0], gg_ref, wb_ref)
    merged = _sigmoid(ma_ref[0].astype(F32)) * ya + _sigmoid(mb_ref[0].astype(F32)) * yb
    x1 = x_ref[0] + _dot(merged.astype(BF16), wout_ref[...])
    e = _rms_rows(_dot(p_ref[0].astype(BF16), wple_ref[...]), gple_ref[...])
    gate = _sigmoid(_dot(_rms_rows(x1, ggate_ref[...]).astype(BF16), wgate_ref[...]))
    x2 = x1 + gate * e
    o_ref[0] = _rms_rows(x2, gfin_ref[...]) if final else x2


def _output(x, p, om, gm, og, gg, ma, mb, wa, wb, wout, wple, gple, ggate, wgate, gfin, final):
    b, s, d = x.shape
    rows = PROJ_ROWS

    def full(a):
        return pl.BlockSpec(a.shape, lambda i, j: (0,) * a.ndim)

    def tok(width):
        return pl.BlockSpec((1, rows, width), lambda i, j: (i, j, 0))

    def tr(n):
        return pl.BlockSpec((1, n, rows), lambda i, j: (i, 0, j))

    heads_tr = pl.BlockSpec((1, MLA_HEADS, MLA_V, rows), lambda i, j: (i, 0, 0, j))
    in_specs = [tok(d), tok(p.shape[-1]), heads_tr, tr(MLA_WIDTH), tr(GLA_DV), tr(GLA_DV),
                tok(d), tok(d), full(wa), full(wb), full(wout), full(wple), full(gple), full(ggate),
                full(wgate), full(gfin)]
    return pl.pallas_call(
        functools.partial(_out_kernel, final=final),
        grid=(b, s // rows),
        in_specs=in_specs,
        out_specs=tok(d),
        out_shape=jax.ShapeDtypeStruct((b, s, d), F32),
        compiler_params=pltpu.CompilerParams(
            dimension_semantics=("parallel", "parallel"), vmem_limit_bytes=VMEM_LIMIT),
        name="out",
    )(x, p, om, gm, og, gg, ma, mb, wa, wb, wout, wple, gple, ggate, wgate, gfin)


def _swap_halves(w):
    half = w.shape[-1] // 2
    return jnp.concatenate([-w[..., half:], w[..., :half]], axis=-1)


def _rope_rows():
    inv_freq = 1.0 / (ROPE_THETA ** (jnp.arange(0, MLA_ROPE, 2, dtype=F32) / MLA_ROPE))
    zeros = jnp.zeros((MLA_NOPE,), F32)
    freq = jnp.concatenate([zeros, inv_freq, inv_freq, inv_freq, inv_freq])
    phase = jnp.concatenate([zeros, jnp.zeros((MLA_ROPE,), F32), jnp.full((MLA_ROPE,), np.pi / 2, F32)])
    return freq[None, :], phase[None, :]


def kernel(x, p, positions, norm_in_g, w_in, q_norm_g, w_uq, kv_norm_g, w_ukv, w_gk_up, b_gk,
           gla_norm_g, w_mla_br, w_gla_br, w_out, w_ple, ple_norm_g, ple_gate_norm_g, w_ple_gate,
           final_norm_g):
    b, s, d = x.shape
    depth = w_in.shape[0]
    assert s % PROJ_ROWS == 0 and PROJ_ROWS % TILE == 0
    pos3 = positions.reshape(b, s, 1)
    freq, phase = _rope_rows()
    seg_np, level_np = _gla_constants()
    seg = jnp.asarray(seg_np, BF16)
    level = jnp.asarray(level_np)

    sizes = (MLA_Q_RANK, MLA_KV_RANK, MLA_ROPE, MLA_WIDTH, GLA_DK, GLA_DK, GLA_DV, GLA_GATE_RANK,
             GLA_DV, d, d)
    offs = np.concatenate([[0], np.cumsum(sizes)])

    def cols(w, idx):
        return w[:, int(offs[idx]):int(offs[idx + 1])]

    for l in range(depth):
        w = w_in[l]
        wkr = cols(w, 2)
        wstd = jnp.concatenate(
            [cols(w, 0), cols(w, 1), jnp.zeros((d, MLA_NOPE), F32), wkr, _swap_halves(wkr),
             cols(w, 9), cols(w, 10)], axis=1).astype(BF16)
        wt = jnp.concatenate(
            [cols(w, 3), cols(w, 4) * (GLA_HEAD_K ** -0.5), cols(w, 5), cols(w, 6), cols(w, 8),
             cols(w, 7)], axis=1).T.astype(BF16)
        uq = w_uq[l].reshape(MLA_Q_RANK, MLA_HEADS, MLA_NOPE + MLA_ROPE)
        uq_r = uq[..., MLA_NOPE:]
        wq = jnp.concatenate([uq[..., :MLA_NOPE], uq_r, _swap_halves(uq_r)], axis=-1)
        wq = wq.reshape(MLA_Q_RANK, MLA_HEADS * HEAD_LANES).astype(BF16)
        ukv = w_ukv[l].reshape(MLA_KV_RANK, MLA_HEADS, MLA_NOPE + MLA_V)
        wk = jnp.concatenate([ukv[..., :MLA_NOPE], jnp.zeros_like(ukv[..., :MLA_NOPE])], axis=-1)
        wk = wk.reshape(MLA_KV_RANK, MLA_HEADS * HEAD_LANES).astype(BF16)
        wvt = ukv[..., MLA_NOPE:].reshape(MLA_KV_RANK, MLA_WIDTH).T.astype(BF16)

        (q, k, vt, gmla, gq, gk, gv, ggla, glow, ma, mb) = _projections(
            x, pos3, freq, phase, norm_in_g[l][None, :], wstd, wt, q_norm_g[l][None, :], wq,
            kv_norm_g[l][None, :], wk, wvt)

        o_mla = _mla_attention(q, k, vt)
        o_gla = _gla(gq, gk, gv, glow, w_gk_up[l].T.astype(BF16), b_gk[l][:, None], seg, level,
                     gla_norm_g[l][:, None])

        x = _output(x, p[l], o_mla, gmla, o_gla, ggla, ma, mb, w_mla_br[l].astype(BF16),
                    w_gla_br[l].astype(BF16), w_out[l].astype(BF16), w_ple[l].astype(BF16),
                    ple_norm_g[l][None, :], ple_gate_norm_g[l][None, :], w_ple_gate[l].astype(BF16),
                    final_norm_g[None, :], l == depth - 1)
    return x
```

```python
import functools

import numpy as np
import jax
import jax.numpy as jnp
from jax import lax
from jax.experimental import pallas as pl
from jax.experimental.pallas import tpu as pltpu

F32 = jnp.float32
BF16 = jnp.bfloat16

MLA_HEADS = 8
MLA_Q_RANK = 384
MLA_KV_RANK = 256
MLA_NOPE = 64
MLA_ROPE = 32
MLA_V = 64
MLA_WIDTH = MLA_HEADS * MLA_V
ROPE_THETA = 10000.0
GLA_HEADS = 4
GLA_HEAD_K = 64
GLA_HEAD_V = 128
GLA_DK = GLA_HEADS * GLA_HEAD_K
GLA_DV = GLA_HEADS * GLA_HEAD_V
GLA_GATE_RANK = 16
GLA_GATE_NORMALIZER = 16.0
EPS = 1e-6
LOG2E = 1.4426950408889634

HEAD_LANES = 128
TILE = 256
PROJ_ROWS = 512
MLA_HEADS_PER_STEP = 4
MLA_VX = MLA_V + 16
VMEM_LIMIT = 56 * 1024 * 1024

GLA_BLOCK = 16
HALF = TILE // 2
GLA_LEVELS = (HALF, 64, 32, 16)
GLA_NFACT = 2 + len(GLA_LEVELS)
GLA_BAND = len(GLA_LEVELS)

NT = (((1,), (1,)), ((), ()))
TN = (((0,), (0,)), ((), ()))


def _dot(a, b):
    return jnp.dot(a, b, preferred_element_type=F32)


def _dot_nt(a, b):
    return lax.dot_general(a, b, NT, preferred_element_type=F32)


def _dot_tn(a, b):
    return lax.dot_general(a, b, TN, preferred_element_type=F32)


def _rms_rows(x, g):
    return x * lax.rsqrt(jnp.mean(x * x, axis=-1, keepdims=True) + EPS) * g


def _proj_kernel(x_ref, pos_ref, freq_ref, phase_ref, gin_ref, wstd_ref, wt_ref, gq_ref, wq_ref,
                 gkv_ref, wk_ref, wvt_ref,
                 q_out, k_out, vt_out, gmla_out, gq_out, gk_out, gv_out, ggla_out, glow_out,
                 ma_out, mb_out):
    h = _rms_rows(x_ref[0], gin_ref[...]).astype(BF16)

    ht = _dot_nt(wt_ref[...], h)
    for out, lo, hi in ((gmla_out, 0, 512), (gq_out, 512, 768), (gk_out, 768, 1024),
                        (gv_out, 1024, 1536), (ggla_out, 1536, 2048), (glow_out, 2048, 2064)):
        out[0] = ht[lo:hi, :].astype(BF16)
    ma_out[0] = _dot(h, wstd_ref[:, 768:1792]).astype(BF16)
    mb_out[0] = _dot(h, wstd_ref[:, 1792:2816]).astype(BF16)

    pos = pos_ref[0].astype(F32)
    table = jnp.cos(pos * freq_ref[...] - phase_ref[...])

    cq = _rms_rows(_dot(h, wstd_ref[:, 0:384]), gq_ref[...]).astype(BF16)
    q = _dot(cq, wq_ref[...])
    qscale = table * ((MLA_NOPE + MLA_ROPE) ** -0.5 * LOG2E)
    for hh in range(MLA_HEADS):
        q_out[0, hh] = (q[:, hh * HEAD_LANES:(hh + 1) * HEAD_LANES] * qscale).astype(BF16)

    ckv = _rms_rows(_dot(h, wstd_ref[:, 384:640]), gkv_ref[...]).astype(BF16)
    u = _dot(h, wstd_ref[:, 640:768]) * table
    lane = lax.broadcasted_iota(jnp.int32, u.shape, 1)
    krot = jnp.where(lane >= MLA_NOPE, u + pltpu.roll(u, 32, 1) + pltpu.roll(u, 96, 1), 0.0)
    kk = _dot(ckv, wk_ref[...])
    for hh in range(MLA_HEADS):
        k_out[0, hh] = (kk[:, hh * HEAD_LANES:(hh + 1) * HEAD_LANES] + krot).astype(BF16)

    vt = _dot_nt(wvt_ref[...], ckv)
    for hh in range(MLA_HEADS):
        vt_out[0, hh] = vt[hh * MLA_V:(hh + 1) * MLA_V, :].astype(BF16)


def _projections(x, pos3, freq, phase, gin, wstd, wt, gq, wq, gkv, wk, wvt):
    b, s, d = x.shape
    rows = PROJ_ROWS

    def full(a):
        return pl.BlockSpec(a.shape, lambda i, j: (0,) * a.ndim)

    def tok(width):
        return pl.BlockSpec((1, rows, width), lambda i, j: (i, j, 0))

    def tr(n):
        return pl.BlockSpec((1, n, rows), lambda i, j: (i, 0, j))

    def tr_shape(n):
        return jax.ShapeDtypeStruct((b, n, s), BF16)

    heads_tok = pl.BlockSpec((1, MLA_HEADS, rows, HEAD_LANES), lambda i, j: (i, 0, j, 0))
    heads_tr = pl.BlockSpec((1, MLA_HEADS, MLA_V, rows), lambda i, j: (i, 0, 0, j))
    out_shape = (
        jax.ShapeDtypeStruct((b, MLA_HEADS, s, HEAD_LANES), BF16),
        jax.ShapeDtypeStruct((b, MLA_HEADS, s, HEAD_LANES), BF16),
        jax.ShapeDtypeStruct((b, MLA_HEADS, MLA_V, s), BF16),
        tr_shape(MLA_WIDTH),
        tr_shape(GLA_DK),
        tr_shape(GLA_DK),
        tr_shape(GLA_DV),
        tr_shape(GLA_DV),
        tr_shape(GLA_GATE_RANK),
        jax.ShapeDtypeStruct((b, s, d), BF16),
        jax.ShapeDtypeStruct((b, s, d), BF16),
    )
    out_specs = (heads_tok, heads_tok, heads_tr, tr(MLA_WIDTH),
                 tr(GLA_DK), tr(GLA_DK), tr(GLA_DV), tr(GLA_DV), tr(GLA_GATE_RANK), tok(d), tok(d))
    in_specs = [tok(d), tok(1), full(freq), full(phase), full(gin), full(wstd), full(wt), full(gq),
                full(wq), full(gkv), full(wk), full(wvt)]
    return pl.pallas_call(
        _proj_kernel,
        grid=(b, s // rows),
        in_specs=in_specs,
        out_specs=out_specs,
        out_shape=out_shape,
        compiler_params=pltpu.CompilerParams(
            dimension_semantics=("parallel", "parallel"), vmem_limit_bytes=VMEM_LIMIT),
        name="proj",
    )(x, pos3, freq, phase, gin, wstd, wt, gq, wq, gkv, wk, wvt)


def _mla_kernel(q_ref, k_ref, vt_ref, o_ref, s_ref, p_ref, vx_ref):
    heads, seq = q_ref.shape[1], q_ref.shape[2]
    row = lax.broadcasted_iota(jnp.int32, (TILE, TILE), 0)
    col = lax.broadcasted_iota(jnp.int32, (TILE, TILE), 1)
    causal = row <= col
    tasks = [(hh, qi) for hh in range(heads) for qi in range(seq // TILE)]

    for hh in range(heads):
        vx_ref[hh, 0:MLA_V, :] = vt_ref[0, hh]
        vx_ref[hh, MLA_V:, :] = jnp.ones((MLA_VX - MLA_V, seq), BF16)

    def scores(t):
        hh, qi = tasks[t]
        lo, hi = qi * TILE, (qi + 1) * TILE
        qt = q_ref[0, hh, lo:hi, :]
        if qi:
            s_ref[t % 2, 0:lo, :] = _dot_nt(k_ref[0, hh, 0:lo, :], qt)
        s_ref[t % 2, lo:hi, :] = jnp.where(causal, _dot_nt(k_ref[0, hh, lo:hi, :], qt), -jnp.inf)

    def softmax(t):
        _, qi = tasks[t]
        m = jnp.full((1, TILE), -jnp.inf, F32)
        for kj in range(qi + 1):
            m = jnp.maximum(m, jnp.max(s_ref[t % 2, kj * TILE:(kj + 1) * TILE, :], axis=0, keepdims=True))
        for kj in range(qi + 1):
            blk = slice(kj * TILE, (kj + 1) * TILE)
            p_ref[t % 2, blk, :] = jnp.exp2(s_ref[t % 2, blk, :] - m).astype(BF16)

    def weighted_values(t):
        hh, qi = tasks[t]
        lo, hi = qi * TILE, (qi + 1) * TILE
        acc = _dot(vx_ref[hh, :, 0:hi], p_ref[t % 2, 0:hi, :])
        o_ref[0, hh, :, lo:hi] = (acc[0:MLA_V] / acc[MLA_V:MLA_V + 1]).astype(BF16)

    scores(0)
    for t in range(len(tasks)):
        if t + 1 < len(tasks):
            scores(t + 1)
        if t:
            weighted_values(t - 1)
        softmax(t)
    weighted_values(len(tasks) - 1)


def _mla_attention(q, k, vt):
    b, heads, s, _ = q.shape
    hp = MLA_HEADS_PER_STEP
    tok = pl.BlockSpec((1, hp, s, HEAD_LANES), lambda i, h: (i, h, 0, 0))
    tr = pl.BlockSpec((1, hp, MLA_V, s), lambda i, h: (i, h, 0, 0))
    return pl.pallas_call(
        _mla_kernel,
        grid=(b, heads // hp),
        in_specs=[tok, tok, tr],
        out_specs=tr,
        out_shape=jax.ShapeDtypeStruct((b, heads, MLA_V, s), BF16),
        scratch_shapes=[pltpu.VMEM((2, s, TILE), F32), pltpu.VMEM((2, s, TILE), BF16),
                        pltpu.VMEM((hp, MLA_VX, s), BF16)],
        compiler_params=pltpu.CompilerParams(
            dimension_semantics=("parallel", "parallel"), vmem_limit_bytes=VMEM_LIMIT),
        name="mla",
    )(q, k, vt)


def _gla_constants():
    u = np.arange(TILE)[:, None]
    t = np.arange(TILE)[None, :]
    mats = [(u <= t), (u > t)]
    for m in GLA_LEVELS:
        pos = t % (2 * m)
        mid = t - pos + m - 1
        upper = pos >= m
        mats.append(np.where(upper, (u > mid) & (u <= t), (u > t) & (u <= mid)))
    seg = np.concatenate([mm.astype(np.float32) for mm in mats], axis=1)
    seg = np.concatenate([seg, seg, seg], axis=0)
    j = np.arange(TILE)[:, None]
    i = np.arange(TILE)[None, :]
    level = np.full((TILE, TILE), -1, np.int32)
    level[(j // GLA_BLOCK) == (i // GLA_BLOCK)] = GLA_BAND
    for idx, m in enumerate(GLA_LEVELS):
        same = (j // (2 * m)) == (i // (2 * m))
        split = ((j % (2 * m)) < m) & ((i % (2 * m)) >= m)
        level[same & split & (j < i)] = idx
    return seg, level[:HALF, :HALF]


def _gla_band(qf, kf, decay):
    rel = (lax.broadcasted_iota(jnp.int32, (GLA_BLOCK, TILE), 1) % GLA_BLOCK
           - lax.broadcasted_iota(jnp.int32, (GLA_BLOCK, TILE), 0))
    w = [kf]
    span = decay
    step = 1
    while step < GLA_BLOCK:
        w += [span * pltpu.roll(w[d], step, 1) for d in range(step)]
        span = span * pltpu.roll(span, step, 1)
        step *= 2
    bands = [jnp.zeros((GLA_BLOCK, TILE), F32) for _ in range(GLA_HEADS)]
    for d in range(GLA_BLOCK):
        prod = qf * w[d]
        for hh in range(GLA_HEADS):
            diag = jnp.sum(prod[hh * GLA_HEAD_K:(hh + 1) * GLA_HEAD_K], axis=0, keepdims=True)
            bands[hh] = jnp.where(rel == d, diag, bands[hh])
    return bands


def _gla_kernel(qt_ref, kt_ref, vt_ref, glow_ref, wg_ref, bg_ref, seg_ref, level_ref, gn_ref,
                o_ref, state_ref, fact_ref):
    @pl.when(pl.program_id(1) == 0)
    def _():
        state_ref[...] = jnp.zeros_like(state_ref)

    z = _dot(wg_ref[...], glow_ref[0]) + bg_ref[...]
    g = -(jnp.maximum(-z, 0.0) + jnp.log1p(jnp.exp(-jnp.abs(z)))) * (1.0 / GLA_GATE_NORMALIZER)
    g_hi = g.astype(BF16)
    r = g - g_hi.astype(F32)
    g_mid = r.astype(BF16)
    g_lo = (r - g_mid.astype(F32)).astype(BF16)
    g3 = jnp.concatenate([g_hi, g_mid, g_lo], axis=1)
    for n in range(GLA_NFACT):
        fact_ref[n] = jnp.exp(_dot(g3, seg_ref[:, n * TILE:(n + 1) * TILE]))
    bands = _gla_band(qt_ref[0].astype(F32), kt_ref[0].astype(F32), jnp.exp(g))

    level = level_ref[...]
    lane = lax.broadcasted_iota(jnp.int32, (GLA_HEAD_K, TILE), 1)
    for hh in range(GLA_HEADS):
        rk = slice(hh * GLA_HEAD_K, (hh + 1) * GLA_HEAD_K)
        rv = slice(hh * GLA_HEAD_V, (hh + 1) * GLA_HEAD_V)
        qf = qt_ref[0, rk, :].astype(F32)
        kf = kt_ref[0, rk, :].astype(F32)
        vt = vt_ref[0, rv, :]
        f_in = fact_ref[0, rk, :]
        q_in = (qf * f_in).astype(BF16)
        k_out = (kf * fact_ref[1, rk, :]).astype(BF16)
        state = state_ref[hh]
        o = _dot_tn(state.astype(BF16), q_in)

        far = []
        for idx, m in enumerate(GLA_LEVELS):
            f = fact_ref[2 + idx, rk, :]
            upper = (lane // m) % 2 == 1
            q_up = jnp.where(upper, qf * f, 0.0).astype(BF16)
            k_lo = jnp.where(upper, 0.0, kf * f).astype(BF16)
            far.append(_dot_tn(k_lo, q_up))
        band = jnp.tile(bands[hh], (HALF // GLA_BLOCK, 1))

        def diagonal(c):
            sl = slice(c * HALF, (c + 1) * HALF)
            blk = far[1][sl, sl]
            for idx in range(2, len(GLA_LEVELS)):
                blk = jnp.where(level == idx, far[idx][sl, sl], blk)
            return jnp.where(level == GLA_BAND, band[:, sl], blk)

        at = jnp.concatenate(
            [jnp.concatenate([diagonal(0), far[0][0:HALF, HALF:]], axis=1),
             jnp.concatenate([jnp.zeros((HALF, HALF), F32), diagonal(1)], axis=1)], axis=0)
        o = o + _dot(vt, at.astype(BF16))

        ms = jnp.mean(o * o, axis=0, keepdims=True)
        o_ref[0, rv, :] = (o * lax.rsqrt(ms + EPS) * gn_ref[...]).astype(BF16)

        state_ref[hh] = state * f_in[:, TILE - 1:TILE] + _dot_nt(k_out, vt)


def _gla(qt, kt, vt, glow, wg, bg, seg, level, gn):
    b, _, s = qt.shape

    def full(a):
        return pl.BlockSpec(a.shape, lambda i, j: (0,) * a.ndim)

    def tr(n):
        return pl.BlockSpec((1, n, TILE), lambda i, j: (i, 0, j))

    return pl.pallas_call(
        _gla_kernel,
        grid=(b, s // TILE),
        in_specs=[tr(GLA_DK), tr(GLA_DK), tr(GLA_DV), tr(GLA_GATE_RANK), full(wg), full(bg),
                  full(seg), full(level), full(gn)],
        out_specs=tr(GLA_DV),
        out_shape=jax.ShapeDtypeStruct((b, GLA_DV, s), BF16),
        scratch_shapes=[pltpu.VMEM((GLA_HEADS, GLA_HEAD_K, GLA_HEAD_V), F32),
                        pltpu.VMEM((GLA_NFACT, GLA_DK, TILE), F32)],
        compiler_params=pltpu.CompilerParams(
            dimension_semantics=("parallel", "arbitrary"), vmem_limit_bytes=VMEM_LIMIT),
        name="gla",
    )(qt, kt, vt, glow, wg, bg, seg, level, gn)


def _out_kernel(x_ref, p_ref, om_ref, gm_ref, og_ref, gg_ref, ma_ref, mb_ref, wa_ref, wb_ref,
                wout_ref, wple_ref, gple_ref, ggate_ref, wgate_ref, gfin_ref, o_ref, *, final):
    def branch(o_t, g_t_ref, w_ref):
        gh = g_t_ref[0].astype(F32)
        zt = (o_t.astype(F32) * (gh * (jnp.tanh(gh) + 1.0))).astype(BF16)
        return _dot_tn(zt, w_ref[...])

    ya = branch(om_ref[0].reshape(MLA_WIDTH, PROJ_ROWS), gm_ref, wa_ref)
    yb = branch(og_ref[0], gg_ref, wb_ref)
    merged2 = ((jnp.tanh(ma_ref[0].astype(F32)) + 1.0) * ya
               + (jnp.tanh(mb_ref[0].astype(F32)) + 1.0) * yb)
    x1 = x_ref[0] + _dot(merged2.astype(BF16), wout_ref[...])
    e_half = _rms_rows(_dot(p_ref[0].astype(BF16), wple_ref[...]), gple_ref[...])
    gate2 = jnp.tanh(_dot(_rms_rows(x1, ggate_ref[...]).astype(BF16), wgate_ref[...])) + 1.0
    x2 = x1 + gate2 * e_half
    o_ref[0] = _rms_rows(x2, gfin_ref[...]) if final else x2


def _output(x, p, om, gm, og, gg, ma, mb, wa, wb, wout, wple, gple, ggate, wgate, gfin, final):
    b, s, d = x.shape
    rows = PROJ_ROWS

    def full(a):
        return pl.BlockSpec(a.shape, lambda i, j: (0,) * a.ndim)

    def tok(width):
        return pl.BlockSpec((1, rows, width), lambda i, j: (i, j, 0))

    def tr(n):
        return pl.BlockSpec((1, n, rows), lambda i, j: (i, 0, j))

    heads_tr = pl.BlockSpec((1, MLA_HEADS, MLA_V, rows), lambda i, j: (i, 0, 0, j))
    in_specs = [tok(d), tok(p.shape[-1]), heads_tr, tr(MLA_WIDTH), tr(GLA_DV), tr(GLA_DV),
                tok(d), tok(d), full(wa), full(wb), full(wout), full(wple), full(gple), full(ggate),
                full(wgate), full(gfin)]
    return pl.pallas_call(
        functools.partial(_out_kernel, final=final),
        grid=(b, s // rows),
        in_specs=in_specs,
        out_specs=tok(d),
        out_shape=jax.ShapeDtypeStruct((b, s, d), F32),
        compiler_params=pltpu.CompilerParams(
            dimension_semantics=("parallel", "parallel"), vmem_limit_bytes=VMEM_LIMIT),
        name="out",
    )(x, p, om, gm, og, gg, ma, mb, wa, wb, wout, wple, gple, ggate, wgate, gfin)


def _swap_halves(w):
    half = w.shape[-1] // 2
    return jnp.concatenate([-w[..., half:], w[..., :half]], axis=-1)


def _rope_rows():
    inv_freq = 1.0 / (ROPE_THETA ** (jnp.arange(0, MLA_ROPE, 2, dtype=F32) / MLA_ROPE))
    zeros = jnp.zeros((MLA_NOPE,), F32)
    freq = jnp.concatenate([zeros, inv_freq, inv_freq, inv_freq, inv_freq])
    phase = jnp.concatenate([zeros, jnp.zeros((MLA_ROPE,), F32), jnp.full((MLA_ROPE,), np.pi / 2, F32)])
    return freq[None, :], phase[None, :]


def kernel(x, p, positions, norm_in_g, w_in, q_norm_g, w_uq, kv_norm_g, w_ukv, w_gk_up, b_gk,
           gla_norm_g, w_mla_br, w_gla_br, w_out, w_ple, ple_norm_g, ple_gate_norm_g, w_ple_gate,
           final_norm_g):
    b, s, d = x.shape
    depth = w_in.shape[0]
    assert s % PROJ_ROWS == 0 and PROJ_ROWS % TILE == 0
    pos3 = positions.reshape(b, s, 1)
    freq, phase = _rope_rows()
    seg_np, level_np = _gla_constants()
    seg = jnp.asarray(seg_np, BF16)
    level = jnp.asarray(level_np)

    sizes = (MLA_Q_RANK, MLA_KV_RANK, MLA_ROPE, MLA_WIDTH, GLA_DK, GLA_DK, GLA_DV, GLA_GATE_RANK,
             GLA_DV, d, d)
    offs = np.concatenate([[0], np.cumsum(sizes)])

    def cols(w, idx):
        return w[:, int(offs[idx]):int(offs[idx + 1])]

    for l in range(depth):
        w = w_in[l]
        wkr = cols(w, 2)
        wstd = jnp.concatenate(
            [cols(w, 0), cols(w, 1), jnp.zeros((d, MLA_NOPE), F32), wkr, _swap_halves(wkr),
             cols(w, 9) * 0.5, cols(w, 10) * 0.5], axis=1).astype(BF16)
        wt = jnp.concatenate(
            [cols(w, 3) * 0.5, cols(w, 4) * (GLA_HEAD_K ** -0.5), cols(w, 5), cols(w, 6),
             cols(w, 8) * 0.5, cols(w, 7)], axis=1).T.astype(BF16)
        uq = w_uq[l].reshape(MLA_Q_RANK, MLA_HEADS, MLA_NOPE + MLA_ROPE)
        uq_r = uq[..., MLA_NOPE:]
        wq = jnp.concatenate([uq[..., :MLA_NOPE], uq_r, _swap_halves(uq_r)], axis=-1)
        wq = wq.reshape(MLA_Q_RANK, MLA_HEADS * HEAD_LANES).astype(BF16)
        ukv = w_ukv[l].reshape(MLA_KV_RANK, MLA_HEADS, MLA_NOPE + MLA_V)
        wk = jnp.concatenate([ukv[..., :MLA_NOPE], jnp.zeros_like(ukv[..., :MLA_NOPE])], axis=-1)
        wk = wk.reshape(MLA_KV_RANK, MLA_HEADS * HEAD_LANES).astype(BF16)
        wvt = ukv[..., MLA_NOPE:].reshape(MLA_KV_RANK, MLA_WIDTH).T.astype(BF16)

        (q, k, vt, gmla, gq, gk, gv, ggla, glow, ma, mb) = _projections(
            x, pos3, freq, phase, norm_in_g[l][None, :], wstd, wt, q_norm_g[l][None, :], wq,
            kv_norm_g[l][None, :], wk, wvt)

        o_mla = _mla_attention(q, k, vt)
        o_gla = _gla(gq, gk, gv, glow, w_gk_up[l].T.astype(BF16), b_gk[l][:, None], seg, level,
                     gla_norm_g[l][:, None])

        x = _output(x, p[l], o_mla, gmla, o_gla, ggla, ma, mb, w_mla_br[l].astype(BF16),
                    w_gla_br[l].astype(BF16), (w_out[l] * 0.5).astype(BF16), w_ple[l].astype(BF16),
                    ple_norm_g[l][None, :] * 0.5, ple_gate_norm_g[l][None, :],
                    (w_ple_gate[l] * 0.5).astype(BF16),
                    final_norm_g[None, :], l == depth - 1)
    return x
```

```python
import functools

import numpy as np
import jax
import jax.numpy as jnp
from jax import lax
from jax.experimental import pallas as pl
from jax.experimental.pallas import tpu as pltpu

F32 = jnp.float32
BF16 = jnp.bfloat16

MLA_HEADS = 8
MLA_Q_RANK = 384
MLA_KV_RANK = 256
MLA_NOPE = 64
MLA_ROPE = 32
MLA_V = 64
MLA_WIDTH = MLA_HEADS * MLA_V
ROPE_THETA = 10000.0
GLA_HEADS = 4
GLA_HEAD_K = 64
GLA_HEAD_V = 128
GLA_DK = GLA_HEADS * GLA_HEAD_K
GLA_DV = GLA_HEADS * GLA_HEAD_V
GLA_GATE_RANK = 16
GLA_GATE_NORMALIZER = 16.0
EPS = 1e-6
LOG2E = 1.4426950408889634

HEAD_LANES = 128
TILE = 256
PROJ_ROWS = 512
MLA_HEADS_PER_STEP = 4
MLA_VX = MLA_V + 16
ROPE_PACK = 128 // MLA_ROPE
VMEM_LIMIT = 56 * 1024 * 1024

GLA_BLOCK = 16
HALF = TILE // 2
GLA_LEVELS = (HALF, 64, 32, 16)
GLA_NFACT = 2 + len(GLA_LEVELS)
GLA_BAND = len(GLA_LEVELS)

NT = (((1,), (1,)), ((), ()))
TN = (((0,), (0,)), ((), ()))


def _dot(a, b):
    return jnp.dot(a, b, preferred_element_type=F32)


def _dot_nt(a, b):
    return lax.dot_general(a, b, NT, preferred_element_type=F32)


def _dot_tn(a, b):
    return lax.dot_general(a, b, TN, preferred_element_type=F32)


def _rms_rows(x, g):
    return x * lax.rsqrt(jnp.mean(x * x, axis=-1, keepdims=True) + EPS) * g


def _proj_kernel(x_ref, cs_ref, gin_ref, wstd_ref, wt_ref, gq_ref, wq_ref,
                 gkv_ref, wk_ref, wvt_ref,
                 q_out, k_out, vt_out, gmla_out, gq_out, gk_out, gv_out, ggla_out, glow_out,
                 ma_out, mb_out):
    h = _rms_rows(x_ref[0], gin_ref[...]).astype(BF16)

    ht = _dot_nt(wt_ref[...], h)
    for out, lo, hi in ((gmla_out, 0, 512), (gq_out, 512, 768), (gk_out, 768, 1024),
                        (gv_out, 1024, 1536), (ggla_out, 1536, 2048), (glow_out, 2048, 2064)):
        out[0] = ht[lo:hi, :].astype(BF16)
    ma_out[0] = _dot(h, wstd_ref[:, 768:1792]).astype(BF16)
    mb_out[0] = _dot(h, wstd_ref[:, 1792:2816]).astype(BF16)

    cos, sin = cs_ref[0, :, 0:MLA_ROPE // 2], cs_ref[0, :, MLA_ROPE // 2:MLA_ROPE]
    table = jnp.concatenate([jnp.ones((PROJ_ROWS, MLA_NOPE), F32), cos, cos, sin, sin], axis=1)

    cq = _rms_rows(_dot(h, wstd_ref[:, 0:384]), gq_ref[...]).astype(BF16)
    q = _dot(cq, wq_ref[...])
    qscale = table * ((MLA_NOPE + MLA_ROPE) ** -0.5 * LOG2E)
    for hh in range(MLA_HEADS):
        q_out[0, hh] = (q[:, hh * HEAD_LANES:(hh + 1) * HEAD_LANES] * qscale).astype(BF16)

    ckv = _rms_rows(_dot(h, wstd_ref[:, 384:640]), gkv_ref[...]).astype(BF16)
    u = _dot(h, wstd_ref[:, 640:768]) * table
    lane = lax.broadcasted_iota(jnp.int32, u.shape, 1)
    krot = jnp.where(lane >= MLA_NOPE, u + pltpu.roll(u, 32, 1) + pltpu.roll(u, 96, 1), 0.0)
    kk = _dot(ckv, wk_ref[...])
    for hh in range(MLA_HEADS):
        k_out[0, hh] = (kk[:, hh * HEAD_LANES:(hh + 1) * HEAD_LANES] + krot).astype(BF16)

    vt = _dot_nt(wvt_ref[...], ckv)
    for hh in range(MLA_HEADS):
        vt_out[0, hh] = vt[hh * MLA_V:(hh + 1) * MLA_V, :].astype(BF16)


def _projections(x, cs, gin, wstd, wt, gq, wq, gkv, wk, wvt):
    b, s, d = x.shape
    rows = PROJ_ROWS

    def full(a):
        return pl.BlockSpec(a.shape, lambda i, j: (0,) * a.ndim)

    def tok(width):
        return pl.BlockSpec((1, rows, width), lambda i, j: (i, j, 0))

    def tr(n):
        return pl.BlockSpec((1, n, rows), lambda i, j: (i, 0, j))

    def tr_shape(n):
        return jax.ShapeDtypeStruct((b, n, s), BF16)

    heads_tok = pl.BlockSpec((1, MLA_HEADS, rows, HEAD_LANES), lambda i, j: (i, 0, j, 0))
    heads_tr = pl.BlockSpec((1, MLA_HEADS, MLA_V, rows), lambda i, j: (i, 0, 0, j))
    out_shape = (
        jax.ShapeDtypeStruct((b, MLA_HEADS, s, HEAD_LANES), BF16),
        jax.ShapeDtypeStruct((b, MLA_HEADS, s, HEAD_LANES), BF16),
        jax.ShapeDtypeStruct((b, MLA_HEADS, MLA_V, s), BF16),
        tr_shape(MLA_WIDTH),
        tr_shape(GLA_DK),
        tr_shape(GLA_DK),
        tr_shape(GLA_DV),
        tr_shape(GLA_DV),
        tr_shape(GLA_GATE_RANK),
        jax.ShapeDtypeStruct((b, s, d), BF16),
        jax.ShapeDtypeStruct((b, s, d), BF16),
    )
    out_specs = (heads_tok, heads_tok, heads_tr, tr(MLA_WIDTH),
                 tr(GLA_DK), tr(GLA_DK), tr(GLA_DV), tr(GLA_DV), tr(GLA_GATE_RANK), tok(d), tok(d))
    in_specs = [tok(d), tok(MLA_ROPE), full(gin), full(wstd), full(wt), full(gq),
                full(wq), full(gkv), full(wk), full(wvt)]
    return pl.pallas_call(
        _proj_kernel,
        grid=(b, s // rows),
        in_specs=in_specs,
        out_specs=out_specs,
        out_shape=out_shape,
        compiler_params=pltpu.CompilerParams(
            dimension_semantics=("parallel", "parallel"), vmem_limit_bytes=VMEM_LIMIT),
        name="proj",
    )(x, cs, gin, wstd, wt, gq, wq, gkv, wk, wvt)


def _mla_kernel(q_ref, k_ref, vt_ref, o_ref, s_ref, p_ref, vx_ref):
    heads, seq = q_ref.shape[1], q_ref.shape[2]
    row = lax.broadcasted_iota(jnp.int32, (TILE, TILE), 0)
    col = lax.broadcasted_iota(jnp.int32, (TILE, TILE), 1)
    causal = row <= col
    tasks = [(hh, qi) for hh in range(heads) for qi in range(seq // TILE)]

    for hh in range(heads):
        vx_ref[hh, 0:MLA_V, :] = vt_ref[0, hh]
        vx_ref[hh, MLA_V:, :] = jnp.ones((MLA_VX - MLA_V, seq), BF16)

    def scores(t):
        hh, qi = tasks[t]
        lo, hi = qi * TILE, (qi + 1) * TILE
        qt = q_ref[0, hh, lo:hi, :]
        if qi:
            s_ref[t % 2, 0:lo, :] = _dot_nt(k_ref[0, hh, 0:lo, :], qt)
        s_ref[t % 2, lo:hi, :] = jnp.where(causal, _dot_nt(k_ref[0, hh, lo:hi, :], qt), -jnp.inf)

    def softmax(t):
        _, qi = tasks[t]
        m = jnp.full((1, TILE), -jnp.inf, F32)
        for kj in range(qi + 1):
            m = jnp.maximum(m, jnp.max(s_ref[t % 2, kj * TILE:(kj + 1) * TILE, :], axis=0, keepdims=True))
        for kj in range(qi + 1):
            blk = slice(kj * TILE, (kj + 1) * TILE)
            p_ref[t % 2, blk, :] = jnp.exp2(s_ref[t % 2, blk, :] - m).astype(BF16)

    def weighted_values(t):
        hh, qi = tasks[t]
        lo, hi = qi * TILE, (qi + 1) * TILE
        acc = _dot(vx_ref[hh, :, 0:hi], p_ref[t % 2, 0:hi, :])
        o_ref[0, hh, :, lo:hi] = (acc[0:MLA_V] / acc[MLA_V:MLA_V + 1]).astype(BF16)

    scores(0)
    for t in range(len(tasks)):
        if t + 1 < len(tasks):
            scores(t + 1)
        if t:
            weighted_values(t - 1)
        softmax(t)
    weighted_values(len(tasks) - 1)


def _mla_attention(q, k, vt):
    b, heads, s, _ = q.shape
    hp = MLA_HEADS_PER_STEP
    tok = pl.BlockSpec((1, hp, s, HEAD_LANES), lambda i, h: (i, h, 0, 0))
    tr = pl.BlockSpec((1, hp, MLA_V, s), lambda i, h: (i, h, 0, 0))
    return pl.pallas_call(
        _mla_kernel,
        grid=(b, heads // hp),
        in_specs=[tok, tok, tr],
        out_specs=tr,
        out_shape=jax.ShapeDtypeStruct((b, heads, MLA_V, s), BF16),
        scratch_shapes=[pltpu.VMEM((2, s, TILE), F32), pltpu.VMEM((2, s, TILE), BF16),
                        pltpu.VMEM((hp, MLA_VX, s), BF16)],
        compiler_params=pltpu.CompilerParams(
            dimension_semantics=("parallel", "parallel"), vmem_limit_bytes=VMEM_LIMIT),
        name="mla",
    )(q, k, vt)


def _gla_constants():
    u = np.arange(TILE)[:, None]
    t = np.arange(TILE)[None, :]
    mats = [(u <= t), (u > t)]
    for m in GLA_LEVELS:
        pos = t % (2 * m)
        mid = t - pos + m - 1
        upper = pos >= m
        mats.append(np.where(upper, (u > mid) & (u <= t), (u > t) & (u <= mid)))
    seg = np.concatenate([mm.astype(np.float32) for mm in mats], axis=1)
    seg = np.concatenate([seg, seg, seg], axis=0)
    j = np.arange(TILE)[:, None]
    i = np.arange(TILE)[None, :]
    level = np.full((TILE, TILE), -1, np.int32)
    level[(j // GLA_BLOCK) == (i // GLA_BLOCK)] = GLA_BAND
    for idx, m in enumerate(GLA_LEVELS):
        same = (j // (2 * m)) == (i // (2 * m))
        split = ((j % (2 * m)) < m) & ((i % (2 * m)) >= m)
        level[same & split & (j < i)] = idx
    return seg, level[:HALF, :HALF]


def _gla_band(qf, kf, decay):
    rel = (lax.broadcasted_iota(jnp.int32, (GLA_BLOCK, TILE), 1) % GLA_BLOCK
           - lax.broadcasted_iota(jnp.int32, (GLA_BLOCK, TILE), 0))
    w = [kf]
    span = decay
    step = 1
    while step < GLA_BLOCK:
        w += [span * pltpu.roll(w[d], step, 1) for d in range(step)]
        span = span * pltpu.roll(span, step, 1)
        step *= 2
    bands = [jnp.zeros((GLA_BLOCK, TILE), F32) for _ in range(GLA_HEADS)]
    for d in range(GLA_BLOCK):
        prod = qf * w[d]
        for hh in range(GLA_HEADS):
            diag = jnp.sum(prod[hh * GLA_HEAD_K:(hh + 1) * GLA_HEAD_K], axis=0, keepdims=True)
            bands[hh] = jnp.where(rel == d, diag, bands[hh])
    return bands


def _gla_kernel(qt_ref, kt_ref, vt_ref, glow_ref, wg_ref, bg_ref, seg_ref, level_ref, gn_ref,
                o_ref, state_ref, fact_ref):
    @pl.when(pl.program_id(1) == 0)
    def _():
        state_ref[...] = jnp.zeros_like(state_ref)

    z = _dot(wg_ref[...], glow_ref[0]) + bg_ref[...]
    g = -(jnp.maximum(-z, 0.0) + jnp.log1p(jnp.exp(-jnp.abs(z)))) * (1.0 / GLA_GATE_NORMALIZER)
    g_hi = g.astype(BF16)
    r = g - g_hi.astype(F32)
    g_mid = r.astype(BF16)
    g_lo = (r - g_mid.astype(F32)).astype(BF16)
    g3 = jnp.concatenate([g_hi, g_mid, g_lo], axis=1)
    for n in range(GLA_NFACT):
        fact_ref[n] = jnp.exp(_dot(g3, seg_ref[:, n * TILE:(n + 1) * TILE]))
    bands = _gla_band(qt_ref[0].astype(F32), kt_ref[0].astype(F32), jnp.exp(g))

    level = level_ref[...]
    lane = lax.broadcasted_iota(jnp.int32, (GLA_HEAD_K, TILE), 1)
    for hh in range(GLA_HEADS):
        rk = slice(hh * GLA_HEAD_K, (hh + 1) * GLA_HEAD_K)
        rv = slice(hh * GLA_HEAD_V, (hh + 1) * GLA_HEAD_V)
        qf = qt_ref[0, rk, :].astype(F32)
        kf = kt_ref[0, rk, :].astype(F32)
        vt = vt_ref[0, rv, :]
        f_in = fact_ref[0, rk, :]
        q_in = (qf * f_in).astype(BF16)
        k_out = (kf * fact_ref[1, rk, :]).astype(BF16)
        state = state_ref[hh]
        o = _dot_tn(state.astype(BF16), q_in)

        far = []
        for idx, m in enumerate(GLA_LEVELS):
            f = fact_ref[2 + idx, rk, :]
            upper = (lane // m) % 2 == 1
            q_up = jnp.where(upper, qf * f, 0.0).astype(BF16)
            k_lo = jnp.where(upper, 0.0, kf * f).astype(BF16)
            far.append(_dot_tn(k_lo, q_up))
        band = jnp.tile(bands[hh], (HALF // GLA_BLOCK, 1))

        def diagonal(c):
            sl = slice(c * HALF, (c + 1) * HALF)
            blk = far[1][sl, sl]
            for idx in range(2, len(GLA_LEVELS)):
                blk = jnp.where(level == idx, far[idx][sl, sl], blk)
            return jnp.where(level == GLA_BAND, band[:, sl], blk)

        at = jnp.concatenate(
            [jnp.concatenate([diagonal(0), far[0][0:HALF, HALF:]], axis=1),
             jnp.concatenate([jnp.zeros((HALF, HALF), F32), diagonal(1)], axis=1)], axis=0)
        o = o + _dot(vt, at.astype(BF16))

        ms = jnp.mean(o * o, axis=0, keepdims=True)
        o_ref[0, rv, :] = (o * lax.rsqrt(ms + EPS) * gn_ref[...]).astype(BF16)

        state_ref[hh] = state * f_in[:, TILE - 1:TILE] + _dot_nt(k_out, vt)


def _gla(qt, kt, vt, glow, wg, bg, seg, level, gn):
    b, _, s = qt.shape

    def full(a):
        return pl.BlockSpec(a.shape, lambda i, j: (0,) * a.ndim)

    def tr(n):
        return pl.BlockSpec((1, n, TILE), lambda i, j: (i, 0, j))

    return pl.pallas_call(
        _gla_kernel,
        grid=(b, s // TILE),
        in_specs=[tr(GLA_DK), tr(GLA_DK), tr(GLA_DV), tr(GLA_GATE_RANK), full(wg), full(bg),
                  full(seg), full(level), full(gn)],
        out_specs=tr(GLA_DV),
        out_shape=jax.ShapeDtypeStruct((b, GLA_DV, s), BF16),
        scratch_shapes=[pltpu.VMEM((GLA_HEADS, GLA_HEAD_K, GLA_HEAD_V), F32),
                        pltpu.VMEM((GLA_NFACT, GLA_DK, TILE), F32)],
        compiler_params=pltpu.CompilerParams(
            dimension_semantics=("parallel", "arbitrary"), vmem_limit_bytes=VMEM_LIMIT),
        name="gla",
    )(qt, kt, vt, glow, wg, bg, seg, level, gn)


def _out_kernel(x_ref, p_ref, om_ref, gm_ref, og_ref, gg_ref, ma_ref, mb_ref, wa_ref, wb_ref,
                wout_ref, wple_ref, gple_ref, ggate_ref, wgate_ref, gfin_ref, o_ref, *, final):
    def branch(o_t, g_t_ref, w_ref):
        gh = g_t_ref[0].astype(F32)
        zt = (o_t.astype(F32) * (gh * (jnp.tanh(gh) + 1.0))).astype(BF16)
        return _dot_tn(zt, w_ref[...])

    ya = branch(om_ref[0].reshape(MLA_WIDTH, PROJ_ROWS), gm_ref, wa_ref)
    yb = branch(og_ref[0], gg_ref, wb_ref)
    merged2 = ((jnp.tanh(ma_ref[0].astype(F32)) + 1.0) * ya
               + (jnp.tanh(mb_ref[0].astype(F32)) + 1.0) * yb)
    x1 = x_ref[0] + _dot(merged2.astype(BF16), wout_ref[...])
    e_half = _rms_rows(_dot(p_ref[0].astype(BF16), wple_ref[...]), gple_ref[...])
    gate2 = jnp.tanh(_dot(_rms_rows(x1, ggate_ref[...]).astype(BF16), wgate_ref[...])) + 1.0
    x2 = x1 + gate2 * e_half
    o_ref[0] = _rms_rows(x2, gfin_ref[...]) if final else x2


def _output(x, p, om, gm, og, gg, ma, mb, wa, wb, wout, wple, gple, ggate, wgate, gfin, final):
    b, s, d = x.shape
    rows = PROJ_ROWS

    def full(a):
        return pl.BlockSpec(a.shape, lambda i, j: (0,) * a.ndim)

    def tok(width):
        return pl.BlockSpec((1, rows, width), lambda i, j: (i, j, 0))

    def tr(n):
        return pl.BlockSpec((1, n, rows), lambda i, j: (i, 0, j))

    heads_tr = pl.BlockSpec((1, MLA_HEADS, MLA_V, rows), lambda i, j: (i, 0, 0, j))
    in_specs = [tok(d), tok(p.shape[-1]), heads_tr, tr(MLA_WIDTH), tr(GLA_DV), tr(GLA_DV),
                tok(d), tok(d), full(wa), full(wb), full(wout), full(wple), full(gple), full(ggate),
                full(wgate), full(gfin)]
    return pl.pallas_call(
        functools.partial(_out_kernel, final=final),
        grid=(b, s // rows),
        in_specs=in_specs,
        out_specs=tok(d),
        out_shape=jax.ShapeDtypeStruct((b, s, d), F32),
        compiler_params=pltpu.CompilerParams(
            dimension_semantics=("parallel", "parallel"), vmem_limit_bytes=VMEM_LIMIT),
        name="out",
    )(x, p, om, gm, og, gg, ma, mb, wa, wb, wout, wple, gple, ggate, wgate, gfin)


def _swap_halves(w):
    half = w.shape[-1] // 2
    return jnp.concatenate([-w[..., half:], w[..., :half]], axis=-1)


def _rope_kernel(pos_ref, freq_ref, phase_ref, o_ref):
    pos = pos_ref[0].astype(F32)
    slot = lax.broadcasted_iota(jnp.int32, o_ref.shape[1:], 1) // MLA_ROPE
    posx = pos[:, 0:1]
    for t in range(1, ROPE_PACK):
        posx = jnp.where(slot == t, pos[:, t:t + 1], posx)
    o_ref[0] = jnp.cos(posx * freq_ref[...] - phase_ref[...])


def _rope_table(positions):
    b, s = positions.shape
    inv_freq = 1.0 / (ROPE_THETA ** (jnp.arange(0, MLA_ROPE, 2, dtype=F32) / MLA_ROPE))
    half = MLA_ROPE // 2
    freq = jnp.tile(jnp.concatenate([inv_freq, inv_freq]), ROPE_PACK)[None, :]
    phase = jnp.tile(jnp.concatenate([jnp.zeros((half,), F32), jnp.full((half,), np.pi / 2, F32)]),
                     ROPE_PACK)[None, :]
    rows = s // ROPE_PACK
    out = pl.pallas_call(
        _rope_kernel,
        grid=(b,),
        in_specs=[pl.BlockSpec((1, rows, ROPE_PACK), lambda i: (i, 0, 0)),
                  pl.BlockSpec((1, ROPE_PACK * MLA_ROPE), lambda i: (0, 0)),
                  pl.BlockSpec((1, ROPE_PACK * MLA_ROPE), lambda i: (0, 0))],
        out_specs=pl.BlockSpec((1, rows, ROPE_PACK * MLA_ROPE), lambda i: (i, 0, 0)),
        out_shape=jax.ShapeDtypeStruct((b, rows, ROPE_PACK * MLA_ROPE), F32),
        compiler_params=pltpu.CompilerParams(dimension_semantics=("parallel",)),
        name="rope",
    )(positions.reshape(b, rows, ROPE_PACK), freq, phase)
    return out.reshape(b, s, MLA_ROPE)


def kernel(x, p, positions, norm_in_g, w_in, q_norm_g, w_uq, kv_norm_g, w_ukv, w_gk_up, b_gk,
           gla_norm_g, w_mla_br, w_gla_br, w_out, w_ple, ple_norm_g, ple_gate_norm_g, w_ple_gate,
           final_norm_g):
    b, s, d = x.shape
    depth = w_in.shape[0]
    assert s % PROJ_ROWS == 0 and PROJ_ROWS % TILE == 0
    cs = _rope_table(positions)
    seg_np, level_np = _gla_constants()
    seg = jnp.asarray(seg_np, BF16)
    level = jnp.asarray(level_np)

    sizes = (MLA_Q_RANK, MLA_KV_RANK, MLA_ROPE, MLA_WIDTH, GLA_DK, GLA_DK, GLA_DV, GLA_GATE_RANK,
             GLA_DV, d, d)
    offs = np.concatenate([[0], np.cumsum(sizes)])

    def cols(w, idx):
        return w[:, int(offs[idx]):int(offs[idx + 1])]

    for l in range(depth):
        w = w_in[l]
        wkr = cols(w, 2)
        wstd = jnp.concatenate(
            [cols(w, 0), cols(w, 1), jnp.zeros((d, MLA_NOPE), F32), wkr, _swap_halves(wkr),
             cols(w, 9) * 0.5, cols(w, 10) * 0.5], axis=1).astype(BF16)
        wt = jnp.concatenate(
            [cols(w, 3) * 0.5, cols(w, 4) * (GLA_HEAD_K ** -0.5), cols(w, 5), cols(w, 6),
             cols(w, 8) * 0.5, cols(w, 7)], axis=1).T.astype(BF16)
        uq = w_uq[l].reshape(MLA_Q_RANK, MLA_HEADS, MLA_NOPE + MLA_ROPE)
        uq_r = uq[..., MLA_NOPE:]
        wq = jnp.concatenate([uq[..., :MLA_NOPE], uq_r, _swap_halves(uq_r)], axis=-1)
        wq = wq.reshape(MLA_Q_RANK, MLA_HEADS * HEAD_LANES).astype(BF16)
        ukv = w_ukv[l].reshape(MLA_KV_RANK, MLA_HEADS, MLA_NOPE + MLA_V)
        wk = jnp.concatenate([ukv[..., :MLA_NOPE], jnp.zeros_like(ukv[..., :MLA_NOPE])], axis=-1)
        wk = wk.reshape(MLA_KV_RANK, MLA_HEADS * HEAD_LANES).astype(BF16)
        wvt = ukv[..., MLA_NOPE:].reshape(MLA_KV_RANK, MLA_WIDTH).T.astype(BF16)

        (q, k, vt, gmla, gq, gk, gv, ggla, glow, ma, mb) = _projections(
            x, cs, norm_in_g[l][None, :], wstd, wt, q_norm_g[l][None, :], wq,
            kv_norm_g[l][None, :], wk, wvt)

        o_mla = _mla_attention(q, k, vt)
        o_gla = _gla(gq, gk, gv, glow, w_gk_up[l].T.astype(BF16), b_gk[l][:, None], seg, level,
                     gla_norm_g[l][:, None])

        x = _output(x, p[l], o_mla, gmla, o_gla, ggla, ma, mb, w_mla_br[l].astype(BF16),
                    w_gla_br[l].astype(BF16), (w_out[l] * 0.5).astype(BF16), w_ple[l].astype(BF16),
                    ple_norm_g[l][None, :] * 0.5, ple_gate_norm_g[l][None, :],
                    (w_ple_gate[l] * 0.5).astype(BF16),
                    final_norm_g[None, :], l == depth - 1)
    return x
```

```python
import functools

import numpy as np
import jax
import jax.numpy as jnp
from jax import lax
from jax.experimental import pallas as pl
from jax.experimental.pallas import tpu as pltpu

F32 = jnp.float32
BF16 = jnp.bfloat16

MLA_HEADS = 8
MLA_Q_RANK = 384
MLA_KV_RANK = 256
MLA_NOPE = 64
MLA_ROPE = 32
MLA_V = 64
MLA_WIDTH = MLA_HEADS * MLA_V
ROPE_THETA = 10000.0
GLA_HEADS = 4
GLA_HEAD_K = 64
GLA_HEAD_V = 128
GLA_DK = GLA_HEADS * GLA_HEAD_K
GLA_DV = GLA_HEADS * GLA_HEAD_V
GLA_GATE_RANK = 16
GLA_GATE_NORMALIZER = 16.0
EPS = 1e-6
LOG2E = 1.4426950408889634

HEAD_LANES = 128
TILE = 256
PROJ_ROWS = 512
MLA_HEADS_PER_STEP = 4
MLA_VX = MLA_V + 16
ROPE_PACK = 128 // MLA_ROPE
VMEM_LIMIT = 56 * 1024 * 1024

GLA_BLOCK = 16
GLA_TILES_PER_STEP = 4
HALF = TILE // 2
GLA_LEVELS = (HALF, 64, 32, 16)
GLA_NFACT = 2 + len(GLA_LEVELS)
GLA_BAND = len(GLA_LEVELS)

NT = (((1,), (1,)), ((), ()))
TN = (((0,), (0,)), ((), ()))


def _dot(a, b):
    return jnp.dot(a, b, preferred_element_type=F32)


def _dot_nt(a, b):
    return lax.dot_general(a, b, NT, preferred_element_type=F32)


def _dot_tn(a, b):
    return lax.dot_general(a, b, TN, preferred_element_type=F32)


def _rms_rows(x, g):
    return x * lax.rsqrt(jnp.mean(x * x, axis=-1, keepdims=True) + EPS) * g


def _proj_kernel(x_ref, cs_ref, gin_ref, wstd_ref, wt_ref, gq_ref, wq_ref,
                 gkv_ref, wk_ref, wvt_ref,
                 q_out, k_out, vt_out, gmla_out, gq_out, gk_out, gv_out, ggla_out, glow_out,
                 ma_out, mb_out):
    h = _rms_rows(x_ref[0], gin_ref[...]).astype(BF16)

    ht = _dot_nt(wt_ref[...], h)
    for out, lo, hi in ((gmla_out, 0, 512), (gq_out, 512, 768), (gk_out, 768, 1024),
                        (gv_out, 1024, 1536), (ggla_out, 1536, 2048), (glow_out, 2048, 2064)):
        out[0] = ht[lo:hi, :].astype(BF16)
    ma_out[0] = _dot(h, wstd_ref[:, 768:1792]).astype(BF16)
    mb_out[0] = _dot(h, wstd_ref[:, 1792:2816]).astype(BF16)

    cos, sin = cs_ref[0, :, 0:MLA_ROPE // 2], cs_ref[0, :, MLA_ROPE // 2:MLA_ROPE]
    table = jnp.concatenate([jnp.ones((PROJ_ROWS, MLA_NOPE), F32), cos, cos, sin, sin], axis=1)

    cq = _rms_rows(_dot(h, wstd_ref[:, 0:384]), gq_ref[...]).astype(BF16)
    q = _dot(cq, wq_ref[...])
    qscale = table * ((MLA_NOPE + MLA_ROPE) ** -0.5 * LOG2E)
    for hh in range(MLA_HEADS):
        q_out[0, hh] = (q[:, hh * HEAD_LANES:(hh + 1) * HEAD_LANES] * qscale).astype(BF16)

    ckv = _rms_rows(_dot(h, wstd_ref[:, 384:640]), gkv_ref[...]).astype(BF16)
    u = _dot(h, wstd_ref[:, 640:768]) * table
    lane = lax.broadcasted_iota(jnp.int32, u.shape, 1)
    krot = jnp.where(lane >= MLA_NOPE, u + pltpu.roll(u, 32, 1) + pltpu.roll(u, 96, 1), 0.0)
    kk = _dot(ckv, wk_ref[...])
    for hh in range(MLA_HEADS):
        k_out[0, hh] = (kk[:, hh * HEAD_LANES:(hh + 1) * HEAD_LANES] + krot).astype(BF16)

    vt = _dot_nt(wvt_ref[...], ckv)
    for hh in range(MLA_HEADS):
        vt_out[0, hh] = vt[hh * MLA_V:(hh + 1) * MLA_V, :].astype(BF16)


def _projections(x, cs, gin, wstd, wt, gq, wq, gkv, wk, wvt):
    b, s, d = x.shape
    rows = PROJ_ROWS

    def full(a):
        return pl.BlockSpec(a.shape, lambda i, j: (0,) * a.ndim)

    def tok(width):
        return pl.BlockSpec((1, rows, width), lambda i, j: (i, j, 0))

    def tr(n):
        return pl.BlockSpec((1, n, rows), lambda i, j: (i, 0, j))

    def tr_shape(n):
        return jax.ShapeDtypeStruct((b, n, s), BF16)

    heads_tok = pl.BlockSpec((1, MLA_HEADS, rows, HEAD_LANES), lambda i, j: (i, 0, j, 0))
    heads_tr = pl.BlockSpec((1, MLA_HEADS, MLA_V, rows), lambda i, j: (i, 0, 0, j))
    out_shape = (
        jax.ShapeDtypeStruct((b, MLA_HEADS, s, HEAD_LANES), BF16),
        jax.ShapeDtypeStruct((b, MLA_HEADS, s, HEAD_LANES), BF16),
        jax.ShapeDtypeStruct((b, MLA_HEADS, MLA_V, s), BF16),
        tr_shape(MLA_WIDTH),
        tr_shape(GLA_DK),
        tr_shape(GLA_DK),
        tr_shape(GLA_DV),
        tr_shape(GLA_DV),
        tr_shape(GLA_GATE_RANK),
        jax.ShapeDtypeStruct((b, s, d), BF16),
        jax.ShapeDtypeStruct((b, s, d), BF16),
    )
    out_specs = (heads_tok, heads_tok, heads_tr, tr(MLA_WIDTH),
                 tr(GLA_DK), tr(GLA_DK), tr(GLA_DV), tr(GLA_DV), tr(GLA_GATE_RANK), tok(d), tok(d))
    in_specs = [tok(d), tok(MLA_ROPE), full(gin), full(wstd), full(wt), full(gq),
                full(wq), full(gkv), full(wk), full(wvt)]
    return pl.pallas_call(
        _proj_kernel,
        grid=(b, s // rows),
        in_specs=in_specs,
        out_specs=out_specs,
        out_shape=out_shape,
        compiler_params=pltpu.CompilerParams(
            dimension_semantics=("parallel", "parallel"), vmem_limit_bytes=VMEM_LIMIT),
        name="proj",
    )(x, cs, gin, wstd, wt, gq, wq, gkv, wk, wvt)


def _mla_kernel(q_ref, k_ref, vt_ref, o_ref, s_ref, p_ref, vx_ref):
    heads, seq = q_ref.shape[1], q_ref.shape[2]
    row = lax.broadcasted_iota(jnp.int32, (TILE, TILE), 0)
    col = lax.broadcasted_iota(jnp.int32, (TILE, TILE), 1)
    causal = row <= col
    tasks = [(hh, qi) for hh in range(heads) for qi in range(seq // TILE)]

    for hh in range(heads):
        vx_ref[hh, 0:MLA_V, :] = vt_ref[0, hh]
        vx_ref[hh, MLA_V:, :] = jnp.ones((MLA_VX - MLA_V, seq), BF16)

    def scores(t):
        hh, qi = tasks[t]
        lo, hi = qi * TILE, (qi + 1) * TILE
        qt = q_ref[0, hh, lo:hi, :]
        if qi:
            s_ref[t % 2, 0:lo, :] = _dot_nt(k_ref[0, hh, 0:lo, :], qt)
        s_ref[t % 2, lo:hi, :] = jnp.where(causal, _dot_nt(k_ref[0, hh, lo:hi, :], qt), -jnp.inf)

    def softmax(t):
        _, qi = tasks[t]
        m = jnp.full((1, TILE), -jnp.inf, F32)
        for kj in range(qi + 1):
            m = jnp.maximum(m, jnp.max(s_ref[t % 2, kj * TILE:(kj + 1) * TILE, :], axis=0, keepdims=True))
        for kj in range(qi + 1):
            blk = slice(kj * TILE, (kj + 1) * TILE)
            p_ref[t % 2, blk, :] = jnp.exp2(s_ref[t % 2, blk, :] - m).astype(BF16)

    def weighted_values(t):
        hh, qi = tasks[t]
        lo, hi = qi * TILE, (qi + 1) * TILE
        acc = _dot(vx_ref[hh, :, 0:hi], p_ref[t % 2, 0:hi, :])
        o_ref[0, hh, :, lo:hi] = (acc[0:MLA_V] / acc[MLA_V:MLA_V + 1]).astype(BF16)

    scores(0)
    for t in range(len(tasks)):
        if t + 1 < len(tasks):
            scores(t + 1)
        if t:
            weighted_values(t - 1)
        softmax(t)
    weighted_values(len(tasks) - 1)


def _mla_attention(q, k, vt):
    b, heads, s, _ = q.shape
    hp = MLA_HEADS_PER_STEP
    tok = pl.BlockSpec((1, hp, s, HEAD_LANES), lambda i, h: (i, h, 0, 0))
    tr = pl.BlockSpec((1, hp, MLA_V, s), lambda i, h: (i, h, 0, 0))
    return pl.pallas_call(
        _mla_kernel,
        grid=(b, heads // hp),
        in_specs=[tok, tok, tr],
        out_specs=tr,
        out_shape=jax.ShapeDtypeStruct((b, heads, MLA_V, s), BF16),
        scratch_shapes=[pltpu.VMEM((2, s, TILE), F32), pltpu.VMEM((2, s, TILE), BF16),
                        pltpu.VMEM((hp, MLA_VX, s), BF16)],
        compiler_params=pltpu.CompilerParams(
            dimension_semantics=("parallel", "parallel"), vmem_limit_bytes=VMEM_LIMIT),
        name="mla",
    )(q, k, vt)


def _gla_constants():
    u = np.arange(TILE)[:, None]
    t = np.arange(TILE)[None, :]
    mats = [(u <= t), (u > t)]
    for m in GLA_LEVELS:
        pos = t % (2 * m)
        mid = t - pos + m - 1
        upper = pos >= m
        mats.append(np.where(upper, (u > mid) & (u <= t), (u > t) & (u <= mid)))
    seg = np.concatenate([mm.astype(np.float32) for mm in mats], axis=1)
    seg = np.concatenate([seg, seg, seg], axis=0)
    j = np.arange(TILE)[:, None]
    i = np.arange(TILE)[None, :]
    level = np.full((TILE, TILE), -1, np.int32)
    level[(j // GLA_BLOCK) == (i // GLA_BLOCK)] = GLA_BAND
    for idx, m in enumerate(GLA_LEVELS):
        same = (j // (2 * m)) == (i // (2 * m))
        split = ((j % (2 * m)) < m) & ((i % (2 * m)) >= m)
        level[same & split & (j < i)] = idx
    return seg, level[:HALF, :HALF]


def _gla_band(qf, kf, decay):
    rel = (lax.broadcasted_iota(jnp.int32, (GLA_BLOCK, TILE), 1) % GLA_BLOCK
           - lax.broadcasted_iota(jnp.int32, (GLA_BLOCK, TILE), 0))
    w = [kf]
    span = decay
    step = 1
    while step < GLA_BLOCK:
        w += [span * pltpu.roll(w[d], step, 1) for d in range(step)]
        span = span * pltpu.roll(span, step, 1)
        step *= 2
    bands = [jnp.zeros((GLA_BLOCK, TILE), F32) for _ in range(GLA_HEADS)]
    for d in range(GLA_BLOCK):
        prod = qf * w[d]
        for hh in range(GLA_HEADS):
            diag = jnp.sum(prod[hh * GLA_HEAD_K:(hh + 1) * GLA_HEAD_K], axis=0, keepdims=True)
            bands[hh] = jnp.where(rel == d, diag, bands[hh])
    return bands


def _gla_kernel(qt_ref, kt_ref, vt_ref, glow_ref, wg_ref, bg_ref, seg_ref, level_ref, gn_ref,
                o_ref, state_ref, fact_ref):
    @pl.when(pl.program_id(1) == 0)
    def _():
        state_ref[...] = jnp.zeros_like(state_ref)

    level = level_ref[...]
    lane = lax.broadcasted_iota(jnp.int32, (GLA_HEAD_K, TILE), 1)

    def prepare(sub):
        ts = slice(sub * TILE, (sub + 1) * TILE)
        z = _dot(wg_ref[...], glow_ref[0, :, ts]) + bg_ref[...]
        g = -(jnp.maximum(-z, 0.0) + jnp.log1p(jnp.exp(-jnp.abs(z)))) * (1.0 / GLA_GATE_NORMALIZER)
        g_hi = g.astype(BF16)
        r = g - g_hi.astype(F32)
        g_mid = r.astype(BF16)
        g_lo = (r - g_mid.astype(F32)).astype(BF16)
        g3 = jnp.concatenate([g_hi, g_mid, g_lo], axis=1)
        for n in range(GLA_NFACT):
            fact_ref[sub, n] = jnp.exp(_dot(g3, seg_ref[:, n * TILE:(n + 1) * TILE]))
        return _gla_band(qt_ref[0, :, ts].astype(F32), kt_ref[0, :, ts].astype(F32), jnp.exp(g))

    def attend(sub, bands):
        ts = slice(sub * TILE, (sub + 1) * TILE)
        for hh in range(GLA_HEADS):
            rk = slice(hh * GLA_HEAD_K, (hh + 1) * GLA_HEAD_K)
            rv = slice(hh * GLA_HEAD_V, (hh + 1) * GLA_HEAD_V)
            qf = qt_ref[0, rk, ts].astype(F32)
            kf = kt_ref[0, rk, ts].astype(F32)
            vt = vt_ref[0, rv, ts]
            f_in = fact_ref[sub, 0, rk, :]
            q_in = (qf * f_in).astype(BF16)
            k_out = (kf * fact_ref[sub, 1, rk, :]).astype(BF16)
            state = state_ref[hh]
            o = _dot_tn(state.astype(BF16), q_in)

            far = []
            for idx, m in enumerate(GLA_LEVELS):
                f = fact_ref[sub, 2 + idx, rk, :]
                upper = (lane // m) % 2 == 1
                q_up = jnp.where(upper, qf * f, 0.0).astype(BF16)
                k_lo = jnp.where(upper, 0.0, kf * f).astype(BF16)
                far.append(_dot_tn(k_lo, q_up))
            band = jnp.tile(bands[hh], (HALF // GLA_BLOCK, 1))

            def diagonal(c):
                sl = slice(c * HALF, (c + 1) * HALF)
                blk = far[1][sl, sl]
                for idx in range(2, len(GLA_LEVELS)):
                    blk = jnp.where(level == idx, far[idx][sl, sl], blk)
                return jnp.where(level == GLA_BAND, band[:, sl], blk)

            at = jnp.concatenate(
                [jnp.concatenate([diagonal(0), far[0][0:HALF, HALF:]], axis=1),
                 jnp.concatenate([jnp.zeros((HALF, HALF), F32), diagonal(1)], axis=1)], axis=0)
            o = o + _dot(vt, at.astype(BF16))

            ms = jnp.mean(o * o, axis=0, keepdims=True)
            o_ref[0, rv, ts] = (o * lax.rsqrt(ms + EPS) * gn_ref[...]).astype(BF16)

            state_ref[hh] = state * f_in[:, TILE - 1:TILE] + _dot_nt(k_out, vt)

    prepared = [prepare(sub) for sub in range(GLA_TILES_PER_STEP)]
    for sub in range(GLA_TILES_PER_STEP):
        attend(sub, prepared[sub])


def _gla(qt, kt, vt, glow, wg, bg, seg, level, gn):
    b, _, s = qt.shape
    span = GLA_TILES_PER_STEP * TILE

    def full(a):
        return pl.BlockSpec(a.shape, lambda i, j: (0,) * a.ndim)

    def tr(n):
        return pl.BlockSpec((1, n, span), lambda i, j: (i, 0, j))

    return pl.pallas_call(
        _gla_kernel,
        grid=(b, s // span),
        in_specs=[tr(GLA_DK), tr(GLA_DK), tr(GLA_DV), tr(GLA_GATE_RANK), full(wg), full(bg),
                  full(seg), full(level), full(gn)],
        out_specs=tr(GLA_DV),
        out_shape=jax.ShapeDtypeStruct((b, GLA_DV, s), BF16),
        scratch_shapes=[pltpu.VMEM((GLA_HEADS, GLA_HEAD_K, GLA_HEAD_V), F32),
                        pltpu.VMEM((GLA_TILES_PER_STEP, GLA_NFACT, GLA_DK, TILE), F32)],
        compiler_params=pltpu.CompilerParams(
            dimension_semantics=("parallel", "arbitrary"), vmem_limit_bytes=VMEM_LIMIT),
        name="gla",
    )(qt, kt, vt, glow, wg, bg, seg, level, gn)


def _out_kernel(x_ref, p_ref, om_ref, gm_ref, og_ref, gg_ref, ma_ref, mb_ref, wa_ref, wb_ref,
                wout_ref, wple_ref, gple_ref, ggate_ref, wgate_ref, gfin_ref, o_ref, *, final):
    def branch(o_t, g_t_ref, w_ref):
        gh = g_t_ref[0].astype(F32)
        zt = (o_t.astype(F32) * (gh * (jnp.tanh(gh) + 1.0))).astype(BF16)
        return _dot_tn(zt, w_ref[...])

    ya = branch(om_ref[0].reshape(MLA_WIDTH, PROJ_ROWS), gm_ref, wa_ref)
    yb = branch(og_ref[0], gg_ref, wb_ref)
    merged2 = ((jnp.tanh(ma_ref[0].astype(F32)) + 1.0) * ya
               + (jnp.tanh(mb_ref[0].astype(F32)) + 1.0) * yb)
    x1 = x_ref[0] + _dot(merged2.astype(BF16), wout_ref[...])
    e_half = _rms_rows(_dot(p_ref[0].astype(BF16), wple_ref[...]), gple_ref[...])
    gate2 = jnp.tanh(_dot(_rms_rows(x1, ggate_ref[...]).astype(BF16), wgate_ref[...])) + 1.0
    x2 = x1 + gate2 * e_half
    o_ref[0] = _rms_rows(x2, gfin_ref[...]) if final else x2


def _output(x, p, om, gm, og, gg, ma, mb, wa, wb, wout, wple, gple, ggate, wgate, gfin, final):
    b, s, d = x.shape
    rows = PROJ_ROWS

    def full(a):
        return pl.BlockSpec(a.shape, lambda i, j: (0,) * a.ndim)

    def tok(width):
        return pl.BlockSpec((1, rows, width), lambda i, j: (i, j, 0))

    def tr(n):
        return pl.BlockSpec((1, n, rows), lambda i, j: (i, 0, j))

    heads_tr = pl.BlockSpec((1, MLA_HEADS, MLA_V, rows), lambda i, j: (i, 0, 0, j))
    in_specs = [tok(d), tok(p.shape[-1]), heads_tr, tr(MLA_WIDTH), tr(GLA_DV), tr(GLA_DV),
                tok(d), tok(d), full(wa), full(wb), full(wout), full(wple), full(gple), full(ggate),
                full(wgate), full(gfin)]
    return pl.pallas_call(
        functools.partial(_out_kernel, final=final),
        grid=(b, s // rows),
        in_specs=in_specs,
        out_specs=tok(d),
        out_shape=jax.ShapeDtypeStruct((b, s, d), F32),
        compiler_params=pltpu.CompilerParams(
            dimension_semantics=("parallel", "parallel"), vmem_limit_bytes=VMEM_LIMIT),
        name="out",
    )(x, p, om, gm, og, gg, ma, mb, wa, wb, wout, wple, gple, ggate, wgate, gfin)


def _swap_halves(w):
    half = w.shape[-1] // 2
    return jnp.concatenate([-w[..., half:], w[..., :half]], axis=-1)


def _rope_kernel(pos_ref, freq_ref, phase_ref, o_ref):
    pos = pos_ref[0].astype(F32)
    slot = lax.broadcasted_iota(jnp.int32, o_ref.shape[1:], 1) // MLA_ROPE
    posx = pos[:, 0:1]
    for t in range(1, ROPE_PACK):
        posx = jnp.where(slot == t, pos[:, t:t + 1], posx)
    o_ref[0] = jnp.cos(posx * freq_ref[...] - phase_ref[...])


def _rope_table(positions):
    b, s = positions.shape
    inv_freq = 1.0 / (ROPE_THETA ** (jnp.arange(0, MLA_ROPE, 2, dtype=F32) / MLA_ROPE))
    half = MLA_ROPE // 2
    freq = jnp.tile(jnp.concatenate([inv_freq, inv_freq]), ROPE_PACK)[None, :]
    phase = jnp.tile(jnp.concatenate([jnp.zeros((half,), F32), jnp.full((half,), np.pi / 2, F32)]),
                     ROPE_PACK)[None, :]
    rows = s // ROPE_PACK
    out = pl.pallas_call(
        _rope_kernel,
        grid=(b,),
        in_specs=[pl.BlockSpec((1, rows, ROPE_PACK), lambda i: (i, 0, 0)),
                  pl.BlockSpec((1, ROPE_PACK * MLA_ROPE), lambda i: (0, 0)),
                  pl.BlockSpec((1, ROPE_PACK * MLA_ROPE), lambda i: (0, 0))],
        out_specs=pl.BlockSpec((1, rows, ROPE_PACK * MLA_ROPE), lambda i: (i, 0, 0)),
        out_shape=jax.ShapeDtypeStruct((b, rows, ROPE_PACK * MLA_ROPE), F32),
        compiler_params=pltpu.CompilerParams(dimension_semantics=("parallel",)),
        name="rope",
    )(positions.reshape(b, rows, ROPE_PACK), freq, phase)
    return out.reshape(b, s, MLA_ROPE)


def kernel(x, p, positions, norm_in_g, w_in, q_norm_g, w_uq, kv_norm_g, w_ukv, w_gk_up, b_gk,
           gla_norm_g, w_mla_br, w_gla_br, w_out, w_ple, ple_norm_g, ple_gate_norm_g, w_ple_gate,
           final_norm_g):
    b, s, d = x.shape
    depth = w_in.shape[0]
    assert s % PROJ_ROWS == 0 and PROJ_ROWS % TILE == 0
    cs = _rope_table(positions)
    seg_np, level_np = _gla_constants()
    seg = jnp.asarray(seg_np, BF16)
    level = jnp.asarray(level_np)

    sizes = (MLA_Q_RANK, MLA_KV_RANK, MLA_ROPE, MLA_WIDTH, GLA_DK, GLA_DK, GLA_DV, GLA_GATE_RANK,
             GLA_DV, d, d)
    offs = np.concatenate([[0], np.cumsum(sizes)])

    def cols(w, idx):
        return w[:, int(offs[idx]):int(offs[idx + 1])]

    for l in range(depth):
        w = w_in[l]
        wkr = cols(w, 2)
        wstd = jnp.concatenate(
            [cols(w, 0), cols(w, 1), jnp.zeros((d, MLA_NOPE), F32), wkr, _swap_halves(wkr),
             cols(w, 9) * 0.5, cols(w, 10) * 0.5], axis=1).astype(BF16)
        wt = jnp.concatenate(
            [cols(w, 3) * 0.5, cols(w, 4) * (GLA_HEAD_K ** -0.5), cols(w, 5), cols(w, 6),
             cols(w, 8) * 0.5, cols(w, 7)], axis=1).T.astype(BF16)
        uq = w_uq[l].reshape(MLA_Q_RANK, MLA_HEADS, MLA_NOPE + MLA_ROPE)
        uq_r = uq[..., MLA_NOPE:]
        wq = jnp.concatenate([uq[..., :MLA_NOPE], uq_r, _swap_halves(uq_r)], axis=-1)
        wq = wq.reshape(MLA_Q_RANK, MLA_HEADS * HEAD_LANES).astype(BF16)
        ukv = w_ukv[l].reshape(MLA_KV_RANK, MLA_HEADS, MLA_NOPE + MLA_V)
        wk = jnp.concatenate([ukv[..., :MLA_NOPE], jnp.zeros_like(ukv[..., :MLA_NOPE])], axis=-1)
        wk = wk.reshape(MLA_KV_RANK, MLA_HEADS * HEAD_LANES).astype(BF16)
        wvt = ukv[..., MLA_NOPE:].reshape(MLA_KV_RANK, MLA_WIDTH).T.astype(BF16)

        (q, k, vt, gmla, gq, gk, gv, ggla, glow, ma, mb) = _projections(
            x, cs, norm_in_g[l][None, :], wstd, wt, q_norm_g[l][None, :], wq,
            kv_norm_g[l][None, :], wk, wvt)

        o_mla = _mla_attention(q, k, vt)
        o_gla = _gla(gq, gk, gv, glow, w_gk_up[l].T.astype(BF16), b_gk[l][:, None], seg, level,
                     gla_norm_g[l][:, None])

        x = _output(x, p[l], o_mla, gmla, o_gla, ggla, ma, mb, w_mla_br[l].astype(BF16),
                    w_gla_br[l].astype(BF16), (w_out[l] * 0.5).astype(BF16), w_ple[l].astype(BF16),
                    ple_norm_g[l][None, :] * 0.5, ple_gate_norm_g[l][None, :],
                    (w_ple_gate[l] * 0.5).astype(BF16),
                    final_norm_g[None, :], l == depth - 1)
    return x
```

```python
import functools

import numpy as np
import jax
import jax.numpy as jnp
from jax import lax
from jax.experimental import pallas as pl
from jax.experimental.pallas import tpu as pltpu

F32 = jnp.float32
BF16 = jnp.bfloat16

MLA_HEADS = 8
MLA_Q_RANK = 384
MLA_KV_RANK = 256
MLA_NOPE = 64
MLA_ROPE = 32
MLA_V = 64
MLA_WIDTH = MLA_HEADS * MLA_V
ROPE_THETA = 10000.0
GLA_HEADS = 4
GLA_HEAD_K = 64
GLA_HEAD_V = 128
GLA_DK = GLA_HEADS * GLA_HEAD_K
GLA_DV = GLA_HEADS * GLA_HEAD_V
GLA_GATE_RANK = 16
GLA_GATE_NORMALIZER = 16.0
EPS = 1e-6
LOG2E = 1.4426950408889634

HEAD_LANES = 128
TILE = 256
PROJ_ROWS = 512
MIX_SPLIT = 2
MLA_VX = MLA_V + 16
ROPE_PACK = 128 // MLA_ROPE
VMEM_LIMIT = 56 * 1024 * 1024

GLA_BLOCK = 16
HALF = TILE // 2
GLA_LEVELS = (HALF, 64, 32, 16)
GLA_NFACT = 2 + len(GLA_LEVELS)
GLA_BAND = len(GLA_LEVELS)

NT = (((1,), (1,)), ((), ()))
TN = (((0,), (0,)), ((), ()))


def _dot(a, b):
    return jnp.dot(a, b, preferred_element_type=F32)


def _dot_nt(a, b):
    return lax.dot_general(a, b, NT, preferred_element_type=F32)


def _dot_tn(a, b):
    return lax.dot_general(a, b, TN, preferred_element_type=F32)


def _rms_rows(x, g):
    return x * lax.rsqrt(jnp.mean(x * x, axis=-1, keepdims=True) + EPS) * g


def _proj_kernel(x_ref, cs_ref, gin_ref, wstd_ref, wt_ref, gq_ref, wq_ref,
                 gkv_ref, wk_ref, wvt_ref,
                 q_out, k_out, vt_out, gmla_out, gq_out, gk_out, gv_out, ggla_out, glow_out,
                 ma_out, mb_out):
    h = _rms_rows(x_ref[0], gin_ref[...]).astype(BF16)

    ht = _dot_nt(wt_ref[...], h)
    for out, lo, hi in ((gmla_out, 0, 512), (gq_out, 512, 768), (gk_out, 768, 1024),
                        (gv_out, 1024, 1536), (ggla_out, 1536, 2048), (glow_out, 2048, 2064)):
        out[0] = ht[lo:hi, :].astype(BF16)
    ma_out[0] = _dot(h, wstd_ref[:, 768:1792]).astype(BF16)
    mb_out[0] = _dot(h, wstd_ref[:, 1792:2816]).astype(BF16)

    cos, sin = cs_ref[0, :, 0:MLA_ROPE // 2], cs_ref[0, :, MLA_ROPE // 2:MLA_ROPE]
    table = jnp.concatenate([jnp.ones((PROJ_ROWS, MLA_NOPE), F32), cos, cos, sin, sin], axis=1)

    cq = _rms_rows(_dot(h, wstd_ref[:, 0:384]), gq_ref[...]).astype(BF16)
    q = _dot(cq, wq_ref[...])
    qscale = table * ((MLA_NOPE + MLA_ROPE) ** -0.5 * LOG2E)
    for hh in range(MLA_HEADS):
        q_out[0, hh] = (q[:, hh * HEAD_LANES:(hh + 1) * HEAD_LANES] * qscale).astype(BF16)

    ckv = _rms_rows(_dot(h, wstd_ref[:, 384:640]), gkv_ref[...]).astype(BF16)
    u = _dot(h, wstd_ref[:, 640:768]) * table
    lane = lax.broadcasted_iota(jnp.int32, u.shape, 1)
    krot = jnp.where(lane >= MLA_NOPE, u + pltpu.roll(u, 32, 1) + pltpu.roll(u, 96, 1), 0.0)
    kk = _dot(ckv, wk_ref[...])
    for hh in range(MLA_HEADS):
        k_out[0, hh] = (kk[:, hh * HEAD_LANES:(hh + 1) * HEAD_LANES] + krot).astype(BF16)

    vt = _dot_nt(wvt_ref[...], ckv)
    for hh in range(MLA_HEADS):
        vt_out[0, hh] = vt[hh * MLA_V:(hh + 1) * MLA_V, :].astype(BF16)


def _projections(x, cs, gin, wstd, wt, gq, wq, gkv, wk, wvt):
    b, s, d = x.shape
    rows = PROJ_ROWS

    def full(a):
        return pl.BlockSpec(a.shape, lambda i, j: (0,) * a.ndim)

    def tok(width):
        return pl.BlockSpec((1, rows, width), lambda i, j: (i, j, 0))

    def tr(n):
        return pl.BlockSpec((1, n, rows), lambda i, j: (i, 0, j))

    def tr_shape(n):
        return jax.ShapeDtypeStruct((b, n, s), BF16)

    heads_tok = pl.BlockSpec((1, MLA_HEADS, rows, HEAD_LANES), lambda i, j: (i, 0, j, 0))
    heads_tr = pl.BlockSpec((1, MLA_HEADS, MLA_V, rows), lambda i, j: (i, 0, 0, j))
    out_shape = (
        jax.ShapeDtypeStruct((b, MLA_HEADS, s, HEAD_LANES), BF16),
        jax.ShapeDtypeStruct((b, MLA_HEADS, s, HEAD_LANES), BF16),
        jax.ShapeDtypeStruct((b, MLA_HEADS, MLA_V, s), BF16),
        tr_shape(MLA_WIDTH),
        tr_shape(GLA_DK),
        tr_shape(GLA_DK),
        tr_shape(GLA_DV),
        tr_shape(GLA_DV),
        tr_shape(GLA_GATE_RANK),
        jax.ShapeDtypeStruct((b, s, d), BF16),
        jax.ShapeDtypeStruct((b, s, d), BF16),
    )
    out_specs = (heads_tok, heads_tok, heads_tr, tr(MLA_WIDTH),
                 tr(GLA_DK), tr(GLA_DK), tr(GLA_DV), tr(GLA_DV), tr(GLA_GATE_RANK), tok(d), tok(d))
    in_specs = [tok(d), tok(MLA_ROPE), full(gin), full(wstd), full(wt), full(gq),
                full(wq), full(gkv), full(wk), full(wvt)]
    return pl.pallas_call(
        _proj_kernel,
        grid=(b, s // rows),
        in_specs=in_specs,
        out_specs=out_specs,
        out_shape=out_shape,
        compiler_params=pltpu.CompilerParams(
            dimension_semantics=("parallel", "parallel"), vmem_limit_bytes=VMEM_LIMIT),
        name="proj",
    )(x, cs, gin, wstd, wt, gq, wq, gkv, wk, wvt)


def _mla_stages(q_ref, k_ref, vt_ref, o_ref, s_ref, p_ref, vx_ref):
    heads, seq = q_ref.shape[1], q_ref.shape[2]
    row = lax.broadcasted_iota(jnp.int32, (TILE, TILE), 0)
    col = lax.broadcasted_iota(jnp.int32, (TILE, TILE), 1)
    causal = row <= col
    tasks = [(hh, qi) for hh in range(heads) for qi in range(seq // TILE)]

    for hh in range(heads):
        vx_ref[hh, 0:MLA_V, :] = vt_ref[0, hh]
        vx_ref[hh, MLA_V:, :] = jnp.ones((MLA_VX - MLA_V, seq), BF16)

    def scores(t):
        hh, qi = tasks[t]
        lo, hi = qi * TILE, (qi + 1) * TILE
        qt = q_ref[0, hh, lo:hi, :]
        if qi:
            s_ref[t % 2, 0:lo, :] = _dot_nt(k_ref[0, hh, 0:lo, :], qt)
        s_ref[t % 2, lo:hi, :] = jnp.where(causal, _dot_nt(k_ref[0, hh, lo:hi, :], qt), -jnp.inf)

    def softmax(t):
        _, qi = tasks[t]
        m = jnp.full((1, TILE), -jnp.inf, F32)
        for kj in range(qi + 1):
            m = jnp.maximum(m, jnp.max(s_ref[t % 2, kj * TILE:(kj + 1) * TILE, :], axis=0, keepdims=True))
            yield
        for kj in range(qi + 1):
            blk = slice(kj * TILE, (kj + 1) * TILE)
            p_ref[t % 2, blk, :] = jnp.exp2(s_ref[t % 2, blk, :] - m).astype(BF16)
            yield

    def weighted_values(t):
        hh, qi = tasks[t]
        lo, hi = qi * TILE, (qi + 1) * TILE
        acc = _dot(vx_ref[hh, :, 0:hi], p_ref[t % 2, 0:hi, :])
        o_ref[0, hh, :, lo:hi] = (acc[0:MLA_V] / acc[MLA_V:MLA_V + 1]).astype(BF16)

    def program():
        scores(0)
        yield
        for t in range(len(tasks)):
            if t + 1 < len(tasks):
                scores(t + 1)
                yield
            if t:
                weighted_values(t - 1)
                yield
            yield from softmax(t)
        weighted_values(len(tasks) - 1)

    pieces = 2 + sum(2 + 2 * (qi + 1) for _, qi in tasks) - 2
    return program(), pieces


def _gla_constants():
    u = np.arange(TILE)[:, None]
    t = np.arange(TILE)[None, :]
    mats = [(u <= t), (u > t)]
    for m in GLA_LEVELS:
        pos = t % (2 * m)
        mid = t - pos + m - 1
        upper = pos >= m
        mats.append(np.where(upper, (u > mid) & (u <= t), (u > t) & (u <= mid)))
    seg = np.concatenate([mm.astype(np.float32) for mm in mats], axis=1)
    seg = np.concatenate([seg, seg, seg], axis=0)
    j = np.arange(TILE)[:, None]
    i = np.arange(TILE)[None, :]
    level = np.full((TILE, TILE), -1, np.int32)
    level[(j // GLA_BLOCK) == (i // GLA_BLOCK)] = GLA_BAND
    for idx, m in enumerate(GLA_LEVELS):
        same = (j // (2 * m)) == (i // (2 * m))
        split = ((j % (2 * m)) < m) & ((i % (2 * m)) >= m)
        level[same & split & (j < i)] = idx
    return seg, level[:HALF, :HALF]


def _gla_band(qf, kf, decay, band_ref):
    heads = qf.shape[0] // GLA_HEAD_K
    rel = (lax.broadcasted_iota(jnp.int32, (GLA_BLOCK, TILE), 1) % GLA_BLOCK
           - lax.broadcasted_iota(jnp.int32, (GLA_BLOCK, TILE), 0))
    w = [kf]
    span = decay
    step = 1
    while step < GLA_BLOCK:
        w += [span * pltpu.roll(w[d], step, 1) for d in range(step)]
        span = span * pltpu.roll(span, step, 1)
        step *= 2
        yield
    bands = [jnp.zeros((GLA_BLOCK, TILE), F32) for _ in range(heads)]
    for d in range(GLA_BLOCK):
        prod = qf * w[d]
        for hh in range(heads):
            diag = jnp.sum(prod[hh * GLA_HEAD_K:(hh + 1) * GLA_HEAD_K], axis=0, keepdims=True)
            bands[hh] = jnp.where(rel == d, diag, bands[hh])
        yield
    for hh in range(heads):
        band_ref[hh] = bands[hh]


def _gla_stages(qt_ref, kt_ref, vt_ref, glow_ref, wg_ref, bg_ref, seg_ref, level_ref, gn_ref,
                o_ref, state_ref, fact_ref, band_ref):
    heads = qt_ref.shape[1] // GLA_HEAD_K
    state_ref[...] = jnp.zeros_like(state_ref)
    level = level_ref[...]
    lane = lax.broadcasted_iota(jnp.int32, (GLA_HEAD_K, TILE), 1)

    def prepare(tile):
        ts = slice(tile * TILE, (tile + 1) * TILE)
        slot = tile % 2
        z = _dot(wg_ref[...], glow_ref[0, :, ts]) + bg_ref[...]
        g = -(jnp.maximum(-z, 0.0) + jnp.log1p(jnp.exp(-jnp.abs(z)))) * (1.0 / GLA_GATE_NORMALIZER)
        g_hi = g.astype(BF16)
        r = g - g_hi.astype(F32)
        g_mid = r.astype(BF16)
        g_lo = (r - g_mid.astype(F32)).astype(BF16)
        g3 = jnp.concatenate([g_hi, g_mid, g_lo], axis=1)
        yield
        for n in range(GLA_NFACT):
            fact_ref[slot, n] = jnp.exp(_dot(g3, seg_ref[:, n * TILE:(n + 1) * TILE]))
            yield
        yield from _gla_band(qt_ref[0, :, ts].astype(F32), kt_ref[0, :, ts].astype(F32), jnp.exp(g),
                             band_ref.at[slot])

    def attend(tile, hh):
        ts = slice(tile * TILE, (tile + 1) * TILE)
        slot = tile % 2
        rk = slice(hh * GLA_HEAD_K, (hh + 1) * GLA_HEAD_K)
        rv = slice(hh * GLA_HEAD_V, (hh + 1) * GLA_HEAD_V)
        qf = qt_ref[0, rk, ts].astype(F32)
        kf = kt_ref[0, rk, ts].astype(F32)
        vt = vt_ref[0, rv, ts]
        f_in = fact_ref[slot, 0, rk, :]
        q_in = (qf * f_in).astype(BF16)
        k_out = (kf * fact_ref[slot, 1, rk, :]).astype(BF16)
        state = state_ref[hh]
        o = _dot_tn(state.astype(BF16), q_in)
        yield

        far = []
        for idx, m in enumerate(GLA_LEVELS):
            f = fact_ref[slot, 2 + idx, rk, :]
            upper = (lane // m) % 2 == 1
            q_up = jnp.where(upper, qf * f, 0.0).astype(BF16)
            k_lo = jnp.where(upper, 0.0, kf * f).astype(BF16)
            far.append(_dot_tn(k_lo, q_up))
            yield
        band = jnp.tile(band_ref[slot, hh], (HALF // GLA_BLOCK, 1))

        def diagonal(c):
            sl = slice(c * HALF, (c + 1) * HALF)
            blk = far[1][sl, sl]
            for idx in range(2, len(GLA_LEVELS)):
                blk = jnp.where(level == idx, far[idx][sl, sl], blk)
            return jnp.where(level == GLA_BAND, band[:, sl], blk)

        at = jnp.concatenate(
            [jnp.concatenate([diagonal(0), far[0][0:HALF, HALF:]], axis=1),
             jnp.concatenate([jnp.zeros((HALF, HALF), F32), diagonal(1)], axis=1)], axis=0)
        o = o + _dot(vt, at.astype(BF16))
        yield

        ms = jnp.mean(o * o, axis=0, keepdims=True)
        o_ref[0, rv, ts] = (o * lax.rsqrt(ms + EPS) * gn_ref[...]).astype(BF16)

        state_ref[hh] = state * f_in[:, TILE - 1:TILE] + _dot_nt(k_out, vt)
        yield

    def program():
        tiles = qt_ref.shape[2] // TILE
        yield from prepare(0)
        for tile in range(tiles):
            if tile + 1 < tiles:
                yield from prepare(tile + 1)
            for hh in range(heads):
                yield from attend(tile, hh)

    tiles = qt_ref.shape[2] // TILE
    band_steps = GLA_BLOCK.bit_length() - 1 + GLA_BLOCK
    pieces = tiles * (1 + GLA_NFACT + band_steps + heads * (3 + len(GLA_LEVELS)))
    return program(), pieces


def _mixers_kernel(q_ref, k_ref, vtm_ref, gq_ref, gk_ref, gv_ref, glow_ref, wg_ref, bg_ref, seg_ref,
                   level_ref, gn_ref, om_ref, og_ref,
                   s_ref, p_ref, vx_ref, state_ref, fact_ref, band_ref):
    mla, mla_pieces = _mla_stages(q_ref, k_ref, vtm_ref, om_ref, s_ref, p_ref, vx_ref)
    gla, gla_pieces = _gla_stages(gq_ref, gk_ref, gv_ref, glow_ref, wg_ref, bg_ref, seg_ref, level_ref,
                                  gn_ref, og_ref, state_ref, fact_ref, band_ref)
    done = {id(mla): 0, id(gla): 0}
    live = {id(mla): (mla, mla_pieces), id(gla): (gla, gla_pieces)}
    while live:
        key = min(live, key=lambda kk: done[kk] / live[kk][1])
        try:
            next(live[key][0])
            done[key] += 1
        except StopIteration:
            del live[key]


def _mixers(q, k, vtm, gq, gk, gv, glow, wg, bg, seg, level, gn):
    b, heads, s, _ = q.shape
    hp = heads // MIX_SPLIT
    gh = GLA_HEADS // MIX_SPLIT
    gk_rows, gv_rows = gh * GLA_HEAD_K, gh * GLA_HEAD_V

    def full(a):
        return pl.BlockSpec(a.shape, lambda i, j: (0,) * a.ndim)

    def head_rows(n):
        return pl.BlockSpec((1, n, s), lambda i, j: (i, j, 0))

    tok = pl.BlockSpec((1, hp, s, HEAD_LANES), lambda i, j: (i, j, 0, 0))
    trm = pl.BlockSpec((1, hp, MLA_V, s), lambda i, j: (i, j, 0, 0))
    in_specs = [tok, tok, trm, head_rows(gk_rows), head_rows(gk_rows), head_rows(gv_rows),
                pl.BlockSpec((1, GLA_GATE_RANK, s), lambda i, j: (i, 0, 0)),
                pl.BlockSpec((gk_rows, GLA_GATE_RANK), lambda i, j: (j, 0)),
                pl.BlockSpec((gk_rows, 1), lambda i, j: (j, 0)),
                full(seg), full(level), full(gn)]
    return pl.pallas_call(
        _mixers_kernel,
        grid=(b, MIX_SPLIT),
        in_specs=in_specs,
        out_specs=(trm, head_rows(gv_rows)),
        out_shape=(jax.ShapeDtypeStruct((b, heads, MLA_V, s), BF16),
                   jax.ShapeDtypeStruct((b, GLA_DV, s), BF16)),
        scratch_shapes=[pltpu.VMEM((2, s, TILE), F32),
                        pltpu.VMEM((2, s, TILE), BF16),
                        pltpu.VMEM((hp, MLA_VX, s), BF16),
                        pltpu.VMEM((gh, GLA_HEAD_K, GLA_HEAD_V), F32),
                        pltpu.VMEM((2, GLA_NFACT, gk_rows, TILE), F32),
                        pltpu.VMEM((2, gh, GLA_BLOCK, TILE), F32)],
        compiler_params=pltpu.CompilerParams(
            dimension_semantics=("parallel", "parallel"), vmem_limit_bytes=VMEM_LIMIT),
        name="mixers",
    )(q, k, vtm, gq, gk, gv, glow, wg, bg, seg, level, gn)


def _out_kernel(x_ref, p_ref, om_ref, gm_ref, og_ref, gg_ref, ma_ref, mb_ref, wa_ref, wb_ref,
                wout_ref, wple_ref, gple_ref, ggate_ref, wgate_ref, gfin_ref, o_ref, *, final):
    def branch(o_t, g_t_ref, w_ref):
        gh = g_t_ref[0].astype(F32)
        zt = (o_t.astype(F32) * (gh * (jnp.tanh(gh) + 1.0))).astype(BF16)
        return _dot_tn(zt, w_ref[...])

    ya = branch(om_ref[0].reshape(MLA_WIDTH, PROJ_ROWS), gm_ref, wa_ref)
    yb = branch(og_ref[0], gg_ref, wb_ref)
    merged2 = ((jnp.tanh(ma_ref[0].astype(F32)) + 1.0) * ya
               + (jnp.tanh(mb_ref[0].astype(F32)) + 1.0) * yb)
    x1 = x_ref[0] + _dot(merged2.astype(BF16), wout_ref[...])
    e_half = _rms_rows(_dot(p_ref[0].astype(BF16), wple_ref[...]), gple_ref[...])
    gate2 = jnp.tanh(_dot(_rms_rows(x1, ggate_ref[...]).astype(BF16), wgate_ref[...])) + 1.0
    x2 = x1 + gate2 * e_half
    o_ref[0] = _rms_rows(x2, gfin_ref[...]) if final else x2


def _output(x, p, om, gm, og, gg, ma, mb, wa, wb, wout, wple, gple, ggate, wgate, gfin, final):
    b, s, d = x.shape
    rows = PROJ_ROWS

    def full(a):
        return pl.BlockSpec(a.shape, lambda i, j: (0,) * a.ndim)

    def tok(width):
        return pl.BlockSpec((1, rows, width), lambda i, j: (i, j, 0))

    def tr(n):
        return pl.BlockSpec((1, n, rows), lambda i, j: (i, 0, j))

    heads_tr = pl.BlockSpec((1, MLA_HEADS, MLA_V, rows), lambda i, j: (i, 0, 0, j))
    in_specs = [tok(d), tok(p.shape[-1]), heads_tr, tr(MLA_WIDTH), tr(GLA_DV), tr(GLA_DV),
                tok(d), tok(d), full(wa), full(wb), full(wout), full(wple), full(gple), full(ggate),
                full(wgate), full(gfin)]
    return pl.pallas_call(
        functools.partial(_out_kernel, final=final),
        grid=(b, s // rows),
        in_specs=in_specs,
        out_specs=tok(d),
        out_shape=jax.ShapeDtypeStruct((b, s, d), F32),
        compiler_params=pltpu.CompilerParams(
            dimension_semantics=("parallel", "parallel"), vmem_limit_bytes=VMEM_LIMIT),
        name="out",
    )(x, p, om, gm, og, gg, ma, mb, wa, wb, wout, wple, gple, ggate, wgate, gfin)


def _swap_halves(w):
    half = w.shape[-1] // 2
    return jnp.concatenate([-w[..., half:], w[..., :half]], axis=-1)


def _rope_kernel(pos_ref, freq_ref, phase_ref, o_ref):
    pos = pos_ref[0].astype(F32)
    slot = lax.broadcasted_iota(jnp.int32, o_ref.shape[1:], 1) // MLA_ROPE
    posx = pos[:, 0:1]
    for t in range(1, ROPE_PACK):
        posx = jnp.where(slot == t, pos[:, t:t + 1], posx)
    o_ref[0] = jnp.cos(posx * freq_ref[...] - phase_ref[...])


def _rope_table(positions):
    b, s = positions.shape
    inv_freq = 1.0 / (ROPE_THETA ** (jnp.arange(0, MLA_ROPE, 2, dtype=F32) / MLA_ROPE))
    half = MLA_ROPE // 2
    freq = jnp.tile(jnp.concatenate([inv_freq, inv_freq]), ROPE_PACK)[None, :]
    phase = jnp.tile(jnp.concatenate([jnp.zeros((half,), F32), jnp.full((half,), np.pi / 2, F32)]),
                     ROPE_PACK)[None, :]
    rows = s // ROPE_PACK
    out = pl.pallas_call(
        _rope_kernel,
        grid=(b,),
        in_specs=[pl.BlockSpec((1, rows, ROPE_PACK), lambda i: (i, 0, 0)),
                  pl.BlockSpec((1, ROPE_PACK * MLA_ROPE), lambda i: (0, 0)),
                  pl.BlockSpec((1, ROPE_PACK * MLA_ROPE), lambda i: (0, 0))],
        out_specs=pl.BlockSpec((1, rows, ROPE_PACK * MLA_ROPE), lambda i: (i, 0, 0)),
        out_shape=jax.ShapeDtypeStruct((b, rows, ROPE_PACK * MLA_ROPE), F32),
        compiler_params=pltpu.CompilerParams(dimension_semantics=("parallel",)),
        name="rope",
    )(positions.reshape(b, rows, ROPE_PACK), freq, phase)
    return out.reshape(b, s, MLA_ROPE)


def kernel(x, p, positions, norm_in_g, w_in, q_norm_g, w_uq, kv_norm_g, w_ukv, w_gk_up, b_gk,
           gla_norm_g, w_mla_br, w_gla_br, w_out, w_ple, ple_norm_g, ple_gate_norm_g, w_ple_gate,
           final_norm_g):
    b, s, d = x.shape
    depth = w_in.shape[0]
    assert s % PROJ_ROWS == 0 and PROJ_ROWS % TILE == 0
    cs = _rope_table(positions)
    seg_np, level_np = _gla_constants()
    seg = jnp.asarray(seg_np, BF16)
    level = jnp.asarray(level_np)

    sizes = (MLA_Q_RANK, MLA_KV_RANK, MLA_ROPE, MLA_WIDTH, GLA_DK, GLA_DK, GLA_DV, GLA_GATE_RANK,
             GLA_DV, d, d)
    offs = np.concatenate([[0], np.cumsum(sizes)])

    def cols(w, idx):
        return w[:, int(offs[idx]):int(offs[idx + 1])]

    for l in range(depth):
        w = w_in[l]
        wkr = cols(w, 2)
        wstd = jnp.concatenate(
            [cols(w, 0), cols(w, 1), jnp.zeros((d, MLA_NOPE), F32), wkr, _swap_halves(wkr),
             cols(w, 9) * 0.5, cols(w, 10) * 0.5], axis=1).astype(BF16)
        wt = jnp.concatenate(
            [cols(w, 3) * 0.5, cols(w, 4) * (GLA_HEAD_K ** -0.5), cols(w, 5), cols(w, 6),
             cols(w, 8) * 0.5, cols(w, 7)], axis=1).T.astype(BF16)
        uq = w_uq[l].reshape(MLA_Q_RANK, MLA_HEADS, MLA_NOPE + MLA_ROPE)
        uq_r = uq[..., MLA_NOPE:]
        wq = jnp.concatenate([uq[..., :MLA_NOPE], uq_r, _swap_halves(uq_r)], axis=-1)
        wq = wq.reshape(MLA_Q_RANK, MLA_HEADS * HEAD_LANES).astype(BF16)
        ukv = w_ukv[l].reshape(MLA_KV_RANK, MLA_HEADS, MLA_NOPE + MLA_V)
        wk = jnp.concatenate([ukv[..., :MLA_NOPE], jnp.zeros_like(ukv[..., :MLA_NOPE])], axis=-1)
        wk = wk.reshape(MLA_KV_RANK, MLA_HEADS * HEAD_LANES).astype(BF16)
        wvt = ukv[..., MLA_NOPE:].reshape(MLA_KV_RANK, MLA_WIDTH).T.astype(BF16)

        (q, k, vt, gmla, gq, gk, gv, ggla, glow, ma, mb) = _projections(
            x, cs, norm_in_g[l][None, :], wstd, wt, q_norm_g[l][None, :], wq,
            kv_norm_g[l][None, :], wk, wvt)

        o_mla, o_gla = _mixers(q, k, vt, gq, gk, gv, glow, w_gk_up[l].T.astype(BF16), b_gk[l][:, None],
                               seg, level, gla_norm_g[l][:, None])

        x = _output(x, p[l], o_mla, gmla, o_gla, ggla, ma, mb, w_mla_br[l].astype(BF16),
                    w_gla_br[l].astype(BF16), (w_out[l] * 0.5).astype(BF16), w_ple[l].astype(BF16),
                    ple_norm_g[l][None, :] * 0.5, ple_gate_norm_g[l][None, :],
                    (w_ple_gate[l] * 0.5).astype(BF16),
                    final_norm_g[None, :], l == depth - 1)
    return x
```

```python
import functools

import numpy as np
import jax
import jax.numpy as jnp
from jax import lax
from jax.experimental import pallas as pl
from jax.experimental.pallas import tpu as pltpu

F32 = jnp.float32
BF16 = jnp.bfloat16

MLA_HEADS = 8
MLA_Q_RANK = 384
MLA_KV_RANK = 256
MLA_NOPE = 64
MLA_ROPE = 32
MLA_V = 64
MLA_WIDTH = MLA_HEADS * MLA_V
ROPE_THETA = 10000.0
GLA_HEADS = 4
GLA_HEAD_K = 64
GLA_HEAD_V = 128
GLA_DK = GLA_HEADS * GLA_HEAD_K
GLA_DV = GLA_HEADS * GLA_HEAD_V
GLA_GATE_RANK = 16
GLA_GATE_NORMALIZER = 16.0
EPS = 1e-6
LOG2E = 1.4426950408889634

HEAD_LANES = 128
TILE = 256
PROJ_ROWS = 512
OUT_ROWS = 1024
MIX_SPLIT = 2
MLA_VX = MLA_V + 16
ROPE_PACK = 128 // MLA_ROPE
VMEM_LIMIT = 56 * 1024 * 1024

GLA_BLOCK = 16
HALF = TILE // 2
GLA_LEVELS = (HALF, 64, 32, 16)
GLA_NFACT = 2 + len(GLA_LEVELS)
GLA_BAND = len(GLA_LEVELS)

NT = (((1,), (1,)), ((), ()))
TN = (((0,), (0,)), ((), ()))


def _dot(a, b):
    return jnp.dot(a, b, preferred_element_type=F32)


def _dot_nt(a, b):
    return lax.dot_general(a, b, NT, preferred_element_type=F32)


def _dot_tn(a, b):
    return lax.dot_general(a, b, TN, preferred_element_type=F32)


def _rms_rows(x, g):
    return x * lax.rsqrt(jnp.mean(x * x, axis=-1, keepdims=True) + EPS) * g


def _proj_kernel(x_ref, cs_ref, gin_ref, wstd_ref, wt_ref, gq_ref, wq_ref,
                 gkv_ref, wk_ref, wvt_ref,
                 q_out, k_out, vt_out, gmla_out, gq_out, gk_out, gv_out, ggla_out, glow_out,
                 ma_out, mb_out):
    h = _rms_rows(x_ref[0], gin_ref[...]).astype(BF16)

    ht = _dot_nt(wt_ref[...], h)
    for out, lo, hi in ((gmla_out, 0, 512), (gq_out, 512, 768), (gk_out, 768, 1024),
                        (gv_out, 1024, 1536), (ggla_out, 1536, 2048), (glow_out, 2048, 2064)):
        out[0] = ht[lo:hi, :].astype(BF16)
    ma_out[0] = _dot(h, wstd_ref[:, 768:1792]).astype(BF16)
    mb_out[0] = _dot(h, wstd_ref[:, 1792:2816]).astype(BF16)

    cos, sin = cs_ref[0, :, 0:MLA_ROPE // 2], cs_ref[0, :, MLA_ROPE // 2:MLA_ROPE]
    table = jnp.concatenate([jnp.ones((PROJ_ROWS, MLA_NOPE), F32), cos, cos, sin, sin], axis=1)

    cq = _rms_rows(_dot(h, wstd_ref[:, 0:384]), gq_ref[...]).astype(BF16)
    q = _dot(cq, wq_ref[...])
    qscale = table * ((MLA_NOPE + MLA_ROPE) ** -0.5 * LOG2E)
    for hh in range(MLA_HEADS):
        q_out[0, hh] = (q[:, hh * HEAD_LANES:(hh + 1) * HEAD_LANES] * qscale).astype(BF16)

    ckv = _rms_rows(_dot(h, wstd_ref[:, 384:640]), gkv_ref[...]).astype(BF16)
    u = _dot(h, wstd_ref[:, 640:768]) * table
    lane = lax.broadcasted_iota(jnp.int32, u.shape, 1)
    krot = jnp.where(lane >= MLA_NOPE, u + pltpu.roll(u, 32, 1) + pltpu.roll(u, 96, 1), 0.0)
    kk = _dot(ckv, wk_ref[...])
    for hh in range(MLA_HEADS):
        k_out[0, hh] = (kk[:, hh * HEAD_LANES:(hh + 1) * HEAD_LANES] + krot).astype(BF16)

    vt = _dot_nt(wvt_ref[...], ckv)
    for hh in range(MLA_HEADS):
        vt_out[0, hh] = vt[hh * MLA_V:(hh + 1) * MLA_V, :].astype(BF16)


def _projections(x, cs, gin, wstd, wt, gq, wq, gkv, wk, wvt):
    b, s, d = x.shape
    rows = PROJ_ROWS

    def full(a):
        return pl.BlockSpec(a.shape, lambda i, j: (0,) * a.ndim)

    def tok(width):
        return pl.BlockSpec((1, rows, width), lambda i, j: (i, j, 0))

    def tr(n):
        return pl.BlockSpec((1, n, rows), lambda i, j: (i, 0, j))

    def tr_shape(n):
        return jax.ShapeDtypeStruct((b, n, s), BF16)

    heads_tok = pl.BlockSpec((1, MLA_HEADS, rows, HEAD_LANES), lambda i, j: (i, 0, j, 0))
    heads_tr = pl.BlockSpec((1, MLA_HEADS, MLA_V, rows), lambda i, j: (i, 0, 0, j))
    out_shape = (
        jax.ShapeDtypeStruct((b, MLA_HEADS, s, HEAD_LANES), BF16),
        jax.ShapeDtypeStruct((b, MLA_HEADS, s, HEAD_LANES), BF16),
        jax.ShapeDtypeStruct((b, MLA_HEADS, MLA_V, s), BF16),
        tr_shape(MLA_WIDTH),
        tr_shape(GLA_DK),
        tr_shape(GLA_DK),
        tr_shape(GLA_DV),
        tr_shape(GLA_DV),
        tr_shape(GLA_GATE_RANK),
        jax.ShapeDtypeStruct((b, s, d), BF16),
        jax.ShapeDtypeStruct((b, s, d), BF16),
    )
    out_specs = (heads_tok, heads_tok, heads_tr, tr(MLA_WIDTH),
                 tr(GLA_DK), tr(GLA_DK), tr(GLA_DV), tr(GLA_DV), tr(GLA_GATE_RANK), tok(d), tok(d))
    in_specs = [tok(d), tok(MLA_ROPE), full(gin), full(wstd), full(wt), full(gq),
                full(wq), full(gkv), full(wk), full(wvt)]
    return pl.pallas_call(
        _proj_kernel,
        grid=(b, s // rows),
        in_specs=in_specs,
        out_specs=out_specs,
        out_shape=out_shape,
        compiler_params=pltpu.CompilerParams(
            dimension_semantics=("parallel", "parallel"), vmem_limit_bytes=VMEM_LIMIT),
        name="proj",
    )(x, cs, gin, wstd, wt, gq, wq, gkv, wk, wvt)


def _mla_stages(q_ref, k_ref, vt_ref, o_ref, s_ref, p_ref, vx_ref):
    heads, seq = q_ref.shape[1], q_ref.shape[2]
    row = lax.broadcasted_iota(jnp.int32, (TILE, TILE), 0)
    col = lax.broadcasted_iota(jnp.int32, (TILE, TILE), 1)
    causal = row <= col
    tasks = [(hh, qi) for hh in range(heads) for qi in range(seq // TILE)]

    for hh in range(heads):
        vx_ref[hh, 0:MLA_V, :] = vt_ref[0, hh]
        vx_ref[hh, MLA_V:, :] = jnp.ones((MLA_VX - MLA_V, seq), BF16)

    def scores(t):
        hh, qi = tasks[t]
        lo, hi = qi * TILE, (qi + 1) * TILE
        qt = q_ref[0, hh, lo:hi, :]
        if qi:
            s_ref[t % 2, 0:lo, :] = _dot_nt(k_ref[0, hh, 0:lo, :], qt)
        s_ref[t % 2, lo:hi, :] = jnp.where(causal, _dot_nt(k_ref[0, hh, lo:hi, :], qt), -jnp.inf)

    def softmax(t):
        _, qi = tasks[t]
        m = jnp.full((1, TILE), -jnp.inf, F32)
        for kj in range(qi + 1):
            m = jnp.maximum(m, jnp.max(s_ref[t % 2, kj * TILE:(kj + 1) * TILE, :], axis=0, keepdims=True))
            yield
        for kj in range(qi + 1):
            blk = slice(kj * TILE, (kj + 1) * TILE)
            p_ref[t % 2, blk, :] = jnp.exp2(s_ref[t % 2, blk, :] - m).astype(BF16)
            yield

    def weighted_values(t):
        hh, qi = tasks[t]
        lo, hi = qi * TILE, (qi + 1) * TILE
        acc = _dot(vx_ref[hh, :, 0:hi], p_ref[t % 2, 0:hi, :])
        o_ref[0, hh, :, lo:hi] = (acc[0:MLA_V] / acc[MLA_V:MLA_V + 1]).astype(BF16)

    def program():
        scores(0)
        yield
        for t in range(len(tasks)):
            if t + 1 < len(tasks):
                scores(t + 1)
                yield
            if t:
                weighted_values(t - 1)
                yield
            yield from softmax(t)
        weighted_values(len(tasks) - 1)

    pieces = 2 + sum(2 + 2 * (qi + 1) for _, qi in tasks) - 2
    return program(), pieces


def _gla_constants():
    u = np.arange(TILE)[:, None]
    t = np.arange(TILE)[None, :]
    mats = [(u <= t), (u > t)]
    for m in GLA_LEVELS:
        pos = t % (2 * m)
        mid = t - pos + m - 1
        upper = pos >= m
        mats.append(np.where(upper, (u > mid) & (u <= t), (u > t) & (u <= mid)))
    seg = np.concatenate([mm.astype(np.float32) for mm in mats], axis=1)
    seg = np.concatenate([seg, seg, seg], axis=0)
    j = np.arange(TILE)[:, None]
    i = np.arange(TILE)[None, :]
    level = np.full((TILE, TILE), -1, np.int32)
    level[(j // GLA_BLOCK) == (i // GLA_BLOCK)] = GLA_BAND
    for idx, m in enumerate(GLA_LEVELS):
        same = (j // (2 * m)) == (i // (2 * m))
        split = ((j % (2 * m)) < m) & ((i % (2 * m)) >= m)
        level[same & split & (j < i)] = idx
    return seg, level[:HALF, :HALF]


def _gla_band(qf, kf, decay, band_ref):
    heads = qf.shape[0] // GLA_HEAD_K
    rel = (lax.broadcasted_iota(jnp.int32, (GLA_BLOCK, TILE), 1) % GLA_BLOCK
           - lax.broadcasted_iota(jnp.int32, (GLA_BLOCK, TILE), 0))
    w = [kf]
    span = decay
    step = 1
    while step < GLA_BLOCK:
        w += [span * pltpu.roll(w[d], step, 1) for d in range(step)]
        span = span * pltpu.roll(span, step, 1)
        step *= 2
        yield
    bands = [jnp.zeros((GLA_BLOCK, TILE), F32) for _ in range(heads)]
    for d in range(GLA_BLOCK):
        prod = qf * w[d]
        for hh in range(heads):
            diag = jnp.sum(prod[hh * GLA_HEAD_K:(hh + 1) * GLA_HEAD_K], axis=0, keepdims=True)
            bands[hh] = jnp.where(rel == d, diag, bands[hh])
        yield
    for hh in range(heads):
        band_ref[hh] = bands[hh]


def _gla_stages(qt_ref, kt_ref, vt_ref, glow_ref, wg_ref, bg_ref, seg_ref, level_ref, gn_ref,
                o_ref, state_ref, fact_ref, band_ref):
    heads = qt_ref.shape[1] // GLA_HEAD_K
    state_ref[...] = jnp.zeros_like(state_ref)
    level = level_ref[...]
    lane = lax.broadcasted_iota(jnp.int32, (GLA_HEAD_K, TILE), 1)

    def prepare(tile):
        ts = slice(tile * TILE, (tile + 1) * TILE)
        slot = tile % 2
        z = _dot(wg_ref[...], glow_ref[0, :, ts]) + bg_ref[...]
        g = -(jnp.maximum(-z, 0.0) + jnp.log1p(jnp.exp(-jnp.abs(z)))) * (1.0 / GLA_GATE_NORMALIZER)
        g_hi = g.astype(BF16)
        r = g - g_hi.astype(F32)
        g_mid = r.astype(BF16)
        g_lo = (r - g_mid.astype(F32)).astype(BF16)
        g3 = jnp.concatenate([g_hi, g_mid, g_lo], axis=1)
        yield
        for n in range(GLA_NFACT):
            fact_ref[slot, n] = jnp.exp(_dot(g3, seg_ref[:, n * TILE:(n + 1) * TILE]))
            yield
        yield from _gla_band(qt_ref[0, :, ts].astype(F32), kt_ref[0, :, ts].astype(F32), jnp.exp(g),
                             band_ref.at[slot])

    def attend(tile, hh):
        ts = slice(tile * TILE, (tile + 1) * TILE)
        slot = tile % 2
        rk = slice(hh * GLA_HEAD_K, (hh + 1) * GLA_HEAD_K)
        rv = slice(hh * GLA_HEAD_V, (hh + 1) * GLA_HEAD_V)
        qf = qt_ref[0, rk, ts].astype(F32)
        kf = kt_ref[0, rk, ts].astype(F32)
        vt = vt_ref[0, rv, ts]
        f_in = fact_ref[slot, 0, rk, :]
        q_in = (qf * f_in).astype(BF16)
        k_out = (kf * fact_ref[slot, 1, rk, :]).astype(BF16)
        state = state_ref[hh]
        o = _dot_tn(state.astype(BF16), q_in)
        yield

        far = []
        for idx, m in enumerate(GLA_LEVELS):
            f = fact_ref[slot, 2 + idx, rk, :]
            upper = (lane // m) % 2 == 1
            q_up = jnp.where(upper, qf * f, 0.0).astype(BF16)
            k_lo = jnp.where(upper, 0.0, kf * f).astype(BF16)
            far.append(_dot_tn(k_lo, q_up))
            yield
        band = jnp.tile(band_ref[slot, hh], (HALF // GLA_BLOCK, 1))

        def diagonal(c):
            sl = slice(c * HALF, (c + 1) * HALF)
            blk = far[1][sl, sl]
            for idx in range(2, len(GLA_LEVELS)):
                blk = jnp.where(level == idx, far[idx][sl, sl], blk)
            return jnp.where(level == GLA_BAND, band[:, sl], blk)

        at = jnp.concatenate(
            [jnp.concatenate([diagonal(0), far[0][0:HALF, HALF:]], axis=1),
             jnp.concatenate([jnp.zeros((HALF, HALF), F32), diagonal(1)], axis=1)], axis=0)
        o = o + _dot(vt, at.astype(BF16))
        yield

        ms = jnp.mean(o * o, axis=0, keepdims=True)
        o_ref[0, rv, ts] = (o * lax.rsqrt(ms + EPS) * gn_ref[...]).astype(BF16)

        state_ref[hh] = state * f_in[:, TILE - 1:TILE] + _dot_nt(k_out, vt)
        yield

    def program():
        tiles = qt_ref.shape[2] // TILE
        yield from prepare(0)
        for tile in range(tiles):
            if tile + 1 < tiles:
                yield from prepare(tile + 1)
            for hh in range(heads):
                yield from attend(tile, hh)

    tiles = qt_ref.shape[2] // TILE
    band_steps = GLA_BLOCK.bit_length() - 1 + GLA_BLOCK
    pieces = tiles * (1 + GLA_NFACT + band_steps + heads * (3 + len(GLA_LEVELS)))
    return program(), pieces


def _mixers_kernel(q_ref, k_ref, vtm_ref, gq_ref, gk_ref, gv_ref, glow_ref, wg_ref, bg_ref, seg_ref,
                   level_ref, gn_ref, om_ref, og_ref,
                   s_ref, p_ref, vx_ref, state_ref, fact_ref, band_ref):
    mla, mla_pieces = _mla_stages(q_ref, k_ref, vtm_ref, om_ref, s_ref, p_ref, vx_ref)
    gla, gla_pieces = _gla_stages(gq_ref, gk_ref, gv_ref, glow_ref, wg_ref, bg_ref, seg_ref, level_ref,
                                  gn_ref, og_ref, state_ref, fact_ref, band_ref)
    done = {id(mla): 0, id(gla): 0}
    live = {id(mla): (mla, mla_pieces), id(gla): (gla, gla_pieces)}
    while live:
        key = min(live, key=lambda kk: done[kk] / live[kk][1])
        try:
            next(live[key][0])
            done[key] += 1
        except StopIteration:
            del live[key]


def _mixers(q, k, vtm, gq, gk, gv, glow, wg, bg, seg, level, gn):
    b, heads, s, _ = q.shape
    hp = heads // MIX_SPLIT
    gh = GLA_HEADS // MIX_SPLIT
    gk_rows, gv_rows = gh * GLA_HEAD_K, gh * GLA_HEAD_V

    def full(a):
        return pl.BlockSpec(a.shape, lambda i, j: (0,) * a.ndim)

    def head_rows(n):
        return pl.BlockSpec((1, n, s), lambda i, j: (i, j, 0))

    tok = pl.BlockSpec((1, hp, s, HEAD_LANES), lambda i, j: (i, j, 0, 0))
    trm = pl.BlockSpec((1, hp, MLA_V, s), lambda i, j: (i, j, 0, 0))
    in_specs = [tok, tok, trm, head_rows(gk_rows), head_rows(gk_rows), head_rows(gv_rows),
                pl.BlockSpec((1, GLA_GATE_RANK, s), lambda i, j: (i, 0, 0)),
                pl.BlockSpec((gk_rows, GLA_GATE_RANK), lambda i, j: (j, 0)),
                pl.BlockSpec((gk_rows, 1), lambda i, j: (j, 0)),
                full(seg), full(level), full(gn)]
    return pl.pallas_call(
        _mixers_kernel,
        grid=(b, MIX_SPLIT),
        in_specs=in_specs,
        out_specs=(trm, head_rows(gv_rows)),
        out_shape=(jax.ShapeDtypeStruct((b, heads, MLA_V, s), BF16),
                   jax.ShapeDtypeStruct((b, GLA_DV, s), BF16)),
        scratch_shapes=[pltpu.VMEM((2, s, TILE), F32),
                        pltpu.VMEM((2, s, TILE), BF16),
                        pltpu.VMEM((hp, MLA_VX, s), BF16),
                        pltpu.VMEM((gh, GLA_HEAD_K, GLA_HEAD_V), F32),
                        pltpu.VMEM((2, GLA_NFACT, gk_rows, TILE), F32),
                        pltpu.VMEM((2, gh, GLA_BLOCK, TILE), F32)],
        compiler_params=pltpu.CompilerParams(
            dimension_semantics=("parallel", "parallel"), vmem_limit_bytes=VMEM_LIMIT),
        name="mixers",
    )(q, k, vtm, gq, gk, gv, glow, wg, bg, seg, level, gn)


def _out_kernel(x_ref, p_ref, om_ref, gm_ref, og_ref, gg_ref, ma_ref, mb_ref, wa_ref, wb_ref,
                wout_ref, wple_ref, gple_ref, ggate_ref, wgate_ref, gfin_ref, o_ref, *, final):
    def branch(o_t, g_t_ref, w_ref):
        gh = g_t_ref[0]
        zt = o_t * (gh * (jnp.tanh(gh) + 1.0))
        return _dot_tn(zt, w_ref[...])

    ya = branch(om_ref[0].reshape(MLA_WIDTH, OUT_ROWS), gm_ref, wa_ref)
    yb = branch(og_ref[0], gg_ref, wb_ref)
    merged2 = ((jnp.tanh(ma_ref[0]) + 1.0) * ya.astype(BF16)
               + (jnp.tanh(mb_ref[0]) + 1.0) * yb.astype(BF16))
    x1 = x_ref[0] + _dot(merged2, wout_ref[...])
    e_half = _rms_rows(_dot(p_ref[0].astype(BF16), wple_ref[...]), gple_ref[...])
    gate2 = jnp.tanh(_dot(_rms_rows(x1, ggate_ref[...]).astype(BF16), wgate_ref[...])) + 1.0
    x2 = x1 + gate2 * e_half
    o_ref[0] = _rms_rows(x2, gfin_ref[...]) if final else x2


def _output(x, p, om, gm, og, gg, ma, mb, wa, wb, wout, wple, gple, ggate, wgate, gfin, final):
    b, s, d = x.shape
    rows = OUT_ROWS

    def full(a):
        return pl.BlockSpec(a.shape, lambda i, j: (0,) * a.ndim)

    def tok(width):
        return pl.BlockSpec((1, rows, width), lambda i, j: (i, j, 0))

    def tr(n):
        return pl.BlockSpec((1, n, rows), lambda i, j: (i, 0, j))

    heads_tr = pl.BlockSpec((1, MLA_HEADS, MLA_V, rows), lambda i, j: (i, 0, 0, j))
    in_specs = [tok(d), tok(p.shape[-1]), heads_tr, tr(MLA_WIDTH), tr(GLA_DV), tr(GLA_DV),
                tok(d), tok(d), full(wa), full(wb), full(wout), full(wple), full(gple), full(ggate),
                full(wgate), full(gfin)]
    return pl.pallas_call(
        functools.partial(_out_kernel, final=final),
        grid=(b, s // rows),
        in_specs=in_specs,
        out_specs=tok(d),
        out_shape=jax.ShapeDtypeStruct((b, s, d), F32),
        compiler_params=pltpu.CompilerParams(
            dimension_semantics=("parallel", "parallel"), vmem_limit_bytes=VMEM_LIMIT),
        name="out",
    )(x, p, om, gm, og, gg, ma, mb, wa, wb, wout, wple, gple, ggate, wgate, gfin)


def _swap_halves(w):
    half = w.shape[-1] // 2
    return jnp.concatenate([-w[..., half:], w[..., :half]], axis=-1)


def _rope_kernel(pos_ref, freq_ref, phase_ref, o_ref):
    pos = pos_ref[0].astype(F32)
    slot = lax.broadcasted_iota(jnp.int32, o_ref.shape[1:], 1) // MLA_ROPE
    posx = pos[:, 0:1]
    for t in range(1, ROPE_PACK):
        posx = jnp.where(slot == t, pos[:, t:t + 1], posx)
    o_ref[0] = jnp.cos(posx * freq_ref[...] - phase_ref[...])


def _rope_table(positions):
    b, s = positions.shape
    inv_freq = 1.0 / (ROPE_THETA ** (jnp.arange(0, MLA_ROPE, 2, dtype=F32) / MLA_ROPE))
    half = MLA_ROPE // 2
    freq = jnp.tile(jnp.concatenate([inv_freq, inv_freq]), ROPE_PACK)[None, :]
    phase = jnp.tile(jnp.concatenate([jnp.zeros((half,), F32), jnp.full((half,), np.pi / 2, F32)]),
                     ROPE_PACK)[None, :]
    rows = s // ROPE_PACK
    out = pl.pallas_call(
        _rope_kernel,
        grid=(b,),
        in_specs=[pl.BlockSpec((1, rows, ROPE_PACK), lambda i: (i, 0, 0)),
                  pl.BlockSpec((1, ROPE_PACK * MLA_ROPE), lambda i: (0, 0)),
                  pl.BlockSpec((1, ROPE_PACK * MLA_ROPE), lambda i: (0, 0))],
        out_specs=pl.BlockSpec((1, rows, ROPE_PACK * MLA_ROPE), lambda i: (i, 0, 0)),
        out_shape=jax.ShapeDtypeStruct((b, rows, ROPE_PACK * MLA_ROPE), F32),
        compiler_params=pltpu.CompilerParams(dimension_semantics=("parallel",)),
        name="rope",
    )(positions.reshape(b, rows, ROPE_PACK), freq, phase)
    return out.reshape(b, s, MLA_ROPE)


def kernel(x, p, positions, norm_in_g, w_in, q_norm_g, w_uq, kv_norm_g, w_ukv, w_gk_up, b_gk,
           gla_norm_g, w_mla_br, w_gla_br, w_out, w_ple, ple_norm_g, ple_gate_norm_g, w_ple_gate,
           final_norm_g):
    b, s, d = x.shape
    depth = w_in.shape[0]
    assert s % PROJ_ROWS == 0 and s % OUT_ROWS == 0 and s % TILE == 0
    cs = _rope_table(positions)
    seg_np, level_np = _gla_constants()
    seg = jnp.asarray(seg_np, BF16)
    level = jnp.asarray(level_np)

    sizes = (MLA_Q_RANK, MLA_KV_RANK, MLA_ROPE, MLA_WIDTH, GLA_DK, GLA_DK, GLA_DV, GLA_GATE_RANK,
             GLA_DV, d, d)
    offs = np.concatenate([[0], np.cumsum(sizes)])

    def cols(w, idx):
        return w[:, int(offs[idx]):int(offs[idx + 1])]

    for l in range(depth):
        w = w_in[l]
        wkr = cols(w, 2)
        wstd = jnp.concatenate(
            [cols(w, 0), cols(w, 1), jnp.zeros((d, MLA_NOPE), F32), wkr, _swap_halves(wkr),
             cols(w, 9) * 0.5, cols(w, 10) * 0.5], axis=1).astype(BF16)
        wt = jnp.concatenate(
            [cols(w, 3) * 0.5, cols(w, 4) * (GLA_HEAD_K ** -0.5), cols(w, 5), cols(w, 6),
             cols(w, 8) * 0.5, cols(w, 7)], axis=1).T.astype(BF16)
        uq = w_uq[l].reshape(MLA_Q_RANK, MLA_HEADS, MLA_NOPE + MLA_ROPE)
        uq_r = uq[..., MLA_NOPE:]
        wq = jnp.concatenate([uq[..., :MLA_NOPE], uq_r, _swap_halves(uq_r)], axis=-1)
        wq = wq.reshape(MLA_Q_RANK, MLA_HEADS * HEAD_LANES).astype(BF16)
        ukv = w_ukv[l].reshape(MLA_KV_RANK, MLA_HEADS, MLA_NOPE + MLA_V)
        wk = jnp.concatenate([ukv[..., :MLA_NOPE], jnp.zeros_like(ukv[..., :MLA_NOPE])], axis=-1)
        wk = wk.reshape(MLA_KV_RANK, MLA_HEADS * HEAD_LANES).astype(BF16)
        wvt = ukv[..., MLA_NOPE:].reshape(MLA_KV_RANK, MLA_WIDTH).T.astype(BF16)

        (q, k, vt, gmla, gq, gk, gv, ggla, glow, ma, mb) = _projections(
            x, cs, norm_in_g[l][None, :], wstd, wt, q_norm_g[l][None, :], wq,
            kv_norm_g[l][None, :], wk, wvt)

        o_mla, o_gla = _mixers(q, k, vt, gq, gk, gv, glow, w_gk_up[l].T.astype(BF16), b_gk[l][:, None],
                               seg, level, gla_norm_g[l][:, None])

        x = _output(x, p[l], o_mla, gmla, o_gla, ggla, ma, mb, w_mla_br[l].astype(BF16),
                    w_gla_br[l].astype(BF16), (w_out[l] * 0.5).astype(BF16), w_ple[l].astype(BF16),
                    ple_norm_g[l][None, :] * 0.5, ple_gate_norm_g[l][None, :],
                    (w_ple_gate[l] * 0.5).astype(BF16),
                    final_norm_g[None, :], l == depth - 1)
    return x
```

```python
import functools

import numpy as np
import jax
import jax.numpy as jnp
from jax import lax
from jax.experimental import pallas as pl
from jax.experimental.pallas import tpu as pltpu

F32 = jnp.float32
BF16 = jnp.bfloat16

MLA_HEADS = 8
MLA_Q_RANK = 384
MLA_KV_RANK = 256
MLA_NOPE = 64
MLA_ROPE = 32
MLA_V = 64
MLA_WIDTH = MLA_HEADS * MLA_V
ROPE_THETA = 10000.0
GLA_HEADS = 4
GLA_HEAD_K = 64
GLA_HEAD_V = 128
GLA_DK = GLA_HEADS * GLA_HEAD_K
GLA_DV = GLA_HEADS * GLA_HEAD_V
GLA_GATE_RANK = 16
GLA_GATE_NORMALIZER = 16.0
EPS = 1e-6
LOG2E = 1.4426950408889634

HEAD_LANES = 128
TILE = 256
PROJ_ROWS = 512
OUT_ROWS = 1024
MIX_SPLIT = 2
MLA_VX = MLA_V + 16
ROPE_PACK = 128 // MLA_ROPE
VMEM_LIMIT = 56 * 1024 * 1024

GLA_BLOCK = 16
GLA_PARTS = 2
HALF = TILE // 2
GLA_LEVELS = (HALF, 64, 32, 16)
GLA_NFACT = 2 + len(GLA_LEVELS)
GLA_BAND = len(GLA_LEVELS)

NT = (((1,), (1,)), ((), ()))
TN = (((0,), (0,)), ((), ()))


def _dot(a, b):
    return jnp.dot(a, b, preferred_element_type=F32)


def _dot_nt(a, b):
    return lax.dot_general(a, b, NT, preferred_element_type=F32)


def _dot_tn(a, b):
    return lax.dot_general(a, b, TN, preferred_element_type=F32)


def _rms_rows(x, g):
    return x * lax.rsqrt(jnp.mean(x * x, axis=-1, keepdims=True) + EPS) * g


def _proj_kernel(x_ref, cs_ref, gin_ref, wstd_ref, wt_ref, gq_ref, wq_ref,
                 gkv_ref, wk_ref, wvt_ref,
                 q_out, k_out, vt_out, gmla_out, gq_out, gk_out, gv_out, ggla_out, glow_out,
                 ma_out, mb_out):
    h = _rms_rows(x_ref[0], gin_ref[...]).astype(BF16)

    ht = _dot_nt(wt_ref[...], h)
    for out, lo, hi in ((gmla_out, 0, 512), (gq_out, 512, 768), (gk_out, 768, 1024),
                        (gv_out, 1024, 1536), (ggla_out, 1536, 2048), (glow_out, 2048, 2064)):
        out[0] = ht[lo:hi, :].astype(BF16)
    ma_out[0] = _dot(h, wstd_ref[:, 768:1792]).astype(BF16)
    mb_out[0] = _dot(h, wstd_ref[:, 1792:2816]).astype(BF16)

    cos, sin = cs_ref[0, :, 0:MLA_ROPE // 2], cs_ref[0, :, MLA_ROPE // 2:MLA_ROPE]
    table = jnp.concatenate([jnp.ones((PROJ_ROWS, MLA_NOPE), F32), cos, cos, sin, sin], axis=1)

    cq = _rms_rows(_dot(h, wstd_ref[:, 0:384]), gq_ref[...]).astype(BF16)
    q = _dot(cq, wq_ref[...])
    qscale = table * ((MLA_NOPE + MLA_ROPE) ** -0.5 * LOG2E)
    for hh in range(MLA_HEADS):
        q_out[0, hh] = (q[:, hh * HEAD_LANES:(hh + 1) * HEAD_LANES] * qscale).astype(BF16)

    ckv = _rms_rows(_dot(h, wstd_ref[:, 384:640]), gkv_ref[...]).astype(BF16)
    u = _dot(h, wstd_ref[:, 640:768]) * table
    lane = lax.broadcasted_iota(jnp.int32, u.shape, 1)
    krot = jnp.where(lane >= MLA_NOPE, u + pltpu.roll(u, 32, 1) + pltpu.roll(u, 96, 1), 0.0)
    kk = _dot(ckv, wk_ref[...])
    for hh in range(MLA_HEADS):
        k_out[0, hh] = (kk[:, hh * HEAD_LANES:(hh + 1) * HEAD_LANES] + krot).astype(BF16)

    vt = _dot_nt(wvt_ref[...], ckv)
    for hh in range(MLA_HEADS):
        vt_out[0, hh] = vt[hh * MLA_V:(hh + 1) * MLA_V, :].astype(BF16)


def _projections(x, cs, gin, wstd, wt, gq, wq, gkv, wk, wvt):
    b, s, d = x.shape
    rows = PROJ_ROWS

    def full(a):
        return pl.BlockSpec(a.shape, lambda i, j: (0,) * a.ndim)

    def tok(width):
        return pl.BlockSpec((1, rows, width), lambda i, j: (i, j, 0))

    def tr(n):
        return pl.BlockSpec((1, n, rows), lambda i, j: (i, 0, j))

    def tr_shape(n):
        return jax.ShapeDtypeStruct((b, n, s), BF16)

    heads_tok = pl.BlockSpec((1, MLA_HEADS, rows, HEAD_LANES), lambda i, j: (i, 0, j, 0))
    heads_tr = pl.BlockSpec((1, MLA_HEADS, MLA_V, rows), lambda i, j: (i, 0, 0, j))
    out_shape = (
        jax.ShapeDtypeStruct((b, MLA_HEADS, s, HEAD_LANES), BF16),
        jax.ShapeDtypeStruct((b, MLA_HEADS, s, HEAD_LANES), BF16),
        jax.ShapeDtypeStruct((b, MLA_HEADS, MLA_V, s), BF16),
        tr_shape(MLA_WIDTH),
        tr_shape(GLA_DK),
        tr_shape(GLA_DK),
        tr_shape(GLA_DV),
        tr_shape(GLA_DV),
        tr_shape(GLA_GATE_RANK),
        jax.ShapeDtypeStruct((b, s, d), BF16),
        jax.ShapeDtypeStruct((b, s, d), BF16),
    )
    out_specs = (heads_tok, heads_tok, heads_tr, tr(MLA_WIDTH),
                 tr(GLA_DK), tr(GLA_DK), tr(GLA_DV), tr(GLA_DV), tr(GLA_GATE_RANK), tok(d), tok(d))
    in_specs = [tok(d), tok(MLA_ROPE), full(gin), full(wstd), full(wt), full(gq),
                full(wq), full(gkv), full(wk), full(wvt)]
    return pl.pallas_call(
        _proj_kernel,
        grid=(b, s // rows),
        in_specs=in_specs,
        out_specs=out_specs,
        out_shape=out_shape,
        compiler_params=pltpu.CompilerParams(
            dimension_semantics=("parallel", "parallel"), vmem_limit_bytes=VMEM_LIMIT),
        name="proj",
    )(x, cs, gin, wstd, wt, gq, wq, gkv, wk, wvt)


def _mla_stages(q_ref, k_ref, vt_ref, o_ref, s_ref, p_ref, vx_ref):
    heads, seq = q_ref.shape[1], q_ref.shape[2]
    row = lax.broadcasted_iota(jnp.int32, (TILE, TILE), 0)
    col = lax.broadcasted_iota(jnp.int32, (TILE, TILE), 1)
    causal = row <= col
    tasks = [(hh, qi) for hh in range(heads) for qi in range(seq // TILE)]

    for hh in range(heads):
        vx_ref[hh, 0:MLA_V, :] = vt_ref[0, hh]
        vx_ref[hh, MLA_V:, :] = jnp.ones((MLA_VX - MLA_V, seq), BF16)

    def scores(t):
        hh, qi = tasks[t]
        lo, hi = qi * TILE, (qi + 1) * TILE
        qt = q_ref[0, hh, lo:hi, :]
        if qi:
            s_ref[t % 2, 0:lo, :] = _dot_nt(k_ref[0, hh, 0:lo, :], qt)
        s_ref[t % 2, lo:hi, :] = jnp.where(causal, _dot_nt(k_ref[0, hh, lo:hi, :], qt), -jnp.inf)

    def softmax(t):
        _, qi = tasks[t]
        m = jnp.full((1, TILE), -jnp.inf, F32)
        for kj in range(qi + 1):
            yield 0, 18
            m = jnp.maximum(m, jnp.max(s_ref[t % 2, kj * TILE:(kj + 1) * TILE, :], axis=0, keepdims=True))
        for kj in range(qi + 1):
            yield 0, 45
            blk = slice(kj * TILE, (kj + 1) * TILE)
            p_ref[t % 2, blk, :] = jnp.exp2(s_ref[t % 2, blk, :] - m).astype(BF16)

    def weighted_values(t):
        hh, qi = tasks[t]
        lo, hi = qi * TILE, (qi + 1) * TILE
        acc = _dot(vx_ref[hh, :, 0:hi], p_ref[t % 2, 0:hi, :])
        o_ref[0, hh, :, lo:hi] = (acc[0:MLA_V] / acc[MLA_V:MLA_V + 1]).astype(BF16)

    def program():
        blocks = [qi + 1 for _, qi in tasks]
        yield 64 * blocks[0], 20 * blocks[0]
        scores(0)
        for t in range(len(tasks)):
            if t + 1 < len(tasks):
                yield 64 * blocks[t + 1], 20 * blocks[t + 1]
                scores(t + 1)
            if t:
                yield 32 * blocks[t - 1], 10
                weighted_values(t - 1)
            yield from softmax(t)
        yield 32 * blocks[-1], 10
        weighted_values(len(tasks) - 1)

    blocks = sum(qi + 1 for _, qi in tasks)
    return program(), (96.0 * blocks, 83.0 * blocks + 10.0 * len(tasks))


def _gla_constants():
    u = np.arange(TILE)[:, None]
    t = np.arange(TILE)[None, :]
    mats = [(u <= t), (u > t)]
    for m in GLA_LEVELS:
        pos = t % (2 * m)
        mid = t - pos + m - 1
        upper = pos >= m
        mats.append(np.where(upper, (u > mid) & (u <= t), (u > t) & (u <= mid)))
    seg = np.concatenate([mm.astype(np.float32) for mm in mats], axis=1)
    seg = np.concatenate([seg] * GLA_PARTS, axis=0)
    j = np.arange(TILE)[:, None]
    i = np.arange(TILE)[None, :]
    level = np.full((TILE, TILE), -1, np.int32)
    level[(j // GLA_BLOCK) == (i // GLA_BLOCK)] = GLA_BAND
    for idx, m in enumerate(GLA_LEVELS):
        same = (j // (2 * m)) == (i // (2 * m))
        split = ((j % (2 * m)) < m) & ((i % (2 * m)) >= m)
        level[same & split & (j < i)] = idx
    return seg, level[:HALF, :HALF]


def _gla_band(qf, kf, decay, band_ref):
    heads = qf.shape[0] // GLA_HEAD_K
    rel = (lax.broadcasted_iota(jnp.int32, (GLA_BLOCK, TILE), 1) % GLA_BLOCK
           - lax.broadcasted_iota(jnp.int32, (GLA_BLOCK, TILE), 0))
    w = [kf]
    span = decay
    step = 1
    while step < GLA_BLOCK:
        yield 0, 4 * heads * (step + 1)
        w += [span * pltpu.roll(w[d], step, 1) for d in range(step)]
        span = span * pltpu.roll(span, step, 1)
        step *= 2
    bands = [jnp.zeros((GLA_BLOCK, TILE), F32) for _ in range(heads)]
    for d in range(GLA_BLOCK):
        yield 0, 12 * heads
        prod = qf * w[d]
        for hh in range(heads):
            diag = jnp.sum(prod[hh * GLA_HEAD_K:(hh + 1) * GLA_HEAD_K], axis=0, keepdims=True)
            bands[hh] = jnp.where(rel == d, diag, bands[hh])
    for hh in range(heads):
        band_ref[hh] = bands[hh]


def _gla_stages(qt_ref, kt_ref, vt_ref, glow_ref, wg_ref, bg_ref, seg_ref, level_ref, gn_ref,
                o_ref, state_ref, fact_ref, band_ref):
    heads = qt_ref.shape[1] // GLA_HEAD_K
    state_ref[...] = jnp.zeros_like(state_ref)
    level = level_ref[...]
    lane = lax.broadcasted_iota(jnp.int32, (GLA_HEAD_K, TILE), 1)

    def prepare(tile):
        ts = slice(tile * TILE, (tile + 1) * TILE)
        slot = tile % 2
        yield 8, 50 * heads
        z = _dot(wg_ref[...], glow_ref[0, :, ts]) + bg_ref[...]
        g = -(jnp.maximum(-z, 0.0) + jnp.log1p(jnp.exp(-jnp.abs(z)))) * (1.0 / GLA_GATE_NORMALIZER)
        parts, rest = [], g
        for _ in range(GLA_PARTS):
            parts.append(rest.astype(BF16))
            rest = rest - parts[-1].astype(F32)
        g3 = jnp.concatenate(parts, axis=1)
        for n in range(GLA_NFACT):
            yield 8 * GLA_PARTS * heads, 6 * heads
            fact_ref[slot, n] = jnp.exp(_dot(g3, seg_ref[:, n * TILE:(n + 1) * TILE]))
        yield from _gla_band(qt_ref[0, :, ts].astype(F32), kt_ref[0, :, ts].astype(F32), jnp.exp(g),
                             band_ref.at[slot])

    def attend(tile, hh):
        ts = slice(tile * TILE, (tile + 1) * TILE)
        slot = tile % 2
        rk = slice(hh * GLA_HEAD_K, (hh + 1) * GLA_HEAD_K)
        rv = slice(hh * GLA_HEAD_V, (hh + 1) * GLA_HEAD_V)
        yield 20, 25
        qf = qt_ref[0, rk, ts].astype(F32)
        kf = kt_ref[0, rk, ts].astype(F32)
        vt = vt_ref[0, rv, ts]
        f_in = fact_ref[slot, 0, rk, :]
        q_in = (qf * f_in).astype(BF16)
        k_out = (kf * fact_ref[slot, 1, rk, :]).astype(BF16)
        state = state_ref[hh]
        o = _dot_tn(state.astype(BF16), q_in)

        far = []
        for idx, m in enumerate(GLA_LEVELS):
            yield 64, 20
            f = fact_ref[slot, 2 + idx, rk, :]
            upper = (lane // m) % 2 == 1
            q_up = jnp.where(upper, qf * f, 0.0).astype(BF16)
            k_lo = jnp.where(upper, 0.0, kf * f).astype(BF16)
            far.append(_dot_tn(k_lo, q_up))
        yield 32, 60
        band = jnp.tile(band_ref[slot, hh], (HALF // GLA_BLOCK, 1))

        def diagonal(c):
            sl = slice(c * HALF, (c + 1) * HALF)
            blk = far[1][sl, sl]
            for idx in range(2, len(GLA_LEVELS)):
                blk = jnp.where(level == idx, far[idx][sl, sl], blk)
            return jnp.where(level == GLA_BAND, band[:, sl], blk)

        at = jnp.concatenate(
            [jnp.concatenate([diagonal(0), far[0][0:HALF, HALF:]], axis=1),
             jnp.concatenate([jnp.zeros((HALF, HALF), F32), diagonal(1)], axis=1)], axis=0)
        o = o + _dot(vt, at.astype(BF16))

        yield 16, 45
        ms = jnp.mean(o * o, axis=0, keepdims=True)
        o_ref[0, rv, ts] = (o * lax.rsqrt(ms + EPS) * gn_ref[...]).astype(BF16)

        state_ref[hh] = state * f_in[:, TILE - 1:TILE] + _dot_nt(k_out, vt)

    def program():
        tiles = qt_ref.shape[2] // TILE
        yield from prepare(0)
        for tile in range(tiles):
            if tile + 1 < tiles:
                yield from prepare(tile + 1)
            for hh in range(heads):
                yield from attend(tile, hh)

    tiles = qt_ref.shape[2] // TILE
    mxu = 8 + 8 * GLA_PARTS * heads * GLA_NFACT + heads * (20 + 64 * len(GLA_LEVELS) + 32 + 16)
    vpu = 50 * heads + 6 * heads * GLA_NFACT + 4 * heads * (GLA_BLOCK - 1 + GLA_BLOCK.bit_length() - 1) + 12 * heads * GLA_BLOCK \
        + heads * (25 + 20 * len(GLA_LEVELS) + 60 + 45)
    return program(), (float(tiles * mxu), float(tiles * vpu))


def _mixers_kernel(q_ref, k_ref, vtm_ref, gq_ref, gk_ref, gv_ref, glow_ref, wg_ref, bg_ref, seg_ref,
                   level_ref, gn_ref, om_ref, og_ref,
                   s_ref, p_ref, vx_ref, state_ref, fact_ref, band_ref):
    mla, mla_totals = _mla_stages(q_ref, k_ref, vtm_ref, om_ref, s_ref, p_ref, vx_ref)
    gla, gla_totals = _gla_stages(gq_ref, gk_ref, gv_ref, glow_ref, wg_ref, bg_ref, seg_ref, level_ref,
                                  gn_ref, og_ref, state_ref, fact_ref, band_ref)
    _interleave([mla, gla], [mla_totals, gla_totals])


def _interleave(programs, totals):
    all_mxu = sum(t[0] for t in totals)
    all_vpu = sum(t[1] for t in totals)
    pending = [next(p) for p in programs]
    spent = [0.0] * len(programs)
    mxu = vpu = 0.0
    live = set(range(len(programs)))
    while live:
        def cost(i):
            m, v = pending[i]
            skew = abs((mxu + m) / all_mxu - (vpu + v) / all_vpu)
            lag = spent[i] / sum(totals[i]) - min(spent[j] / sum(totals[j]) for j in live)
            return skew + lag
        i = min(live, key=cost)
        m, v = pending[i]
        mxu, vpu, spent[i] = mxu + m, vpu + v, spent[i] + m + v
        try:
            pending[i] = next(programs[i])
        except StopIteration:
            live.discard(i)


def _mixers(q, k, vtm, gq, gk, gv, glow, wg, bg, seg, level, gn):
    b, heads, s, _ = q.shape
    hp = heads // MIX_SPLIT
    gh = GLA_HEADS // MIX_SPLIT
    gk_rows, gv_rows = gh * GLA_HEAD_K, gh * GLA_HEAD_V

    def full(a):
        return pl.BlockSpec(a.shape, lambda i, j: (0,) * a.ndim)

    def head_rows(n):
        return pl.BlockSpec((1, n, s), lambda i, j: (i, j, 0))

    tok = pl.BlockSpec((1, hp, s, HEAD_LANES), lambda i, j: (i, j, 0, 0))
    trm = pl.BlockSpec((1, hp, MLA_V, s), lambda i, j: (i, j, 0, 0))
    in_specs = [tok, tok, trm, head_rows(gk_rows), head_rows(gk_rows), head_rows(gv_rows),
                pl.BlockSpec((1, GLA_GATE_RANK, s), lambda i, j: (i, 0, 0)),
                pl.BlockSpec((gk_rows, GLA_GATE_RANK), lambda i, j: (j, 0)),
                pl.BlockSpec((gk_rows, 1), lambda i, j: (j, 0)),
                full(seg), full(level), full(gn)]
    return pl.pallas_call(
        _mixers_kernel,
        grid=(b, MIX_SPLIT),
        in_specs=in_specs,
        out_specs=(trm, head_rows(gv_rows)),
        out_shape=(jax.ShapeDtypeStruct((b, heads, MLA_V, s), BF16),
                   jax.ShapeDtypeStruct((b, GLA_DV, s), BF16)),
        scratch_shapes=[pltpu.VMEM((2, s, TILE), F32),
                        pltpu.VMEM((2, s, TILE), BF16),
                        pltpu.VMEM((hp, MLA_VX, s), BF16),
                        pltpu.VMEM((gh, GLA_HEAD_K, GLA_HEAD_V), F32),
                        pltpu.VMEM((2, GLA_NFACT, gk_rows, TILE), F32),
                        pltpu.VMEM((2, gh, GLA_BLOCK, TILE), F32)],
        compiler_params=pltpu.CompilerParams(
            dimension_semantics=("parallel", "parallel"), vmem_limit_bytes=VMEM_LIMIT),
        name="mixers",
    )(q, k, vtm, gq, gk, gv, glow, wg, bg, seg, level, gn)


def _out_kernel(x_ref, p_ref, om_ref, gm_ref, og_ref, gg_ref, ma_ref, mb_ref, wa_ref, wb_ref,
                wout_ref, wple_ref, gple_ref, ggate_ref, wgate_ref, gfin_ref, o_ref, *, final):
    def branch(o_t, g_t_ref, w_ref):
        gh = g_t_ref[0]
        zt = o_t * (gh * (jnp.tanh(gh) + 1.0))
        return _dot_tn(zt, w_ref[...])

    ya = branch(om_ref[0].reshape(MLA_WIDTH, OUT_ROWS), gm_ref, wa_ref)
    yb = branch(og_ref[0], gg_ref, wb_ref)
    merged2 = ((jnp.tanh(ma_ref[0]) + 1.0) * ya.astype(BF16)
               + (jnp.tanh(mb_ref[0]) + 1.0) * yb.astype(BF16))
    x1 = x_ref[0] + _dot(merged2, wout_ref[...])
    e_half = _rms_rows(_dot(p_ref[0].astype(BF16), wple_ref[...]), gple_ref[...])
    gate2 = jnp.tanh(_dot(_rms_rows(x1, ggate_ref[...]).astype(BF16), wgate_ref[...])) + 1.0
    x2 = x1 + gate2 * e_half
    o_ref[0] = _rms_rows(x2, gfin_ref[...]) if final else x2


def _output(x, p, om, gm, og, gg, ma, mb, wa, wb, wout, wple, gple, ggate, wgate, gfin, final):
    b, s, d = x.shape
    rows = OUT_ROWS

    def full(a):
        return pl.BlockSpec(a.shape, lambda i, j: (0,) * a.ndim)

    def tok(width):
        return pl.BlockSpec((1, rows, width), lambda i, j: (i, j, 0))

    def tr(n):
        return pl.BlockSpec((1, n, rows), lambda i, j: (i, 0, j))

    heads_tr = pl.BlockSpec((1, MLA_HEADS, MLA_V, rows), lambda i, j: (i, 0, 0, j))
    in_specs = [tok(d), tok(p.shape[-1]), heads_tr, tr(MLA_WIDTH), tr(GLA_DV), tr(GLA_DV),
                tok(d), tok(d), full(wa), full(wb), full(wout), full(wple), full(gple), full(ggate),
                full(wgate), full(gfin)]
    return pl.pallas_call(
        functools.partial(_out_kernel, final=final),
        grid=(b, s // rows),
        in_specs=in_specs,
        out_specs=tok(d),
        out_shape=jax.ShapeDtypeStruct((b, s, d), F32),
        compiler_params=pltpu.CompilerParams(
            dimension_semantics=("parallel", "parallel"), vmem_limit_bytes=VMEM_LIMIT),
        name="out",
    )(x, p, om, gm, og, gg, ma, mb, wa, wb, wout, wple, gple, ggate, wgate, gfin)


def _swap_halves(w):
    half = w.shape[-1] // 2
    return jnp.concatenate([-w[..., half:], w[..., :half]], axis=-1)


def _rope_kernel(pos_ref, freq_ref, phase_ref, o_ref):
    pos = pos_ref[0].astype(F32)
    slot = lax.broadcasted_iota(jnp.int32, o_ref.shape[1:], 1) // MLA_ROPE
    posx = pos[:, 0:1]
    for t in range(1, ROPE_PACK):
        posx = jnp.where(slot == t, pos[:, t:t + 1], posx)
    o_ref[0] = jnp.cos(posx * freq_ref[...] - phase_ref[...])


def _rope_table(positions):
    b, s = positions.shape
    inv_freq = 1.0 / (ROPE_THETA ** (jnp.arange(0, MLA_ROPE, 2, dtype=F32) / MLA_ROPE))
    half = MLA_ROPE // 2
    freq = jnp.tile(jnp.concatenate([inv_freq, inv_freq]), ROPE_PACK)[None, :]
    phase = jnp.tile(jnp.concatenate([jnp.zeros((half,), F32), jnp.full((half,), np.pi / 2, F32)]),
                     ROPE_PACK)[None, :]
    rows = s // ROPE_PACK
    out = pl.pallas_call(
        _rope_kernel,
        grid=(b,),
        in_specs=[pl.BlockSpec((1, rows, ROPE_PACK), lambda i: (i, 0, 0)),
                  pl.BlockSpec((1, ROPE_PACK * MLA_ROPE), lambda i: (0, 0)),
                  pl.BlockSpec((1, ROPE_PACK * MLA_ROPE), lambda i: (0, 0))],
        out_specs=pl.BlockSpec((1, rows, ROPE_PACK * MLA_ROPE), lambda i: (i, 0, 0)),
        out_shape=jax.ShapeDtypeStruct((b, rows, ROPE_PACK * MLA_ROPE), F32),
        compiler_params=pltpu.CompilerParams(dimension_semantics=("parallel",)),
        name="rope",
    )(positions.reshape(b, rows, ROPE_PACK), freq, phase)
    return out.reshape(b, s, MLA_ROPE)


def kernel(x, p, positions, norm_in_g, w_in, q_norm_g, w_uq, kv_norm_g, w_ukv, w_gk_up, b_gk,
           gla_norm_g, w_mla_br, w_gla_br, w_out, w_ple, ple_norm_g, ple_gate_norm_g, w_ple_gate,
           final_norm_g):
    b, s, d = x.shape
    depth = w_in.shape[0]
    assert s % PROJ_ROWS == 0 and s % OUT_ROWS == 0 and s % TILE == 0
    cs = _rope_table(positions)
    seg_np, level_np = _gla_constants()
    seg = jnp.asarray(seg_np, BF16)
    level = jnp.asarray(level_np)

    sizes = (MLA_Q_RANK, MLA_KV_RANK, MLA_ROPE, MLA_WIDTH, GLA_DK, GLA_DK, GLA_DV, GLA_GATE_RANK,
             GLA_DV, d, d)
    offs = np.concatenate([[0], np.cumsum(sizes)])

    def cols(w, idx):
        return w[:, int(offs[idx]):int(offs[idx + 1])]

    for l in range(depth):
        w = w_in[l]
        wkr = cols(w, 2)
        wstd = jnp.concatenate(
            [cols(w, 0), cols(w, 1), jnp.zeros((d, MLA_NOPE), F32), wkr, _swap_halves(wkr),
             cols(w, 9) * 0.5, cols(w, 10) * 0.5], axis=1).astype(BF16)
        wt = jnp.concatenate(
            [cols(w, 3) * 0.5, cols(w, 4) * (GLA_HEAD_K ** -0.5), cols(w, 5), cols(w, 6),
             cols(w, 8) * 0.5, cols(w, 7)], axis=1).T.astype(BF16)
        uq = w_uq[l].reshape(MLA_Q_RANK, MLA_HEADS, MLA_NOPE + MLA_ROPE)
        uq_r = uq[..., MLA_NOPE:]
        wq = jnp.concatenate([uq[..., :MLA_NOPE], uq_r, _swap_halves(uq_r)], axis=-1)
        wq = wq.reshape(MLA_Q_RANK, MLA_HEADS * HEAD_LANES).astype(BF16)
        ukv = w_ukv[l].reshape(MLA_KV_RANK, MLA_HEADS, MLA_NOPE + MLA_V)
        wk = jnp.concatenate([ukv[..., :MLA_NOPE], jnp.zeros_like(ukv[..., :MLA_NOPE])], axis=-1)
        wk = wk.reshape(MLA_KV_RANK, MLA_HEADS * HEAD_LANES).astype(BF16)
        wvt = ukv[..., MLA_NOPE:].reshape(MLA_KV_RANK, MLA_WIDTH).T.astype(BF16)

        (q, k, vt, gmla, gq, gk, gv, ggla, glow, ma, mb) = _projections(
            x, cs, norm_in_g[l][None, :], wstd, wt, q_norm_g[l][None, :], wq,
            kv_norm_g[l][None, :], wk, wvt)

        o_mla, o_gla = _mixers(q, k, vt, gq, gk, gv, glow, w_gk_up[l].T.astype(BF16), b_gk[l][:, None],
                               seg, level, gla_norm_g[l][:, None])

        x = _output(x, p[l], o_mla, gmla, o_gla, ggla, ma, mb, w_mla_br[l].astype(BF16),
                    w_gla_br[l].astype(BF16), (w_out[l] * 0.5).astype(BF16), w_ple[l].astype(BF16),
                    ple_norm_g[l][None, :] * 0.5, ple_gate_norm_g[l][None, :],
                    (w_ple_gate[l] * 0.5).astype(BF16),
                    final_norm_g[None, :], l == depth - 1)
    return x
```

```python
import functools

import numpy as np
import jax
import jax.numpy as jnp
from jax import lax
from jax.experimental import pallas as pl
from jax.experimental.pallas import tpu as pltpu

F32 = jnp.float32
BF16 = jnp.bfloat16

MLA_HEADS = 8
MLA_Q_RANK = 384
MLA_KV_RANK = 256
MLA_NOPE = 64
MLA_ROPE = 32
MLA_V = 64
MLA_WIDTH = MLA_HEADS * MLA_V
ROPE_THETA = 10000.0
GLA_HEADS = 4
GLA_HEAD_K = 64
GLA_HEAD_V = 128
GLA_DK = GLA_HEADS * GLA_HEAD_K
GLA_DV = GLA_HEADS * GLA_HEAD_V
GLA_GATE_RANK = 16
GLA_GATE_NORMALIZER = 16.0
EPS = 1e-6
LOG2E = 1.4426950408889634

HEAD_LANES = 128
TILE = 256
PROJ_ROWS = 512
OUT_ROWS = 1024
MIX_SPLIT = 2
MLA_VX = MLA_V + 16
ROPE_PACK = 128 // MLA_ROPE
VMEM_LIMIT = 56 * 1024 * 1024

GLA_BLOCK = 16
GLA_PARTS = 2
HALF = TILE // 2
GLA_LEVELS = (HALF, 64, 32, 16)
GLA_NFACT = 2 + len(GLA_LEVELS)
GLA_BAND = len(GLA_LEVELS)

NT = (((1,), (1,)), ((), ()))
TN = (((0,), (0,)), ((), ()))


def _dot(a, b):
    return jnp.dot(a, b, preferred_element_type=F32)


def _dot_nt(a, b):
    return lax.dot_general(a, b, NT, preferred_element_type=F32)


def _dot_tn(a, b):
    return lax.dot_general(a, b, TN, preferred_element_type=F32)


def _rms_rows(x, g):
    return x * lax.rsqrt(jnp.mean(x * x, axis=-1, keepdims=True) + EPS) * g


def _proj_kernel(x_ref, cs_ref, gin_ref, wstd_ref, wt_ref, gq_ref, wq_ref,
                 gkv_ref, wk_ref, wvt_ref,
                 q_out, k_out, vt_out, gmla_out, gq_out, gk_out, gv_out, ggla_out, glow_out,
                 ma_out, mb_out):
    h = _rms_rows(x_ref[0], gin_ref[...]).astype(BF16)

    ma_out[0] = _dot(h, wstd_ref[:, 768:1792]).astype(BF16)
    mb_out[0] = _dot(h, wstd_ref[:, 1792:2816]).astype(BF16)
    ht = _dot_nt(wt_ref[...], h)
    for out, lo, hi in ((gmla_out, 0, 512), (gq_out, 512, 768), (gk_out, 768, 1024),
                        (gv_out, 1024, 1536), (ggla_out, 1536, 2048), (glow_out, 2048, 2064)):
        out[0] = ht[lo:hi, :].astype(BF16)

    cos, sin = cs_ref[0, :, 0:MLA_ROPE // 2], cs_ref[0, :, MLA_ROPE // 2:MLA_ROPE]
    table = jnp.concatenate([jnp.ones((PROJ_ROWS, MLA_NOPE), F32), cos, cos, sin, sin], axis=1)

    cq = _rms_rows(_dot(h, wstd_ref[:, 0:384]), gq_ref[...]).astype(BF16)
    q = _dot(cq, wq_ref[...])
    qscale = table * ((MLA_NOPE + MLA_ROPE) ** -0.5 * LOG2E)
    for hh in range(MLA_HEADS):
        q_out[0, hh] = (q[:, hh * HEAD_LANES:(hh + 1) * HEAD_LANES] * qscale).astype(BF16)

    ckv = _rms_rows(_dot(h, wstd_ref[:, 384:640]), gkv_ref[...]).astype(BF16)
    u = _dot(h, wstd_ref[:, 640:768]) * table
    lane = lax.broadcasted_iota(jnp.int32, u.shape, 1)
    krot = jnp.where(lane >= MLA_NOPE, u + pltpu.roll(u, 32, 1) + pltpu.roll(u, 96, 1), 0.0)
    kk = _dot(ckv, wk_ref[...])
    for hh in range(MLA_HEADS):
        k_out[0, hh] = (kk[:, hh * HEAD_LANES:(hh + 1) * HEAD_LANES] + krot).astype(BF16)

    vt = _dot_nt(wvt_ref[...], ckv)
    for hh in range(MLA_HEADS):
        vt_out[0, hh] = vt[hh * MLA_V:(hh + 1) * MLA_V, :].astype(BF16)


def _projections(x, cs, gin, wstd, wt, gq, wq, gkv, wk, wvt):
    b, s, d = x.shape
    rows = PROJ_ROWS

    def full(a):
        return pl.BlockSpec(a.shape, lambda i, j: (0,) * a.ndim)

    def tok(width):
        return pl.BlockSpec((1, rows, width), lambda i, j: (i, j, 0))

    def tr(n):
        return pl.BlockSpec((1, n, rows), lambda i, j: (i, 0, j))

    def tr_shape(n):
        return jax.ShapeDtypeStruct((b, n, s), BF16)

    heads_tok = pl.BlockSpec((1, MLA_HEADS, rows, HEAD_LANES), lambda i, j: (i, 0, j, 0))
    heads_tr = pl.BlockSpec((1, MLA_HEADS, MLA_V, rows), lambda i, j: (i, 0, 0, j))
    out_shape = (
        jax.ShapeDtypeStruct((b, MLA_HEADS, s, HEAD_LANES), BF16),
        jax.ShapeDtypeStruct((b, MLA_HEADS, s, HEAD_LANES), BF16),
        jax.ShapeDtypeStruct((b, MLA_HEADS, MLA_V, s), BF16),
        tr_shape(MLA_WIDTH),
        tr_shape(GLA_DK),
        tr_shape(GLA_DK),
        tr_shape(GLA_DV),
        tr_shape(GLA_DV),
        tr_shape(GLA_GATE_RANK),
        jax.ShapeDtypeStruct((b, s, d), BF16),
        jax.ShapeDtypeStruct((b, s, d), BF16),
    )
    out_specs = (heads_tok, heads_tok, heads_tr, tr(MLA_WIDTH),
                 tr(GLA_DK), tr(GLA_DK), tr(GLA_DV), tr(GLA_DV), tr(GLA_GATE_RANK), tok(d), tok(d))
    in_specs = [tok(d), tok(MLA_ROPE), full(gin), full(wstd), full(wt), full(gq),
                full(wq), full(gkv), full(wk), full(wvt)]
    return pl.pallas_call(
        _proj_kernel,
        grid=(b, s // rows),
        in_specs=in_specs,
        out_specs=out_specs,
        out_shape=out_shape,
        compiler_params=pltpu.CompilerParams(
            dimension_semantics=("parallel", "parallel"), vmem_limit_bytes=VMEM_LIMIT),
        name="proj",
    )(x, cs, gin, wstd, wt, gq, wq, gkv, wk, wvt)


def _mla_stages(q_ref, k_ref, vt_ref, o_ref, s_ref, p_ref, vx_ref):
    heads, seq = q_ref.shape[1], q_ref.shape[2]
    row = lax.broadcasted_iota(jnp.int32, (TILE, TILE), 0)
    col = lax.broadcasted_iota(jnp.int32, (TILE, TILE), 1)
    causal = row <= col
    tasks = [(hh, qi) for hh in range(heads) for qi in range(seq // TILE)]

    for hh in range(heads):
        vx_ref[hh, 0:MLA_V, :] = vt_ref[0, hh]
        vx_ref[hh, MLA_V:, :] = jnp.ones((MLA_VX - MLA_V, seq), BF16)

    def scores(t):
        hh, qi = tasks[t]
        lo, hi = qi * TILE, (qi + 1) * TILE
        qt = q_ref[0, hh, lo:hi, :]
        if qi:
            s_ref[t % 2, 0:lo, :] = _dot_nt(k_ref[0, hh, 0:lo, :], qt)
        s_ref[t % 2, lo:hi, :] = jnp.where(causal, _dot_nt(k_ref[0, hh, lo:hi, :], qt), -jnp.inf)

    def softmax(t):
        _, qi = tasks[t]
        m = jnp.full((1, TILE), -jnp.inf, F32)
        for kj in range(qi + 1):
            yield 0, 18
            m = jnp.maximum(m, jnp.max(s_ref[t % 2, kj * TILE:(kj + 1) * TILE, :], axis=0, keepdims=True))
        for kj in range(qi + 1):
            yield 0, 45
            blk = slice(kj * TILE, (kj + 1) * TILE)
            p_ref[t % 2, blk, :] = jnp.exp2(s_ref[t % 2, blk, :] - m).astype(BF16)

    def weighted_values(t):
        hh, qi = tasks[t]
        lo, hi = qi * TILE, (qi + 1) * TILE
        acc = _dot(vx_ref[hh, :, 0:hi], p_ref[t % 2, 0:hi, :])
        o_ref[0, hh, :, lo:hi] = (acc[0:MLA_V] / acc[MLA_V:MLA_V + 1]).astype(BF16)

    def program():
        blocks = [qi + 1 for _, qi in tasks]
        yield 64 * blocks[0], 20 * blocks[0]
        scores(0)
        for t in range(len(tasks)):
            if t + 1 < len(tasks):
                yield 64 * blocks[t + 1], 20 * blocks[t + 1]
                scores(t + 1)
            if t:
                yield 32 * blocks[t - 1], 10
                weighted_values(t - 1)
            yield from softmax(t)
        yield 32 * blocks[-1], 10
        weighted_values(len(tasks) - 1)

    blocks = sum(qi + 1 for _, qi in tasks)
    return program(), (96.0 * blocks, 83.0 * blocks + 10.0 * len(tasks))


def _gla_constants():
    u = np.arange(TILE)[:, None]
    t = np.arange(TILE)[None, :]
    mats = [(u <= t), (u > t)]
    for m in GLA_LEVELS:
        pos = t % (2 * m)
        mid = t - pos + m - 1
        upper = pos >= m
        mats.append(np.where(upper, (u > mid) & (u <= t), (u > t) & (u <= mid)))
    seg = np.concatenate([mm.astype(np.float32) for mm in mats], axis=1)
    seg = np.concatenate([seg] * GLA_PARTS, axis=0)
    j = np.arange(TILE)[:, None]
    i = np.arange(TILE)[None, :]
    level = np.full((TILE, TILE), -1, np.int32)
    level[(j // GLA_BLOCK) == (i // GLA_BLOCK)] = GLA_BAND
    for idx, m in enumerate(GLA_LEVELS):
        same = (j // (2 * m)) == (i // (2 * m))
        split = ((j % (2 * m)) < m) & ((i % (2 * m)) >= m)
        level[same & split & (j < i)] = idx
    return seg, level[:HALF, :HALF]


def _gla_band(qf, kf, decay, band_ref):
    heads = qf.shape[0] // GLA_HEAD_K
    rel = (lax.broadcasted_iota(jnp.int32, (GLA_BLOCK, TILE), 1) % GLA_BLOCK
           - lax.broadcasted_iota(jnp.int32, (GLA_BLOCK, TILE), 0))
    w = [kf]
    span = decay
    step = 1
    while step < GLA_BLOCK:
        yield 0, 4 * heads * (step + 1)
        w += [span * pltpu.roll(w[d], step, 1) for d in range(step)]
        span = span * pltpu.roll(span, step, 1)
        step *= 2
    bands = [jnp.zeros((GLA_BLOCK, TILE), F32) for _ in range(heads)]
    for d in range(GLA_BLOCK):
        yield 0, 12 * heads
        prod = qf * w[d]
        for hh in range(heads):
            diag = jnp.sum(prod[hh * GLA_HEAD_K:(hh + 1) * GLA_HEAD_K], axis=0, keepdims=True)
            bands[hh] = jnp.where(rel == d, diag, bands[hh])
    for hh in range(heads):
        band_ref[hh] = bands[hh]


def _gla_stages(qt_ref, kt_ref, vt_ref, glow_ref, wg_ref, bg_ref, seg_ref, level_ref, gn_ref,
                o_ref, state_ref, fact_ref, band_ref):
    heads = qt_ref.shape[1] // GLA_HEAD_K
    state_ref[...] = jnp.zeros_like(state_ref)
    level = level_ref[...]
    lane = lax.broadcasted_iota(jnp.int32, (GLA_HEAD_K, TILE), 1)

    def prepare(tile):
        ts = slice(tile * TILE, (tile + 1) * TILE)
        slot = tile % 2
        yield 8, 50 * heads
        z = _dot(wg_ref[...], glow_ref[0, :, ts]) + bg_ref[...]
        g = -(jnp.maximum(-z, 0.0) + jnp.log1p(jnp.exp(-jnp.abs(z)))) * (1.0 / GLA_GATE_NORMALIZER)
        parts, rest = [], g
        for _ in range(GLA_PARTS):
            parts.append(rest.astype(BF16))
            rest = rest - parts[-1].astype(F32)
        g3 = jnp.concatenate(parts, axis=1)
        for n in range(GLA_NFACT):
            yield 8 * GLA_PARTS * heads, 6 * heads
            fact_ref[slot, n] = jnp.exp(_dot(g3, seg_ref[:, n * TILE:(n + 1) * TILE]))
        yield from _gla_band(qt_ref[0, :, ts].astype(F32), kt_ref[0, :, ts].astype(F32), jnp.exp(g),
                             band_ref.at[slot])

    def attend(tile, hh):
        ts = slice(tile * TILE, (tile + 1) * TILE)
        slot = tile % 2
        rk = slice(hh * GLA_HEAD_K, (hh + 1) * GLA_HEAD_K)
        rv = slice(hh * GLA_HEAD_V, (hh + 1) * GLA_HEAD_V)
        yield 20, 25
        qf = qt_ref[0, rk, ts].astype(F32)
        kf = kt_ref[0, rk, ts].astype(F32)
        vt = vt_ref[0, rv, ts]
        f_in = fact_ref[slot, 0, rk, :]
        q_in = (qf * f_in).astype(BF16)
        k_out = (kf * fact_ref[slot, 1, rk, :]).astype(BF16)
        state = state_ref[hh]
        o = _dot_tn(state.astype(BF16), q_in)

        far = []
        for idx, m in enumerate(GLA_LEVELS):
            yield 64, 20
            f = fact_ref[slot, 2 + idx, rk, :]
            upper = (lane // m) % 2 == 1
            q_up = jnp.where(upper, qf * f, 0.0).astype(BF16)
            k_lo = jnp.where(upper, 0.0, kf * f).astype(BF16)
            far.append(_dot_tn(k_lo, q_up))
        yield 32, 60
        band = jnp.tile(band_ref[slot, hh], (HALF // GLA_BLOCK, 1))

        def diagonal(c):
            sl = slice(c * HALF, (c + 1) * HALF)
            blk = far[1][sl, sl]
            for idx in range(2, len(GLA_LEVELS)):
                blk = jnp.where(level == idx, far[idx][sl, sl], blk)
            return jnp.where(level == GLA_BAND, band[:, sl], blk)

        at = jnp.concatenate(
            [jnp.concatenate([diagonal(0), far[0][0:HALF, HALF:]], axis=1),
             jnp.concatenate([jnp.zeros((HALF, HALF), F32), diagonal(1)], axis=1)], axis=0)
        o = o + _dot(vt, at.astype(BF16))

        yield 16, 45
        ms = jnp.mean(o * o, axis=0, keepdims=True)
        o_ref[0, rv, ts] = (o * lax.rsqrt(ms + EPS) * gn_ref[...]).astype(BF16)

        state_ref[hh] = state * f_in[:, TILE - 1:TILE] + _dot_nt(k_out, vt)

    def program():
        tiles = qt_ref.shape[2] // TILE
        yield from prepare(0)
        for tile in range(tiles):
            if tile + 1 < tiles:
                yield from prepare(tile + 1)
            for hh in range(heads):
                yield from attend(tile, hh)

    tiles = qt_ref.shape[2] // TILE
    mxu = 8 + 8 * GLA_PARTS * heads * GLA_NFACT + heads * (20 + 64 * len(GLA_LEVELS) + 32 + 16)
    vpu = 50 * heads + 6 * heads * GLA_NFACT + 4 * heads * (GLA_BLOCK - 1 + GLA_BLOCK.bit_length() - 1) + 12 * heads * GLA_BLOCK \
        + heads * (25 + 20 * len(GLA_LEVELS) + 60 + 45)
    return program(), (float(tiles * mxu), float(tiles * vpu))


def _mixers_kernel(q_ref, k_ref, vtm_ref, gq_ref, gk_ref, gv_ref, glow_ref, wg_ref, bg_ref, seg_ref,
                   level_ref, gn_ref, om_ref, og_ref,
                   s_ref, p_ref, vx_ref, state_ref, fact_ref, band_ref):
    mla, mla_totals = _mla_stages(q_ref, k_ref, vtm_ref, om_ref, s_ref, p_ref, vx_ref)
    gla, gla_totals = _gla_stages(gq_ref, gk_ref, gv_ref, glow_ref, wg_ref, bg_ref, seg_ref, level_ref,
                                  gn_ref, og_ref, state_ref, fact_ref, band_ref)
    _interleave([mla, gla], [mla_totals, gla_totals])


def _interleave(programs, totals):
    all_mxu = sum(t[0] for t in totals)
    all_vpu = sum(t[1] for t in totals)
    pending = [next(p) for p in programs]
    spent = [0.0] * len(programs)
    mxu = vpu = 0.0
    live = set(range(len(programs)))
    while live:
        def cost(i):
            m, v = pending[i]
            skew = abs((mxu + m) / all_mxu - (vpu + v) / all_vpu)
            lag = spent[i] / sum(totals[i]) - min(spent[j] / sum(totals[j]) for j in live)
            return skew + lag
        i = min(live, key=cost)
        m, v = pending[i]
        mxu, vpu, spent[i] = mxu + m, vpu + v, spent[i] + m + v
        try:
            pending[i] = next(programs[i])
        except StopIteration:
            live.discard(i)


def _mixers(q, k, vtm, gq, gk, gv, glow, wg, bg, seg, level, gn):
    b, heads, s, _ = q.shape
    hp = heads // MIX_SPLIT
    gh = GLA_HEADS // MIX_SPLIT
    gk_rows, gv_rows = gh * GLA_HEAD_K, gh * GLA_HEAD_V

    def full(a):
        return pl.BlockSpec(a.shape, lambda i, j: (0,) * a.ndim)

    def head_rows(n):
        return pl.BlockSpec((1, n, s), lambda i, j: (i, j, 0))

    tok = pl.BlockSpec((1, hp, s, HEAD_LANES), lambda i, j: (i, j, 0, 0))
    trm = pl.BlockSpec((1, hp, MLA_V, s), lambda i, j: (i, j, 0, 0))
    in_specs = [tok, tok, trm, head_rows(gk_rows), head_rows(gk_rows), head_rows(gv_rows),
                pl.BlockSpec((1, GLA_GATE_RANK, s), lambda i, j: (i, 0, 0)),
                pl.BlockSpec((gk_rows, GLA_GATE_RANK), lambda i, j: (j, 0)),
                pl.BlockSpec((gk_rows, 1), lambda i, j: (j, 0)),
                full(seg), full(level), full(gn)]
    return pl.pallas_call(
        _mixers_kernel,
        grid=(b, MIX_SPLIT),
        in_specs=in_specs,
        out_specs=(trm, head_rows(gv_rows)),
        out_shape=(jax.ShapeDtypeStruct((b, heads, MLA_V, s), BF16),
                   jax.ShapeDtypeStruct((b, GLA_DV, s), BF16)),
        scratch_shapes=[pltpu.VMEM((2, s, TILE), F32),
                        pltpu.VMEM((2, s, TILE), BF16),
                        pltpu.VMEM((hp, MLA_VX, s), BF16),
                        pltpu.VMEM((gh, GLA_HEAD_K, GLA_HEAD_V), F32),
                        pltpu.VMEM((2, GLA_NFACT, gk_rows, TILE), F32),
                        pltpu.VMEM((2, gh, GLA_BLOCK, TILE), F32)],
        compiler_params=pltpu.CompilerParams(
            dimension_semantics=("parallel", "parallel"), vmem_limit_bytes=VMEM_LIMIT),
        name="mixers",
    )(q, k, vtm, gq, gk, gv, glow, wg, bg, seg, level, gn)


def _out_kernel(x_ref, p_ref, om_ref, gm_ref, og_ref, gg_ref, ma_ref, mb_ref, wa_ref, wb_ref,
                wout_ref, wple_ref, gple_ref, ggate_ref, wgate_ref, gfin_ref, o_ref, *, final):
    def branch(o_t, g_t_ref, w_ref):
        gh = g_t_ref[0]
        zt = o_t * (gh * (jnp.tanh(gh) + 1.0))
        return _dot_tn(zt, w_ref[...])

    ya = branch(om_ref[0].reshape(MLA_WIDTH, OUT_ROWS), gm_ref, wa_ref)
    yb = branch(og_ref[0], gg_ref, wb_ref)
    merged2 = ((jnp.tanh(ma_ref[0]) + 1.0) * ya.astype(BF16)
               + (jnp.tanh(mb_ref[0]) + 1.0) * yb.astype(BF16))
    x1 = x_ref[0] + _dot(merged2, wout_ref[...])
    e_half = _rms_rows(_dot(p_ref[0].astype(BF16), wple_ref[...]), gple_ref[...])
    gate2 = jnp.tanh(_dot(_rms_rows(x1, ggate_ref[...]).astype(BF16), wgate_ref[...])) + 1.0
    x2 = x1 + gate2 * e_half
    o_ref[0] = _rms_rows(x2, gfin_ref[...]) if final else x2


def _output(x, p, om, gm, og, gg, ma, mb, wa, wb, wout, wple, gple, ggate, wgate, gfin, final):
    b, s, d = x.shape
    rows = OUT_ROWS

    def full(a):
        return pl.BlockSpec(a.shape, lambda i, j: (0,) * a.ndim)

    def tok(width):
        return pl.BlockSpec((1, rows, width), lambda i, j: (i, j, 0))

    def tr(n):
        return pl.BlockSpec((1, n, rows), lambda i, j: (i, 0, j))

    heads_tr = pl.BlockSpec((1, MLA_HEADS, MLA_V, rows), lambda i, j: (i, 0, 0, j))
    in_specs = [tok(d), tok(p.shape[-1]), heads_tr, tr(MLA_WIDTH), tr(GLA_DV), tr(GLA_DV),
                tok(d), tok(d), full(wa), full(wb), full(wout), full(wple), full(gple), full(ggate),
                full(wgate), full(gfin)]
    return pl.pallas_call(
        functools.partial(_out_kernel, final=final),
        grid=(b, s // rows),
        in_specs=in_specs,
        out_specs=tok(d),
        out_shape=jax.ShapeDtypeStruct((b, s, d), F32),
        compiler_params=pltpu.CompilerParams(
            dimension_semantics=("parallel", "parallel"), vmem_limit_bytes=VMEM_LIMIT),
        name="out",
    )(x, p, om, gm, og, gg, ma, mb, wa, wb, wout, wple, gple, ggate, wgate, gfin)


def _swap_halves(w):
    half = w.shape[-1] // 2
    return jnp.concatenate([-w[..., half:], w[..., :half]], axis=-1)


def _rope_kernel(pos_ref, freq_ref, phase_ref, o_ref):
    pos = pos_ref[0].astype(F32)
    slot = lax.broadcasted_iota(jnp.int32, o_ref.shape[1:], 1) // MLA_ROPE
    posx = pos[:, 0:1]
    for t in range(1, ROPE_PACK):
        posx = jnp.where(slot == t, pos[:, t:t + 1], posx)
    o_ref[0] = jnp.cos(posx * freq_ref[...] - phase_ref[...])


def _rope_table(positions):
    b, s = positions.shape
    inv_freq = 1.0 / (ROPE_THETA ** (jnp.arange(0, MLA_ROPE, 2, dtype=F32) / MLA_ROPE))
    half = MLA_ROPE // 2
    freq = jnp.tile(jnp.concatenate([inv_freq, inv_freq]), ROPE_PACK)[None, :]
    phase = jnp.tile(jnp.concatenate([jnp.zeros((half,), F32), jnp.full((half,), np.pi / 2, F32)]),
                     ROPE_PACK)[None, :]
    rows = s // ROPE_PACK
    out = pl.pallas_call(
        _rope_kernel,
        grid=(b,),
        in_specs=[pl.BlockSpec((1, rows, ROPE_PACK), lambda i: (i, 0, 0)),
                  pl.BlockSpec((1, ROPE_PACK * MLA_ROPE), lambda i: (0, 0)),
                  pl.BlockSpec((1, ROPE_PACK * MLA_ROPE), lambda i: (0, 0))],
        out_specs=pl.BlockSpec((1, rows, ROPE_PACK * MLA_ROPE), lambda i: (i, 0, 0)),
        out_shape=jax.ShapeDtypeStruct((b, rows, ROPE_PACK * MLA_ROPE), F32),
        compiler_params=pltpu.CompilerParams(dimension_semantics=("parallel",)),
        name="rope",
    )(positions.reshape(b, rows, ROPE_PACK), freq, phase)
    return out.reshape(b, s, MLA_ROPE)


def kernel(x, p, positions, norm_in_g, w_in, q_norm_g, w_uq, kv_norm_g, w_ukv, w_gk_up, b_gk,
           gla_norm_g, w_mla_br, w_gla_br, w_out, w_ple, ple_norm_g, ple_gate_norm_g, w_ple_gate,
           final_norm_g):
    b, s, d = x.shape
    depth = w_in.shape[0]
    assert s % PROJ_ROWS == 0 and s % OUT_ROWS == 0 and s % TILE == 0
    cs = _rope_table(positions)
    seg_np, level_np = _gla_constants()
    seg = jnp.asarray(seg_np, BF16)
    level = jnp.asarray(level_np)

    sizes = (MLA_Q_RANK, MLA_KV_RANK, MLA_ROPE, MLA_WIDTH, GLA_DK, GLA_DK, GLA_DV, GLA_GATE_RANK,
             GLA_DV, d, d)
    offs = np.concatenate([[0], np.cumsum(sizes)])

    def cols(w, idx):
        return w[:, int(offs[idx]):int(offs[idx + 1])]

    for l in range(depth):
        w = w_in[l]
        wkr = cols(w, 2)
        wstd = jnp.concatenate(
            [cols(w, 0), cols(w, 1), jnp.zeros((d, MLA_NOPE), F32), wkr, _swap_halves(wkr),
             cols(w, 9) * 0.5, cols(w, 10) * 0.5], axis=1).astype(BF16)
        wt = jnp.concatenate(
            [cols(w, 3) * 0.5, cols(w, 4) * (GLA_HEAD_K ** -0.5), cols(w, 5), cols(w, 6),
             cols(w, 8) * 0.5, cols(w, 7)], axis=1).T.astype(BF16)
        uq = w_uq[l].reshape(MLA_Q_RANK, MLA_HEADS, MLA_NOPE + MLA_ROPE)
        uq_r = uq[..., MLA_NOPE:]
        wq = jnp.concatenate([uq[..., :MLA_NOPE], uq_r, _swap_halves(uq_r)], axis=-1)
        wq = wq.reshape(MLA_Q_RANK, MLA_HEADS * HEAD_LANES).astype(BF16)
        ukv = w_ukv[l].reshape(MLA_KV_RANK, MLA_HEADS, MLA_NOPE + MLA_V)
        wk = jnp.concatenate([ukv[..., :MLA_NOPE], jnp.zeros_like(ukv[..., :MLA_NOPE])], axis=-1)
        wk = wk.reshape(MLA_KV_RANK, MLA_HEADS * HEAD_LANES).astype(BF16)
        wvt = ukv[..., MLA_NOPE:].reshape(MLA_KV_RANK, MLA_WIDTH).T.astype(BF16)

        (q, k, vt, gmla, gq, gk, gv, ggla, glow, ma, mb) = _projections(
            x, cs, norm_in_g[l][None, :], wstd, wt, q_norm_g[l][None, :], wq,
            kv_norm_g[l][None, :], wk, wvt)

        o_mla, o_gla = _mixers(q, k, vt, gq, gk, gv, glow, w_gk_up[l].T.astype(BF16), b_gk[l][:, None],
                               seg, level, gla_norm_g[l][:, None])

        x = _output(x, p[l], o_mla, gmla, o_gla, ggla, ma, mb, w_mla_br[l].astype(BF16),
                    w_gla_br[l].astype(BF16), (w_out[l] * 0.5).astype(BF16), w_ple[l].astype(BF16),
                    ple_norm_g[l][None, :] * 0.5, ple_gate_norm_g[l][None, :],
                    (w_ple_gate[l] * 0.5).astype(BF16),
                    final_norm_g[None, :], l == depth - 1)
    return x
```

```python
import functools

import numpy as np
import jax
import jax.numpy as jnp
from jax import lax
from jax.experimental import pallas as pl
from jax.experimental.pallas import tpu as pltpu

F32 = jnp.float32
BF16 = jnp.bfloat16

MLA_HEADS = 8
MLA_Q_RANK = 384
MLA_KV_RANK = 256
MLA_NOPE = 64
MLA_ROPE = 32
MLA_V = 64
MLA_WIDTH = MLA_HEADS * MLA_V
ROPE_THETA = 10000.0
GLA_HEADS = 4
GLA_HEAD_K = 64
GLA_HEAD_V = 128
GLA_DK = GLA_HEADS * GLA_HEAD_K
GLA_DV = GLA_HEADS * GLA_HEAD_V
GLA_GATE_RANK = 16
GLA_GATE_NORMALIZER = 16.0
EPS = 1e-6
LOG2E = 1.4426950408889634

HEAD_LANES = 128
TILE = 256
PROJ_ROWS = 512
OUT_ROWS = 1024
MIX_SPLIT = 2
MLA_VX = MLA_V + 16
ROPE_PACK = 128 // MLA_ROPE
VMEM_LIMIT = 56 * 1024 * 1024

GLA_BLOCK = 16
GLA_PARTS = 2
HALF = TILE // 2
GLA_LEVELS = (HALF, 64, 32, 16)
GLA_NFACT = 2 + len(GLA_LEVELS)
GLA_BAND = len(GLA_LEVELS)

NT = (((1,), (1,)), ((), ()))
TN = (((0,), (0,)), ((), ()))


def _dot(a, b):
    return jnp.dot(a, b, preferred_element_type=F32)


def _dot_nt(a, b):
    return lax.dot_general(a, b, NT, preferred_element_type=F32)


def _dot_tn(a, b):
    return lax.dot_general(a, b, TN, preferred_element_type=F32)


def _rms_rows(x, g):
    return x * lax.rsqrt(jnp.mean(x * x, axis=-1, keepdims=True) + EPS) * g


def _proj_kernel(x_ref, cs_ref, gin_ref, wstd_ref, wt_ref, gq_ref, wq_ref,
                 gkv_ref, wk_ref, wvt_ref,
                 q_out, k_out, vt_out, gmla_out, gq_out, gk_out, gv_out, ggla_out, glow_out,
                 ma_out, mb_out):
    h = _rms_rows(x_ref[0], gin_ref[...]).astype(BF16)

    ma_out[0] = _dot(h, wstd_ref[:, 768:1792]).astype(BF16)
    mb_out[0] = _dot(h, wstd_ref[:, 1792:2816]).astype(BF16)
    ht = _dot_nt(wt_ref[...], h)
    for out, lo, hi in ((gmla_out, 0, 512), (gq_out, 512, 768), (gk_out, 768, 1024),
                        (gv_out, 1024, 1536), (ggla_out, 1536, 2048), (glow_out, 2048, 2064)):
        out[0] = ht[lo:hi, :].astype(BF16)

    cos, sin = cs_ref[0, :, 0:MLA_ROPE // 2], cs_ref[0, :, MLA_ROPE // 2:MLA_ROPE]
    table = jnp.concatenate([jnp.ones((PROJ_ROWS, MLA_NOPE), F32), cos, cos, sin, sin], axis=1)

    cq = _rms_rows(_dot(h, wstd_ref[:, 0:384]), gq_ref[...]).astype(BF16)
    q = _dot(cq, wq_ref[...])
    qscale = table * ((MLA_NOPE + MLA_ROPE) ** -0.5 * LOG2E)
    for hh in range(MLA_HEADS):
        q_out[0, hh] = (q[:, hh * HEAD_LANES:(hh + 1) * HEAD_LANES] * qscale).astype(BF16)

    ckv = _rms_rows(_dot(h, wstd_ref[:, 384:640]), gkv_ref[...]).astype(BF16)
    u = _dot(h, wstd_ref[:, 640:768]) * table
    lane = lax.broadcasted_iota(jnp.int32, u.shape, 1)
    krot = jnp.where(lane >= MLA_NOPE, u + pltpu.roll(u, 32, 1) + pltpu.roll(u, 96, 1), 0.0)
    kk = _dot(ckv, wk_ref[...])
    for hh in range(MLA_HEADS):
        k_out[0, hh] = (kk[:, hh * HEAD_LANES:(hh + 1) * HEAD_LANES] + krot).astype(BF16)

    vt = _dot_nt(wvt_ref[...], ckv)
    for hh in range(MLA_HEADS):
        vt_out[0, hh] = vt[hh * MLA_V:(hh + 1) * MLA_V, :].astype(BF16)


def _projections(x, cs, gin, wstd, wt, gq, wq, gkv, wk, wvt):
    b, s, d = x.shape
    rows = PROJ_ROWS

    def full(a):
        return pl.BlockSpec(a.shape, lambda i, j: (0,) * a.ndim)

    def tok(width):
        return pl.BlockSpec((1, rows, width), lambda i, j: (i, j, 0))

    def tr(n):
        return pl.BlockSpec((1, n, rows), lambda i, j: (i, 0, j))

    def tr_shape(n):
        return jax.ShapeDtypeStruct((b, n, s), BF16)

    heads_tok = pl.BlockSpec((1, MLA_HEADS, rows, HEAD_LANES), lambda i, j: (i, 0, j, 0))
    heads_tr = pl.BlockSpec((1, MLA_HEADS, MLA_V, rows), lambda i, j: (i, 0, 0, j))
    out_shape = (
        jax.ShapeDtypeStruct((b, MLA_HEADS, s, HEAD_LANES), BF16),
        jax.ShapeDtypeStruct((b, MLA_HEADS, s, HEAD_LANES), BF16),
        jax.ShapeDtypeStruct((b, MLA_HEADS, MLA_V, s), BF16),
        tr_shape(MLA_WIDTH),
        tr_shape(GLA_DK),
        tr_shape(GLA_DK),
        tr_shape(GLA_DV),
        tr_shape(GLA_DV),
        tr_shape(GLA_GATE_RANK),
        jax.ShapeDtypeStruct((b, s, d), BF16),
        jax.ShapeDtypeStruct((b, s, d), BF16),
    )
    out_specs = (heads_tok, heads_tok, heads_tr, tr(MLA_WIDTH),
                 tr(GLA_DK), tr(GLA_DK), tr(GLA_DV), tr(GLA_DV), tr(GLA_GATE_RANK), tok(d), tok(d))
    in_specs = [tok(d), tok(MLA_ROPE), full(gin), full(wstd), full(wt), full(gq),
                full(wq), full(gkv), full(wk), full(wvt)]
    return pl.pallas_call(
        _proj_kernel,
        grid=(b, s // rows),
        in_specs=in_specs,
        out_specs=out_specs,
        out_shape=out_shape,
        compiler_params=pltpu.CompilerParams(
            dimension_semantics=("parallel", "parallel"), vmem_limit_bytes=VMEM_LIMIT),
        name="proj",
    )(x, cs, gin, wstd, wt, gq, wq, gkv, wk, wvt)


def _mla_stages(q_ref, k_ref, vt_ref, o_ref, s_ref, p_ref, vx_ref):
    heads, seq = q_ref.shape[1], q_ref.shape[2]
    row = lax.broadcasted_iota(jnp.int32, (TILE, TILE), 0)
    col = lax.broadcasted_iota(jnp.int32, (TILE, TILE), 1)
    causal = row <= col
    tasks = [(hh, qi) for hh in range(heads) for qi in range(seq // TILE)]

    for hh in range(heads):
        vx_ref[hh, 0:MLA_V, :] = vt_ref[0, hh]
        vx_ref[hh, MLA_V:, :] = jnp.ones((MLA_VX - MLA_V, seq), BF16)

    def scores(t):
        hh, qi = tasks[t]
        lo, hi = qi * TILE, (qi + 1) * TILE
        qt = q_ref[0, hh, lo:hi, :]
        if qi:
            s_ref[t % 2, 0:lo, :] = _dot_nt(k_ref[0, hh, 0:lo, :], qt)
        s_ref[t % 2, lo:hi, :] = jnp.where(causal, _dot_nt(k_ref[0, hh, lo:hi, :], qt), -jnp.inf)

    def softmax(t):
        _, qi = tasks[t]
        m = jnp.full((1, TILE), -jnp.inf, F32)
        for kj in range(qi + 1):
            yield 0, 18
            m = jnp.maximum(m, jnp.max(s_ref[t % 2, kj * TILE:(kj + 1) * TILE, :], axis=0, keepdims=True))
        for kj in range(qi + 1):
            yield 0, 45
            blk = slice(kj * TILE, (kj + 1) * TILE)
            p_ref[t % 2, blk, :] = jnp.exp2(s_ref[t % 2, blk, :] - m).astype(BF16)

    def weighted_values(t):
        hh, qi = tasks[t]
        lo, hi = qi * TILE, (qi + 1) * TILE
        acc = _dot(vx_ref[hh, :, 0:hi], p_ref[t % 2, 0:hi, :])
        o_ref[0, hh, :, lo:hi] = (acc[0:MLA_V] / acc[MLA_V:MLA_V + 1]).astype(BF16)

    def program():
        blocks = [qi + 1 for _, qi in tasks]
        yield 64 * blocks[0], 20 * blocks[0]
        scores(0)
        for t in range(len(tasks)):
            if t + 1 < len(tasks):
                yield 64 * blocks[t + 1], 20 * blocks[t + 1]
                scores(t + 1)
            if t:
                yield 32 * blocks[t - 1], 10
                weighted_values(t - 1)
            yield from softmax(t)
        yield 32 * blocks[-1], 10
        weighted_values(len(tasks) - 1)

    blocks = sum(qi + 1 for _, qi in tasks)
    return program(), (96.0 * blocks, 83.0 * blocks + 10.0 * len(tasks))


def _gla_constants():
    u = np.arange(TILE)[:, None]
    t = np.arange(TILE)[None, :]
    mats = [(u <= t), (u > t)]
    for m in GLA_LEVELS:
        pos = t % (2 * m)
        mid = t - pos + m - 1
        upper = pos >= m
        mats.append(np.where(upper, (u > mid) & (u <= t), (u > t) & (u <= mid)))
    seg = np.concatenate([mm.astype(np.float32) for mm in mats], axis=1)
    seg = np.concatenate([seg] * GLA_PARTS, axis=0)
    j = np.arange(TILE)[:, None]
    i = np.arange(TILE)[None, :]
    level = np.full((TILE, TILE), -1, np.int32)
    level[(j // GLA_BLOCK) == (i // GLA_BLOCK)] = GLA_BAND
    for idx, m in enumerate(GLA_LEVELS):
        same = (j // (2 * m)) == (i // (2 * m))
        split = ((j % (2 * m)) < m) & ((i % (2 * m)) >= m)
        level[same & split & (j < i)] = idx
    return seg, level[:HALF, :HALF]


def _gla_band(qf, kf, decay, band_ref):
    heads = qf.shape[0] // GLA_HEAD_K
    rel = (lax.broadcasted_iota(jnp.int32, (GLA_BLOCK, TILE), 1) % GLA_BLOCK
           - lax.broadcasted_iota(jnp.int32, (GLA_BLOCK, TILE), 0))
    w = [kf]
    span = decay
    step = 1
    while step < GLA_BLOCK:
        yield 0, 4 * heads * (step + 1)
        w += [span * pltpu.roll(w[d], step, 1) for d in range(step)]
        span = span * pltpu.roll(span, step, 1)
        step *= 2
    bands = [jnp.zeros((GLA_BLOCK, TILE), F32) for _ in range(heads)]
    for d in range(GLA_BLOCK):
        yield 0, 12 * heads
        prod = qf * w[d]
        for hh in range(heads):
            diag = jnp.sum(prod[hh * GLA_HEAD_K:(hh + 1) * GLA_HEAD_K], axis=0, keepdims=True)
            bands[hh] = jnp.where(rel == d, diag, bands[hh])
    for hh in range(heads):
        band_ref[hh] = bands[hh]


def _gla_stages(qt_ref, kt_ref, vt_ref, glow_ref, wg_ref, bg_ref, seg_ref, level_ref, gn_ref,
                o_ref, state_ref, fact_ref, band_ref):
    heads = qt_ref.shape[1] // GLA_HEAD_K
    state_ref[...] = jnp.zeros_like(state_ref)
    level = level_ref[...]
    lane = lax.broadcasted_iota(jnp.int32, (GLA_HEAD_K, TILE), 1)

    def prepare(tile):
        ts = slice(tile * TILE, (tile + 1) * TILE)
        slot = tile % 2
        yield 8, 50 * heads
        z = _dot(wg_ref[...], glow_ref[0, :, ts]) + bg_ref[...]
        g = -(jnp.maximum(-z, 0.0) + jnp.log1p(jnp.exp(-jnp.abs(z)))) * (1.0 / GLA_GATE_NORMALIZER)
        parts, rest = [], g
        for _ in range(GLA_PARTS):
            parts.append(rest.astype(BF16))
            rest = rest - parts[-1].astype(F32)
        g3 = jnp.concatenate(parts, axis=1)
        for n in range(GLA_NFACT):
            yield 8 * GLA_PARTS * heads, 6 * heads
            fact_ref[slot, n] = jnp.exp(_dot(g3, seg_ref[:, n * TILE:(n + 1) * TILE]))
        yield from _gla_band(qt_ref[0, :, ts].astype(F32), kt_ref[0, :, ts].astype(F32), jnp.exp(g),
                             band_ref.at[slot])

    def attend(tile, hh):
        ts = slice(tile * TILE, (tile + 1) * TILE)
        slot = tile % 2
        rk = slice(hh * GLA_HEAD_K, (hh + 1) * GLA_HEAD_K)
        rv = slice(hh * GLA_HEAD_V, (hh + 1) * GLA_HEAD_V)
        yield 20, 25
        qf = qt_ref[0, rk, ts].astype(F32)
        kf = kt_ref[0, rk, ts].astype(F32)
        vt = vt_ref[0, rv, ts]
        f_in = fact_ref[slot, 0, rk, :]
        q_in = (qf * f_in).astype(BF16)
        k_out = (kf * fact_ref[slot, 1, rk, :]).astype(BF16)
        state = state_ref[hh]
        o = _dot_tn(state.astype(BF16), q_in)

        far = []
        for idx, m in enumerate(GLA_LEVELS):
            yield 64, 20
            f = fact_ref[slot, 2 + idx, rk, :]
            upper = (lane // m) % 2 == 1
            q_up = jnp.where(upper, qf * f, 0.0).astype(BF16)
            k_lo = jnp.where(upper, 0.0, kf * f).astype(BF16)
            far.append(_dot_tn(k_lo, q_up))
        yield 32, 60
        band = jnp.tile(band_ref[slot, hh], (HALF // GLA_BLOCK, 1))

        def diagonal(c):
            sl = slice(c * HALF, (c + 1) * HALF)
            blk = far[1][sl, sl]
            for idx in range(2, len(GLA_LEVELS)):
                blk = jnp.where(level == idx, far[idx][sl, sl], blk)
            return jnp.where(level == GLA_BAND, band[:, sl], blk)

        at = jnp.concatenate(
            [jnp.concatenate([diagonal(0), far[0][0:HALF, HALF:]], axis=1),
             jnp.concatenate([jnp.zeros((HALF, HALF), F32), diagonal(1)], axis=1)], axis=0)
        o = o + _dot(vt, at.astype(BF16))

        yield 16, 45
        ms = jnp.mean(o * o, axis=0, keepdims=True)
        o_ref[0, rv, ts] = (o * lax.rsqrt(ms + EPS) * gn_ref[...]).astype(BF16)

        state_ref[hh] = state * f_in[:, TILE - 1:TILE] + _dot_nt(k_out, vt)

    def program():
        tiles = qt_ref.shape[2] // TILE
        yield from prepare(0)
        for tile in range(tiles):
            if tile + 1 < tiles:
                yield from prepare(tile + 1)
            for hh in range(heads):
                yield from attend(tile, hh)

    tiles = qt_ref.shape[2] // TILE
    mxu = 8 + 8 * GLA_PARTS * heads * GLA_NFACT + heads * (20 + 64 * len(GLA_LEVELS) + 32 + 16)
    vpu = 50 * heads + 6 * heads * GLA_NFACT + 4 * heads * (GLA_BLOCK - 1 + GLA_BLOCK.bit_length() - 1) + 12 * heads * GLA_BLOCK \
        + heads * (25 + 20 * len(GLA_LEVELS) + 60 + 45)
    return program(), (float(tiles * mxu), float(tiles * vpu))


def _mixers_kernel(q_ref, k_ref, vtm_ref, gq_ref, gk_ref, gv_ref, glow_ref, wg_ref, bg_ref, seg_ref,
                   level_ref, gn_ref, om_ref, og_ref,
                   s_ref, p_ref, vx_ref, state_ref, fact_ref, band_ref):
    mla, mla_totals = _mla_stages(q_ref, k_ref, vtm_ref, om_ref, s_ref, p_ref, vx_ref)
    gla, gla_totals = _gla_stages(gq_ref, gk_ref, gv_ref, glow_ref, wg_ref, bg_ref, seg_ref, level_ref,
                                  gn_ref, og_ref, state_ref, fact_ref, band_ref)
    _interleave([mla, gla], [mla_totals, gla_totals])


def _interleave(programs, totals):
    all_mxu = sum(t[0] for t in totals)
    all_vpu = sum(t[1] for t in totals)
    pending = [next(p) for p in programs]
    spent = [0.0] * len(programs)
    mxu = vpu = 0.0
    live = set(range(len(programs)))
    while live:
        def cost(i):
            m, v = pending[i]
            skew = abs((mxu + m) / all_mxu - (vpu + v) / all_vpu)
            lag = spent[i] / sum(totals[i]) - min(spent[j] / sum(totals[j]) for j in live)
            return skew + lag
        i = min(live, key=cost)
        m, v = pending[i]
        mxu, vpu, spent[i] = mxu + m, vpu + v, spent[i] + m + v
        try:
            pending[i] = next(programs[i])
        except StopIteration:
            live.discard(i)


def _mixers(q, k, vtm, gq, gk, gv, glow, wg, bg, seg, level, gn):
    b, heads, s, _ = q.shape
    hp = heads // MIX_SPLIT
    gh = GLA_HEADS // MIX_SPLIT
    gk_rows, gv_rows = gh * GLA_HEAD_K, gh * GLA_HEAD_V

    def full(a):
        return pl.BlockSpec(a.shape, lambda i, j: (0,) * a.ndim)

    def head_rows(n):
        return pl.BlockSpec((1, n, s), lambda i, j: (i, j, 0))

    tok = pl.BlockSpec((1, hp, s, HEAD_LANES), lambda i, j: (i, j, 0, 0))
    trm = pl.BlockSpec((1, hp, MLA_V, s), lambda i, j: (i, j, 0, 0))
    in_specs = [tok, tok, trm, head_rows(gk_rows), head_rows(gk_rows), head_rows(gv_rows),
                pl.BlockSpec((1, GLA_GATE_RANK, s), lambda i, j: (i, 0, 0)),
                pl.BlockSpec((gk_rows, GLA_GATE_RANK), lambda i, j: (j, 0)),
                pl.BlockSpec((gk_rows, 1), lambda i, j: (j, 0)),
                full(seg), full(level), full(gn)]
    return pl.pallas_call(
        _mixers_kernel,
        grid=(b, MIX_SPLIT),
        in_specs=in_specs,
        out_specs=(trm, head_rows(gv_rows)),
        out_shape=(jax.ShapeDtypeStruct((b, heads, MLA_V, s), BF16),
                   jax.ShapeDtypeStruct((b, GLA_DV, s), BF16)),
        scratch_shapes=[pltpu.VMEM((2, s, TILE), F32),
                        pltpu.VMEM((2, s, TILE), BF16),
                        pltpu.VMEM((hp, MLA_VX, s), BF16),
                        pltpu.VMEM((gh, GLA_HEAD_K, GLA_HEAD_V), F32),
                        pltpu.VMEM((2, GLA_NFACT, gk_rows, TILE), F32),
                        pltpu.VMEM((2, gh, GLA_BLOCK, TILE), F32)],
        compiler_params=pltpu.CompilerParams(
            dimension_semantics=("parallel", "parallel"), vmem_limit_bytes=VMEM_LIMIT),
        name="mixers",
    )(q, k, vtm, gq, gk, gv, glow, wg, bg, seg, level, gn)


def _out_kernel(x_ref, p_ref, om_ref, gm_ref, og_ref, gg_ref, ma_ref, mb_ref, wa_ref, wb_ref,
                wout_ref, wple_ref, gple_ref, ggate_ref, wgate_ref, gfin_ref, o_ref, *, final):
    def branch(o_t, g_t_ref, w_ref):
        gh = g_t_ref[0]
        zt = o_t * (gh * (jnp.tanh(gh) + 1.0))
        return _dot_tn(zt, w_ref[...])

    ya = branch(om_ref[0].reshape(MLA_WIDTH, OUT_ROWS), gm_ref, wa_ref)
    yb = branch(og_ref[0], gg_ref, wb_ref)
    merged2 = ((jnp.tanh(ma_ref[0]) + 1.0) * ya.astype(BF16)
               + (jnp.tanh(mb_ref[0]) + 1.0) * yb.astype(BF16))
    x1 = x_ref[0] + _dot(merged2, wout_ref[...])
    e_half = _rms_rows(_dot(p_ref[0].astype(BF16), wple_ref[...]), gple_ref[...])
    gate2 = jnp.tanh(_dot(_rms_rows(x1, ggate_ref[...]).astype(BF16), wgate_ref[...])) + 1.0
    x2 = x1 + gate2 * e_half
    o_ref[0] = _rms_rows(x2, gfin_ref[...]) if final else x2


def _output(x, p, om, gm, og, gg, ma, mb, wa, wb, wout, wple, gple, ggate, wgate, gfin, final):
    b, s, d = x.shape
    rows = OUT_ROWS

    def full(a):
        return pl.BlockSpec(a.shape, lambda i, j: (0,) * a.ndim)

    def tok(width):
        return pl.BlockSpec((1, rows, width), lambda i, j: (i, j, 0))

    def tr(n):
        return pl.BlockSpec((1, n, rows), lambda i, j: (i, 0, j))

    heads_tr = pl.BlockSpec((1, MLA_HEADS, MLA_V, rows), lambda i, j: (i, 0, 0, j))
    in_specs = [tok(d), tok(p.shape[-1]), heads_tr, tr(MLA_WIDTH), tr(GLA_DV), tr(GLA_DV),
                tok(d), tok(d), full(wa), full(wb), full(wout), full(wple), full(gple), full(ggate),
                full(wgate), full(gfin)]
    return pl.pallas_call(
        functools.partial(_out_kernel, final=final),
        grid=(b, s // rows),
        in_specs=in_specs,
        out_specs=tok(d),
        out_shape=jax.ShapeDtypeStruct((b, s, d), F32),
        compiler_params=pltpu.CompilerParams(
            dimension_semantics=("parallel", "parallel"), vmem_limit_bytes=VMEM_LIMIT),
        name="out",
    )(x, p, om, gm, og, gg, ma, mb, wa, wb, wout, wple, gple, ggate, wgate, gfin)


def _swap_halves(w):
    half = w.shape[-1] // 2
    return jnp.concatenate([-w[..., half:], w[..., :half]], axis=-1)


def _rope_kernel(pos_ref, freq_ref, phase_ref, o_ref):
    pos = pos_ref[0].astype(F32)
    rows = pos.shape[0]
    slot = lax.broadcasted_iota(jnp.int32, (rows, ROPE_PACK * MLA_ROPE), 1) // MLA_ROPE
    posx = pos[:, 0:1]
    for t in range(1, ROPE_PACK):
        posx = jnp.where(slot == t, pos[:, t:t + 1], posx)
    dense = jnp.cos(posx * freq_ref[...] - phase_ref[...])
    for t in range(ROPE_PACK):
        o_ref[0, t * rows:(t + 1) * rows, :] = dense[:, t * MLA_ROPE:(t + 1) * MLA_ROPE]


def _rope_table(positions):
    b, s = positions.shape
    inv_freq = 1.0 / (ROPE_THETA ** (jnp.arange(0, MLA_ROPE, 2, dtype=F32) / MLA_ROPE))
    half = MLA_ROPE // 2
    freq = jnp.tile(jnp.concatenate([inv_freq, inv_freq]), ROPE_PACK)[None, :]
    phase = jnp.tile(jnp.concatenate([jnp.zeros((half,), F32), jnp.full((half,), np.pi / 2, F32)]),
                     ROPE_PACK)[None, :]
    rows = s // ROPE_PACK
    return pl.pallas_call(
        _rope_kernel,
        grid=(b,),
        in_specs=[pl.BlockSpec((1, rows, ROPE_PACK), lambda i: (i, 0, 0)),
                  pl.BlockSpec((1, ROPE_PACK * MLA_ROPE), lambda i: (0, 0)),
                  pl.BlockSpec((1, ROPE_PACK * MLA_ROPE), lambda i: (0, 0))],
        out_specs=pl.BlockSpec((1, s, MLA_ROPE), lambda i: (i, 0, 0)),
        out_shape=jax.ShapeDtypeStruct((b, s, MLA_ROPE), F32),
        compiler_params=pltpu.CompilerParams(dimension_semantics=("parallel",)),
        name="rope",
    )(positions.reshape(b, ROPE_PACK, rows).transpose(0, 2, 1), freq, phase)


def kernel(x, p, positions, norm_in_g, w_in, q_norm_g, w_uq, kv_norm_g, w_ukv, w_gk_up, b_gk,
           gla_norm_g, w_mla_br, w_gla_br, w_out, w_ple, ple_norm_g, ple_gate_norm_g, w_ple_gate,
           final_norm_g):
    b, s, d = x.shape
    depth = w_in.shape[0]
    assert s % PROJ_ROWS == 0 and s % OUT_ROWS == 0 and s % TILE == 0
    cs = _rope_table(positions)
    seg_np, level_np = _gla_constants()
    seg = jnp.asarray(seg_np, BF16)
    level = jnp.asarray(level_np)

    sizes = (MLA_Q_RANK, MLA_KV_RANK, MLA_ROPE, MLA_WIDTH, GLA_DK, GLA_DK, GLA_DV, GLA_GATE_RANK,
             GLA_DV, d, d)
    offs = np.concatenate([[0], np.cumsum(sizes)])

    def cols(w, idx):
        return w[:, int(offs[idx]):int(offs[idx + 1])]

    for l in range(depth):
        w = w_in[l].astype(BF16)
        wkr = cols(w, 2)
        wstd = jnp.concatenate(
            [cols(w, 0), cols(w, 1), jnp.zeros((d, MLA_NOPE), BF16), wkr, _swap_halves(wkr),
             cols(w, 9) * 0.5, cols(w, 10) * 0.5], axis=1)
        wt = jnp.concatenate(
            [cols(w, 3) * 0.5, cols(w, 4) * (GLA_HEAD_K ** -0.5), cols(w, 5), cols(w, 6),
             cols(w, 8) * 0.5, cols(w, 7)], axis=1).T
        uq = w_uq[l].reshape(MLA_Q_RANK, MLA_HEADS, MLA_NOPE + MLA_ROPE)
        uq_r = uq[..., MLA_NOPE:]
        wq = jnp.concatenate([uq[..., :MLA_NOPE], uq_r, _swap_halves(uq_r)], axis=-1)
        wq = wq.reshape(MLA_Q_RANK, MLA_HEADS * HEAD_LANES).astype(BF16)
        ukv = w_ukv[l].reshape(MLA_KV_RANK, MLA_HEADS, MLA_NOPE + MLA_V)
        wk = jnp.concatenate([ukv[..., :MLA_NOPE], jnp.zeros_like(ukv[..., :MLA_NOPE])], axis=-1)
        wk = wk.reshape(MLA_KV_RANK, MLA_HEADS * HEAD_LANES).astype(BF16)
        wvt = ukv[..., MLA_NOPE:].reshape(MLA_KV_RANK, MLA_WIDTH).T.astype(BF16)

        (q, k, vt, gmla, gq, gk, gv, ggla, glow, ma, mb) = _projections(
            x, cs, norm_in_g[l][None, :], wstd, wt, q_norm_g[l][None, :], wq,
            kv_norm_g[l][None, :], wk, wvt)

        o_mla, o_gla = _mixers(q, k, vt, gq, gk, gv, glow, w_gk_up[l].T.astype(BF16), b_gk[l][:, None],
                               seg, level, gla_norm_g[l][:, None])

        x = _output(x, p[l], o_mla, gmla, o_gla, ggla, ma, mb, w_mla_br[l].astype(BF16),
                    w_gla_br[l].astype(BF16), (w_out[l] * 0.5).astype(BF16), w_ple[l].astype(BF16),
                    ple_norm_g[l][None, :] * 0.5, ple_gate_norm_g[l][None, :],
                    (w_ple_gate[l] * 0.5).astype(BF16),
                    final_norm_g[None, :], l == depth - 1)
    return x
```

```python
import functools

import numpy as np
import jax
import jax.numpy as jnp
from jax import lax
from jax.experimental import pallas as pl
from jax.experimental.pallas import tpu as pltpu

F32 = jnp.float32
BF16 = jnp.bfloat16

MLA_HEADS = 8
MLA_Q_RANK = 384
MLA_KV_RANK = 256
MLA_NOPE = 64
MLA_ROPE = 32
MLA_V = 64
MLA_WIDTH = MLA_HEADS * MLA_V
ROPE_THETA = 10000.0
GLA_HEADS = 4
GLA_HEAD_K = 64
GLA_HEAD_V = 128
GLA_DK = GLA_HEADS * GLA_HEAD_K
GLA_DV = GLA_HEADS * GLA_HEAD_V
GLA_GATE_RANK = 16
GLA_GATE_NORMALIZER = 16.0
EPS = 1e-6
LOG2E = 1.4426950408889634

HEAD_LANES = 128
TILE = 256
PROJ_ROWS = 512
OUT_ROWS = 1024
MIX_SPLIT = 2
MLA_VX = MLA_V + 16
ROPE_PACK = 128 // MLA_ROPE
VMEM_LIMIT = 56 * 1024 * 1024

GLA_BLOCK = 16
GLA_PARTS = 2
HALF = TILE // 2
GLA_LEVELS = (HALF, 64, 32, 16)
GLA_NFACT = 2 + len(GLA_LEVELS)
GLA_BAND = len(GLA_LEVELS)

NT = (((1,), (1,)), ((), ()))
TN = (((0,), (0,)), ((), ()))


def _dot(a, b):
    return jnp.dot(a, b, preferred_element_type=F32)


def _dot_nt(a, b):
    return lax.dot_general(a, b, NT, preferred_element_type=F32)


def _dot_tn(a, b):
    return lax.dot_general(a, b, TN, preferred_element_type=F32)


def _rms_rows(x, g):
    return x * lax.rsqrt(jnp.mean(x * x, axis=-1, keepdims=True) + EPS) * g


def _proj_kernel(x_ref, cs_ref, gin_ref, wstd_ref, wt_ref, gq_ref, wq_ref,
                 gkv_ref, wk_ref, wvt_ref,
                 q_out, k_out, vt_out, gmla_out, gq_out, gk_out, gv_out, ggla_out, glow_out,
                 ma_out, mb_out):
    h = _rms_rows(x_ref[0], gin_ref[...]).astype(BF16)

    ma_out[0] = _dot(h, wstd_ref[:, 768:1792]).astype(BF16)
    mb_out[0] = _dot(h, wstd_ref[:, 1792:2816]).astype(BF16)
    ht = _dot_nt(wt_ref[...], h)
    for out, lo, hi in ((gmla_out, 0, 512), (gq_out, 512, 768), (gk_out, 768, 1024),
                        (gv_out, 1024, 1536), (ggla_out, 1536, 2048), (glow_out, 2048, 2064)):
        out[0] = ht[lo:hi, :].astype(BF16)

    cos, sin = cs_ref[0, :, 0:MLA_ROPE // 2], cs_ref[0, :, MLA_ROPE // 2:MLA_ROPE]
    table = jnp.concatenate([jnp.ones((PROJ_ROWS, MLA_NOPE), F32), cos, cos, sin, sin], axis=1)

    cq = _rms_rows(_dot(h, wstd_ref[:, 0:384]), gq_ref[...]).astype(BF16)
    q = _dot(cq, wq_ref[...])
    qscale = table * ((MLA_NOPE + MLA_ROPE) ** -0.5 * LOG2E)
    for hh in range(MLA_HEADS):
        q_out[0, hh] = (q[:, hh * HEAD_LANES:(hh + 1) * HEAD_LANES] * qscale).astype(BF16)

    ckv = _rms_rows(_dot(h, wstd_ref[:, 384:640]), gkv_ref[...]).astype(BF16)
    u = _dot(h, wstd_ref[:, 640:768]) * table
    lane = lax.broadcasted_iota(jnp.int32, u.shape, 1)
    krot = jnp.where(lane >= MLA_NOPE, u + pltpu.roll(u, 32, 1) + pltpu.roll(u, 96, 1), 0.0)
    kk = _dot(ckv, wk_ref[...])
    for hh in range(MLA_HEADS):
        k_out[0, hh] = (kk[:, hh * HEAD_LANES:(hh + 1) * HEAD_LANES] + krot).astype(BF16)

    vt = _dot_nt(wvt_ref[...], ckv)
    for hh in range(MLA_HEADS):
        vt_out[0, hh] = vt[hh * MLA_V:(hh + 1) * MLA_V, :].astype(BF16)


def _projections(x, cs, gin, wstd, wt, gq, wq, gkv, wk, wvt):
    b, s, d = x.shape
    rows = PROJ_ROWS

    def full(a):
        return pl.BlockSpec(a.shape, lambda i, j: (0,) * a.ndim)

    def tok(width):
        return pl.BlockSpec((1, rows, width), lambda i, j: (i, j, 0))

    def tr(n):
        return pl.BlockSpec((1, n, rows), lambda i, j: (i, 0, j))

    def tr_shape(n):
        return jax.ShapeDtypeStruct((b, n, s), BF16)

    heads_tok = pl.BlockSpec((1, MLA_HEADS, rows, HEAD_LANES), lambda i, j: (i, 0, j, 0))
    heads_tr = pl.BlockSpec((1, MLA_HEADS, MLA_V, rows), lambda i, j: (i, 0, 0, j))
    out_shape = (
        jax.ShapeDtypeStruct((b, MLA_HEADS, s, HEAD_LANES), BF16),
        jax.ShapeDtypeStruct((b, MLA_HEADS, s, HEAD_LANES), BF16),
        jax.ShapeDtypeStruct((b, MLA_HEADS, MLA_V, s), BF16),
        tr_shape(MLA_WIDTH),
        tr_shape(GLA_DK),
        tr_shape(GLA_DK),
        tr_shape(GLA_DV),
        tr_shape(GLA_DV),
        tr_shape(GLA_GATE_RANK),
        jax.ShapeDtypeStruct((b, s, d), BF16),
        jax.ShapeDtypeStruct((b, s, d), BF16),
    )
    out_specs = (heads_tok, heads_tok, heads_tr, tr(MLA_WIDTH),
                 tr(GLA_DK), tr(GLA_DK), tr(GLA_DV), tr(GLA_DV), tr(GLA_GATE_RANK), tok(d), tok(d))
    in_specs = [tok(d), tok(MLA_ROPE), full(gin), full(wstd), full(wt), full(gq),
                full(wq), full(gkv), full(wk), full(wvt)]
    return pl.pallas_call(
        _proj_kernel,
        grid=(b, s // rows),
        in_specs=in_specs,
        out_specs=out_specs,
        out_shape=out_shape,
        compiler_params=pltpu.CompilerParams(
            dimension_semantics=("parallel", "parallel"), vmem_limit_bytes=VMEM_LIMIT),
        name="proj",
    )(x, cs, gin, wstd, wt, gq, wq, gkv, wk, wvt)


def _mla_stages(q_ref, k_ref, vt_ref, o_ref, s_ref, p_ref, vx_ref):
    heads, seq = q_ref.shape[1], q_ref.shape[2]
    row = lax.broadcasted_iota(jnp.int32, (TILE, TILE), 0)
    col = lax.broadcasted_iota(jnp.int32, (TILE, TILE), 1)
    causal = row <= col
    tasks = [(hh, qi) for hh in range(heads) for qi in range(seq // TILE)]

    for hh in range(heads):
        vx_ref[hh, 0:MLA_V, :] = vt_ref[0, hh]
        vx_ref[hh, MLA_V:, :] = jnp.ones((MLA_VX - MLA_V, seq), BF16)

    def scores(t):
        hh, qi = tasks[t]
        lo, hi = qi * TILE, (qi + 1) * TILE
        qt = q_ref[0, hh, lo:hi, :]
        if qi:
            s_ref[t % 2, 0:lo, :] = _dot_nt(k_ref[0, hh, 0:lo, :], qt)
        s_ref[t % 2, lo:hi, :] = jnp.where(causal, _dot_nt(k_ref[0, hh, lo:hi, :], qt), -jnp.inf)

    def softmax(t):
        _, qi = tasks[t]
        m = jnp.full((1, TILE), -jnp.inf, F32)
        for kj in range(qi + 1):
            yield 0, 18
            m = jnp.maximum(m, jnp.max(s_ref[t % 2, kj * TILE:(kj + 1) * TILE, :], axis=0, keepdims=True))
        for kj in range(qi + 1):
            yield 0, 45
            blk = slice(kj * TILE, (kj + 1) * TILE)
            p_ref[t % 2, blk, :] = jnp.exp2(s_ref[t % 2, blk, :] - m).astype(BF16)

    def weighted_values(t):
        hh, qi = tasks[t]
        lo, hi = qi * TILE, (qi + 1) * TILE
        acc = _dot(vx_ref[hh, :, 0:hi], p_ref[t % 2, 0:hi, :])
        o_ref[0, hh, :, lo:hi] = (acc[0:MLA_V] / acc[MLA_V:MLA_V + 1]).astype(BF16)

    def program():
        blocks = [qi + 1 for _, qi in tasks]
        yield 64 * blocks[0], 20 * blocks[0]
        scores(0)
        for t in range(len(tasks)):
            if t + 1 < len(tasks):
                yield 64 * blocks[t + 1], 20 * blocks[t + 1]
                scores(t + 1)
            if t:
                yield 32 * blocks[t - 1], 10
                weighted_values(t - 1)
            yield from softmax(t)
        yield 32 * blocks[-1], 10
        weighted_values(len(tasks) - 1)

    blocks = sum(qi + 1 for _, qi in tasks)
    return program(), (96.0 * blocks, 83.0 * blocks + 10.0 * len(tasks))


def _gla_constants():
    u = np.arange(TILE)[:, None]
    t = np.arange(TILE)[None, :]
    mats = [(u <= t), (u > t)]
    for m in GLA_LEVELS:
        pos = t % (2 * m)
        mid = t - pos + m - 1
        upper = pos >= m
        mats.append(np.where(upper, (u > mid) & (u <= t), (u > t) & (u <= mid)))
    seg = np.concatenate([mm.astype(np.float32) for mm in mats], axis=1)
    seg = np.concatenate([seg] * GLA_PARTS, axis=0)
    j = np.arange(TILE)[:, None]
    i = np.arange(TILE)[None, :]
    level = np.full((TILE, TILE), -1, np.int32)
    level[(j // GLA_BLOCK) == (i // GLA_BLOCK)] = GLA_BAND
    for idx, m in enumerate(GLA_LEVELS):
        same = (j // (2 * m)) == (i // (2 * m))
        split = ((j % (2 * m)) < m) & ((i % (2 * m)) >= m)
        level[same & split & (j < i)] = idx
    return seg, level[:HALF, :HALF]


def _gla_band(qf, kf, decay, band_ref):
    heads = qf.shape[0] // GLA_HEAD_K
    rel = (lax.broadcasted_iota(jnp.int32, (GLA_BLOCK, TILE), 1) % GLA_BLOCK
           - lax.broadcasted_iota(jnp.int32, (GLA_BLOCK, TILE), 0))
    w = [kf]
    span = decay
    step = 1
    while step < GLA_BLOCK:
        yield 0, 4 * heads * (step + 1)
        w += [span * pltpu.roll(w[d], step, 1) for d in range(step)]
        span = span * pltpu.roll(span, step, 1)
        step *= 2
    bands = [jnp.zeros((GLA_BLOCK, TILE), F32) for _ in range(heads)]
    for d in range(GLA_BLOCK):
        yield 0, 12 * heads
        prod = qf * w[d]
        for hh in range(heads):
            diag = jnp.sum(prod[hh * GLA_HEAD_K:(hh + 1) * GLA_HEAD_K], axis=0, keepdims=True)
            bands[hh] = jnp.where(rel == d, diag, bands[hh])
    for hh in range(heads):
        band_ref[hh] = bands[hh]


def _gla_stages(qt_ref, kt_ref, vt_ref, glow_ref, wg_ref, bg_ref, seg_ref, level_ref, gn_ref,
                o_ref, state_ref, fact_ref, band_ref):
    heads = qt_ref.shape[1] // GLA_HEAD_K
    state_ref[...] = jnp.zeros_like(state_ref)
    level = level_ref[...]
    lane = lax.broadcasted_iota(jnp.int32, (GLA_HEAD_K, TILE), 1)

    def prepare(tile):
        ts = slice(tile * TILE, (tile + 1) * TILE)
        slot = tile % 2
        yield 8, 50 * heads
        z = _dot(wg_ref[...], glow_ref[0, :, ts]) + bg_ref[...]
        g = -(jnp.maximum(-z, 0.0) + jnp.log1p(jnp.exp(-jnp.abs(z)))) * (1.0 / GLA_GATE_NORMALIZER)
        parts, rest = [], g
        for _ in range(GLA_PARTS):
            parts.append(rest.astype(BF16))
            rest = rest - parts[-1].astype(F32)
        g3 = jnp.concatenate(parts, axis=1)
        for n in range(GLA_NFACT):
            yield 8 * GLA_PARTS * heads, 6 * heads
            fact_ref[slot, n] = jnp.exp(_dot(g3, seg_ref[:, n * TILE:(n + 1) * TILE]))
        yield from _gla_band(qt_ref[0, :, ts].astype(F32), kt_ref[0, :, ts].astype(F32), jnp.exp(g),
                             band_ref.at[slot])

    def attend(tile, hh):
        ts = slice(tile * TILE, (tile + 1) * TILE)
        slot = tile % 2
        rk = slice(hh * GLA_HEAD_K, (hh + 1) * GLA_HEAD_K)
        rv = slice(hh * GLA_HEAD_V, (hh + 1) * GLA_HEAD_V)
        yield 20, 25
        qf = qt_ref[0, rk, ts].astype(F32)
        kf = kt_ref[0, rk, ts].astype(F32)
        vt = vt_ref[0, rv, ts]
        f_in = fact_ref[slot, 0, rk, :]
        q_in = (qf * f_in).astype(BF16)
        k_out = (kf * fact_ref[slot, 1, rk, :]).astype(BF16)
        state = state_ref[hh]
        o = _dot_tn(state.astype(BF16), q_in)

        far = []
        for idx, m in enumerate(GLA_LEVELS):
            yield 64, 20
            f = fact_ref[slot, 2 + idx, rk, :]
            upper = (lane // m) % 2 == 1
            q_up = jnp.where(upper, qf * f, 0.0).astype(BF16)
            k_lo = jnp.where(upper, 0.0, kf * f).astype(BF16)
            far.append(_dot_tn(k_lo, q_up))
        yield 32, 60
        band = jnp.tile(band_ref[slot, hh], (HALF // GLA_BLOCK, 1))

        def diagonal(c):
            sl = slice(c * HALF, (c + 1) * HALF)
            blk = far[1][sl, sl]
            for idx in range(2, len(GLA_LEVELS)):
                blk = jnp.where(level == idx, far[idx][sl, sl], blk)
            return jnp.where(level == GLA_BAND, band[:, sl], blk)

        at = jnp.concatenate(
            [jnp.concatenate([diagonal(0), far[0][0:HALF, HALF:]], axis=1),
             jnp.concatenate([jnp.zeros((HALF, HALF), F32), diagonal(1)], axis=1)], axis=0)
        o = o + _dot(vt, at.astype(BF16))

        yield 16, 45
        ms = jnp.mean(o * o, axis=0, keepdims=True)
        o_ref[0, rv, ts] = (o * lax.rsqrt(ms + EPS) * gn_ref[...]).astype(BF16)

        state_ref[hh] = state * f_in[:, TILE - 1:TILE] + _dot_nt(k_out, vt)

    def program():
        tiles = qt_ref.shape[2] // TILE
        yield from prepare(0)
        for tile in range(tiles):
            if tile + 1 < tiles:
                yield from prepare(tile + 1)
            for hh in range(heads):
                yield from attend(tile, hh)

    tiles = qt_ref.shape[2] // TILE
    mxu = 8 + 8 * GLA_PARTS * heads * GLA_NFACT + heads * (20 + 64 * len(GLA_LEVELS) + 32 + 16)
    vpu = 50 * heads + 6 * heads * GLA_NFACT + 4 * heads * (GLA_BLOCK - 1 + GLA_BLOCK.bit_length() - 1) + 12 * heads * GLA_BLOCK \
        + heads * (25 + 20 * len(GLA_LEVELS) + 60 + 45)
    return program(), (float(tiles * mxu), float(tiles * vpu))


def _mixers_kernel(q_ref, k_ref, vtm_ref, gq_ref, gk_ref, gv_ref, glow_ref, wg_ref, bg_ref, seg_ref,
                   level_ref, gn_ref, om_ref, og_ref,
                   s_ref, p_ref, vx_ref, state_ref, fact_ref, band_ref):
    mla, mla_totals = _mla_stages(q_ref, k_ref, vtm_ref, om_ref, s_ref, p_ref, vx_ref)
    gla, gla_totals = _gla_stages(gq_ref, gk_ref, gv_ref, glow_ref, wg_ref, bg_ref, seg_ref, level_ref,
                                  gn_ref, og_ref, state_ref, fact_ref, band_ref)
    _interleave([mla, gla], [mla_totals, gla_totals])


def _interleave(programs, totals):
    all_mxu = sum(t[0] for t in totals)
    all_vpu = sum(t[1] for t in totals)
    pending = [next(p) for p in programs]
    spent = [0.0] * len(programs)
    mxu = vpu = 0.0
    live = set(range(len(programs)))
    while live:
        def cost(i):
            m, v = pending[i]
            skew = abs((mxu + m) / all_mxu - (vpu + v) / all_vpu)
            lag = spent[i] / sum(totals[i]) - min(spent[j] / sum(totals[j]) for j in live)
            return skew + lag
        i = min(live, key=cost)
        m, v = pending[i]
        mxu, vpu, spent[i] = mxu + m, vpu + v, spent[i] + m + v
        try:
            pending[i] = next(programs[i])
        except StopIteration:
            live.discard(i)


def _mixers(q, k, vtm, gq, gk, gv, glow, wg, bg, seg, level, gn):
    b, heads, s, _ = q.shape
    hp = heads // MIX_SPLIT
    gh = GLA_HEADS // MIX_SPLIT
    gk_rows, gv_rows = gh * GLA_HEAD_K, gh * GLA_HEAD_V

    def full(a):
        return pl.BlockSpec(a.shape, lambda i, j: (0,) * a.ndim)

    def head_rows(n):
        return pl.BlockSpec((1, n, s), lambda i, j: (i, j, 0))

    tok = pl.BlockSpec((1, hp, s, HEAD_LANES), lambda i, j: (i, j, 0, 0))
    trm = pl.BlockSpec((1, hp, MLA_V, s), lambda i, j: (i, j, 0, 0))
    in_specs = [tok, tok, trm, head_rows(gk_rows), head_rows(gk_rows), head_rows(gv_rows),
                pl.BlockSpec((1, GLA_GATE_RANK, s), lambda i, j: (i, 0, 0)),
                pl.BlockSpec((gk_rows, GLA_GATE_RANK), lambda i, j: (j, 0)),
                pl.BlockSpec((gk_rows, 1), lambda i, j: (j, 0)),
                full(seg), full(level), full(gn)]
    return pl.pallas_call(
        _mixers_kernel,
        grid=(b, MIX_SPLIT),
        in_specs=in_specs,
        out_specs=(trm, head_rows(gv_rows)),
        out_shape=(jax.ShapeDtypeStruct((b, heads, MLA_V, s), BF16),
                   jax.ShapeDtypeStruct((b, GLA_DV, s), BF16)),
        scratch_shapes=[pltpu.VMEM((2, s, TILE), F32),
                        pltpu.VMEM((2, s, TILE), BF16),
                        pltpu.VMEM((hp, MLA_VX, s), BF16),
                        pltpu.VMEM((gh, GLA_HEAD_K, GLA_HEAD_V), F32),
                        pltpu.VMEM((2, GLA_NFACT, gk_rows, TILE), F32),
                        pltpu.VMEM((2, gh, GLA_BLOCK, TILE), F32)],
        compiler_params=pltpu.CompilerParams(
            dimension_semantics=("parallel", "parallel"), vmem_limit_bytes=VMEM_LIMIT),
        name="mixers",
    )(q, k, vtm, gq, gk, gv, glow, wg, bg, seg, level, gn)


def _out_kernel(x_ref, p_ref, om_ref, gm_ref, og_ref, gg_ref, ma_ref, mb_ref, wa_ref, wb_ref,
                wout_ref, wple_ref, gple_ref, ggate_ref, wgate_ref, gfin_ref, o_ref, *, final):
    def chunk(r):
        n = r.stop - r.start

        def branch(o_t, gh, w_ref):
            zt = o_t * (gh * (jnp.tanh(gh) + 1.0))
            return _dot_tn(zt, w_ref[...])

        ya = branch(om_ref[0, :, :, r].reshape(MLA_WIDTH, n), gm_ref[0, :, r], wa_ref)
        yield
        yb = branch(og_ref[0, :, r], gg_ref[0, :, r], wb_ref)
        yield
        merged2 = ((jnp.tanh(ma_ref[0, r, :]) + 1.0) * ya.astype(BF16)
                   + (jnp.tanh(mb_ref[0, r, :]) + 1.0) * yb.astype(BF16))
        x1 = x_ref[0, r, :] + _dot(merged2, wout_ref[...])
        yield
        e_half = _rms_rows(_dot(p_ref[0, r, :].astype(BF16), wple_ref[...]), gple_ref[...])
        yield
        gate2 = jnp.tanh(_dot(_rms_rows(x1, ggate_ref[...]).astype(BF16), wgate_ref[...])) + 1.0
        yield
        x2 = x1 + gate2 * e_half
        o_ref[0, r, :] = _rms_rows(x2, gfin_ref[...]) if final else x2

    half = OUT_ROWS // 2
    chunks = [chunk(slice(0, half)), chunk(slice(half, OUT_ROWS))]
    while chunks:
        for c in list(chunks):
            if next(c, c) is c:
                chunks.remove(c)


def _output(x, p, om, gm, og, gg, ma, mb, wa, wb, wout, wple, gple, ggate, wgate, gfin, final):
    b, s, d = x.shape
    rows = OUT_ROWS

    def full(a):
        return pl.BlockSpec(a.shape, lambda i, j: (0,) * a.ndim)

    def tok(width):
        return pl.BlockSpec((1, rows, width), lambda i, j: (i, j, 0))

    def tr(n):
        return pl.BlockSpec((1, n, rows), lambda i, j: (i, 0, j))

    heads_tr = pl.BlockSpec((1, MLA_HEADS, MLA_V, rows), lambda i, j: (i, 0, 0, j))
    in_specs = [tok(d), tok(p.shape[-1]), heads_tr, tr(MLA_WIDTH), tr(GLA_DV), tr(GLA_DV),
                tok(d), tok(d), full(wa), full(wb), full(wout), full(wple), full(gple), full(ggate),
                full(wgate), full(gfin)]
    return pl.pallas_call(
        functools.partial(_out_kernel, final=final),
        grid=(b, s // rows),
        in_specs=in_specs,
        out_specs=tok(d),
        out_shape=jax.ShapeDtypeStruct((b, s, d), F32),
        compiler_params=pltpu.CompilerParams(
            dimension_semantics=("parallel", "parallel"), vmem_limit_bytes=VMEM_LIMIT),
        name="out",
    )(x, p, om, gm, og, gg, ma, mb, wa, wb, wout, wple, gple, ggate, wgate, gfin)


def _swap_halves(w):
    half = w.shape[-1] // 2
    return jnp.concatenate([-w[..., half:], w[..., :half]], axis=-1)


def _rope_kernel(pos_ref, freq_ref, phase_ref, o_ref):
    pos = pos_ref[0].astype(F32)
    rows = pos.shape[0]
    slot = lax.broadcasted_iota(jnp.int32, (rows, ROPE_PACK * MLA_ROPE), 1) // MLA_ROPE
    posx = pos[:, 0:1]
    for t in range(1, ROPE_PACK):
        posx = jnp.where(slot == t, pos[:, t:t + 1], posx)
    dense = jnp.cos(posx * freq_ref[...] - phase_ref[...])
    for t in range(ROPE_PACK):
        o_ref[0, t * rows:(t + 1) * rows, :] = dense[:, t * MLA_ROPE:(t + 1) * MLA_ROPE]


def _rope_table(positions):
    b, s = positions.shape
    inv_freq = 1.0 / (ROPE_THETA ** (jnp.arange(0, MLA_ROPE, 2, dtype=F32) / MLA_ROPE))
    half = MLA_ROPE // 2
    freq = jnp.tile(jnp.concatenate([inv_freq, inv_freq]), ROPE_PACK)[None, :]
    phase = jnp.tile(jnp.concatenate([jnp.zeros((half,), F32), jnp.full((half,), np.pi / 2, F32)]),
                     ROPE_PACK)[None, :]
    rows = s // ROPE_PACK
    return pl.pallas_call(
        _rope_kernel,
        grid=(b,),
        in_specs=[pl.BlockSpec((1, rows, ROPE_PACK), lambda i: (i, 0, 0)),
                  pl.BlockSpec((1, ROPE_PACK * MLA_ROPE), lambda i: (0, 0)),
                  pl.BlockSpec((1, ROPE_PACK * MLA_ROPE), lambda i: (0, 0))],
        out_specs=pl.BlockSpec((1, s, MLA_ROPE), lambda i: (i, 0, 0)),
        out_shape=jax.ShapeDtypeStruct((b, s, MLA_ROPE), F32),
        compiler_params=pltpu.CompilerParams(dimension_semantics=("parallel",)),
        name="rope",
    )(positions.reshape(b, ROPE_PACK, rows).transpose(0, 2, 1), freq, phase)


def kernel(x, p, positions, norm_in_g, w_in, q_norm_g, w_uq, kv_norm_g, w_ukv, w_gk_up, b_gk,
           gla_norm_g, w_mla_br, w_gla_br, w_out, w_ple, ple_norm_g, ple_gate_norm_g, w_ple_gate,
           final_norm_g):
    b, s, d = x.shape
    depth = w_in.shape[0]
    assert s % PROJ_ROWS == 0 and s % OUT_ROWS == 0 and s % TILE == 0
    cs = _rope_table(positions)
    seg_np, level_np = _gla_constants()
    seg = jnp.asarray(seg_np, BF16)
    level = jnp.asarray(level_np)

    sizes = (MLA_Q_RANK, MLA_KV_RANK, MLA_ROPE, MLA_WIDTH, GLA_DK, GLA_DK, GLA_DV, GLA_GATE_RANK,
             GLA_DV, d, d)
    offs = np.concatenate([[0], np.cumsum(sizes)])

    def cols(w, idx):
        return w[:, int(offs[idx]):int(offs[idx + 1])]

    for l in range(depth):
        w = w_in[l].astype(BF16)
        wkr = cols(w, 2)
        wstd = jnp.concatenate(
            [cols(w, 0), cols(w, 1), jnp.zeros((d, MLA_NOPE), BF16), wkr, _swap_halves(wkr),
             cols(w, 9) * 0.5, cols(w, 10) * 0.5], axis=1)
        wt = jnp.concatenate(
            [cols(w, 3) * 0.5, cols(w, 4) * (GLA_HEAD_K ** -0.5), cols(w, 5), cols(w, 6),
             cols(w, 8) * 0.5, cols(w, 7)], axis=1).T
        uq = w_uq[l].reshape(MLA_Q_RANK, MLA_HEADS, MLA_NOPE + MLA_ROPE)
        uq_r = uq[..., MLA_NOPE:]
        wq = jnp.concatenate([uq[..., :MLA_NOPE], uq_r, _swap_halves(uq_r)], axis=-1)
        wq = wq.reshape(MLA_Q_RANK, MLA_HEADS * HEAD_LANES).astype(BF16)
        ukv = w_ukv[l].reshape(MLA_KV_RANK, MLA_HEADS, MLA_NOPE + MLA_V)
        wk = jnp.concatenate([ukv[..., :MLA_NOPE], jnp.zeros_like(ukv[..., :MLA_NOPE])], axis=-1)
        wk = wk.reshape(MLA_KV_RANK, MLA_HEADS * HEAD_LANES).astype(BF16)
        wvt = ukv[..., MLA_NOPE:].reshape(MLA_KV_RANK, MLA_WIDTH).T.astype(BF16)

        (q, k, vt, gmla, gq, gk, gv, ggla, glow, ma, mb) = _projections(
            x, cs, norm_in_g[l][None, :], wstd, wt, q_norm_g[l][None, :], wq,
            kv_norm_g[l][None, :], wk, wvt)

        o_mla, o_gla = _mixers(q, k, vt, gq, gk, gv, glow, w_gk_up[l].T.astype(BF16), b_gk[l][:, None],
                               seg, level, gla_norm_g[l][:, None])

        x = _output(x, p[l], o_mla, gmla, o_gla, ggla, ma, mb, w_mla_br[l].astype(BF16),
                    w_gla_br[l].astype(BF16), (w_out[l] * 0.5).astype(BF16), w_ple[l].astype(BF16),
                    ple_norm_g[l][None, :] * 0.5, ple_gate_norm_g[l][None, :],
                    (w_ple_gate[l] * 0.5).astype(BF16),
                    final_norm_g[None, :], l == depth - 1)
    return x
```

```python
import functools

import numpy as np
import jax
import jax.numpy as jnp
from jax import lax
from jax.experimental import pallas as pl
from jax.experimental.pallas import tpu as pltpu

F32 = jnp.float32
BF16 = jnp.bfloat16

MLA_HEADS = 8
MLA_Q_RANK = 384
MLA_KV_RANK = 256
MLA_NOPE = 64
MLA_ROPE = 32
MLA_V = 64
MLA_WIDTH = MLA_HEADS * MLA_V
ROPE_THETA = 10000.0
GLA_HEADS = 4
GLA_HEAD_K = 64
GLA_HEAD_V = 128
GLA_DK = GLA_HEADS * GLA_HEAD_K
GLA_DV = GLA_HEADS * GLA_HEAD_V
GLA_GATE_RANK = 16
GLA_GATE_NORMALIZER = 16.0
EPS = 1e-6
LOG2E = 1.4426950408889634

HEAD_LANES = 128
TILE = 256
PROJ_ROWS = 512
OUT_ROWS = 1024
MIX_SPLIT = 2
MLA_VX = MLA_V + 16
ROPE_PACK = 128 // MLA_ROPE
VMEM_LIMIT = 56 * 1024 * 1024

GLA_BLOCK = 16
GLA_PARTS = 3
HALF = TILE // 2
GLA_LEVELS = (HALF, 64, 32, 16)
GLA_NFACT = 2 + len(GLA_LEVELS)
GLA_BAND = len(GLA_LEVELS)

NT = (((1,), (1,)), ((), ()))
TN = (((0,), (0,)), ((), ()))


def _dot(a, b):
    return jnp.dot(a, b, preferred_element_type=F32)


def _dot_nt(a, b):
    return lax.dot_general(a, b, NT, preferred_element_type=F32)


def _dot_tn(a, b):
    return lax.dot_general(a, b, TN, preferred_element_type=F32)


def _rms_rows(x, g):
    return x * lax.rsqrt(jnp.mean(x * x, axis=-1, keepdims=True) + EPS) * g


def _proj_kernel(x_ref, cs_ref, gin_ref, wstd_ref, wt_ref, gq_ref, wq_ref,
                 gkv_ref, wk_ref, wvt_ref,
                 q_out, k_out, vt_out, gmla_out, gq_out, gk_out, gv_out, ggla_out, glow_out,
                 ma_out, mb_out):
    h = _rms_rows(x_ref[0], gin_ref[...]).astype(BF16)

    ma_out[0] = _dot(h, wstd_ref[:, 768:1792]).astype(BF16)
    mb_out[0] = _dot(h, wstd_ref[:, 1792:2816]).astype(BF16)
    ht = _dot_nt(wt_ref[...], h)
    for out, lo, hi in ((gmla_out, 0, 512), (gq_out, 512, 768), (gk_out, 768, 1024),
                        (gv_out, 1024, 1536), (ggla_out, 1536, 2048), (glow_out, 2048, 2064)):
        out[0] = ht[lo:hi, :].astype(BF16)

    cos, sin = cs_ref[0, :, 0:MLA_ROPE // 2], cs_ref[0, :, MLA_ROPE // 2:MLA_ROPE]
    table = jnp.concatenate([jnp.ones((PROJ_ROWS, MLA_NOPE), F32), cos, cos, sin, sin], axis=1)

    cq = _rms_rows(_dot(h, wstd_ref[:, 0:384]), gq_ref[...]).astype(BF16)
    q = _dot(cq, wq_ref[...])
    qscale = table * ((MLA_NOPE + MLA_ROPE) ** -0.5 * LOG2E)
    for hh in range(MLA_HEADS):
        q_out[0, hh] = (q[:, hh * HEAD_LANES:(hh + 1) * HEAD_LANES] * qscale).astype(BF16)

    ckv = _rms_rows(_dot(h, wstd_ref[:, 384:640]), gkv_ref[...]).astype(BF16)
    u = _dot(h, wstd_ref[:, 640:768]) * table
    lane = lax.broadcasted_iota(jnp.int32, u.shape, 1)
    krot = jnp.where(lane >= MLA_NOPE, u + pltpu.roll(u, 32, 1) + pltpu.roll(u, 96, 1), 0.0)
    kk = _dot(ckv, wk_ref[...])
    for hh in range(MLA_HEADS):
        k_out[0, hh] = (kk[:, hh * HEAD_LANES:(hh + 1) * HEAD_LANES] + krot).astype(BF16)

    vt = _dot_nt(wvt_ref[...], ckv)
    for hh in range(MLA_HEADS):
        vt_out[0, hh] = vt[hh * MLA_V:(hh + 1) * MLA_V, :].astype(BF16)


def _projections(x, cs, gin, wstd, wt, gq, wq, gkv, wk, wvt):
    b, s, d = x.shape
    rows = PROJ_ROWS

    def full(a):
        return pl.BlockSpec(a.shape, lambda i, j: (0,) * a.ndim)

    def tok(width):
        return pl.BlockSpec((1, rows, width), lambda i, j: (i, j, 0))

    def tr(n):
        return pl.BlockSpec((1, n, rows), lambda i, j: (i, 0, j))

    def tr_shape(n):
        return jax.ShapeDtypeStruct((b, n, s), BF16)

    heads_tok = pl.BlockSpec((1, MLA_HEADS, rows, HEAD_LANES), lambda i, j: (i, 0, j, 0))
    heads_tr = pl.BlockSpec((1, MLA_HEADS, MLA_V, rows), lambda i, j: (i, 0, 0, j))
    out_shape = (
        jax.ShapeDtypeStruct((b, MLA_HEADS, s, HEAD_LANES), BF16),
        jax.ShapeDtypeStruct((b, MLA_HEADS, s, HEAD_LANES), BF16),
        jax.ShapeDtypeStruct((b, MLA_HEADS, MLA_V, s), BF16),
        tr_shape(MLA_WIDTH),
        tr_shape(GLA_DK),
        tr_shape(GLA_DK),
        tr_shape(GLA_DV),
        tr_shape(GLA_DV),
        tr_shape(GLA_GATE_RANK),
        jax.ShapeDtypeStruct((b, s, d), BF16),
        jax.ShapeDtypeStruct((b, s, d), BF16),
    )
    out_specs = (heads_tok, heads_tok, heads_tr, tr(MLA_WIDTH),
                 tr(GLA_DK), tr(GLA_DK), tr(GLA_DV), tr(GLA_DV), tr(GLA_GATE_RANK), tok(d), tok(d))
    in_specs = [tok(d), tok(MLA_ROPE), full(gin), full(wstd), full(wt), full(gq),
                full(wq), full(gkv), full(wk), full(wvt)]
    return pl.pallas_call(
        _proj_kernel,
        grid=(b, s // rows),
        in_specs=in_specs,
        out_specs=out_specs,
        out_shape=out_shape,
        compiler_params=pltpu.CompilerParams(
            dimension_semantics=("parallel", "parallel"), vmem_limit_bytes=VMEM_LIMIT),
        name="proj",
    )(x, cs, gin, wstd, wt, gq, wq, gkv, wk, wvt)


def _mla_stages(q_ref, k_ref, vt_ref, o_ref, s_ref, p_ref, vx_ref):
    heads, seq = q_ref.shape[1], q_ref.shape[2]
    row = lax.broadcasted_iota(jnp.int32, (TILE, TILE), 0)
    col = lax.broadcasted_iota(jnp.int32, (TILE, TILE), 1)
    causal = row <= col
    tasks = [(hh, qi) for hh in range(heads) for qi in range(seq // TILE)]

    for hh in range(heads):
        vx_ref[hh, 0:MLA_V, :] = vt_ref[0, hh]
        vx_ref[hh, MLA_V:, :] = jnp.ones((MLA_VX - MLA_V, seq), BF16)

    def scores(t):
        hh, qi = tasks[t]
        lo, hi = qi * TILE, (qi + 1) * TILE
        qt = q_ref[0, hh, lo:hi, :]
        if qi:
            s_ref[t % 2, 0:lo, :] = _dot_nt(k_ref[0, hh, 0:lo, :], qt)
        s_ref[t % 2, lo:hi, :] = jnp.where(causal, _dot_nt(k_ref[0, hh, lo:hi, :], qt), -jnp.inf)

    def softmax(t):
        _, qi = tasks[t]
        m = jnp.full((1, TILE), -jnp.inf, F32)
        for kj in range(qi + 1):
            yield 0, 18
            m = jnp.maximum(m, jnp.max(s_ref[t % 2, kj * TILE:(kj + 1) * TILE, :], axis=0, keepdims=True))
        for kj in range(qi + 1):
            yield 0, 45
            blk = slice(kj * TILE, (kj + 1) * TILE)
            p_ref[t % 2, blk, :] = jnp.exp2(s_ref[t % 2, blk, :] - m).astype(BF16)

    def weighted_values(t):
        hh, qi = tasks[t]
        lo, hi = qi * TILE, (qi + 1) * TILE
        acc = _dot(vx_ref[hh, :, 0:hi], p_ref[t % 2, 0:hi, :])
        o_ref[0, hh, :, lo:hi] = (acc[0:MLA_V] / acc[MLA_V:MLA_V + 1]).astype(BF16)

    def program():
        blocks = [qi + 1 for _, qi in tasks]
        yield 64 * blocks[0], 20 * blocks[0]
        scores(0)
        for t in range(len(tasks)):
            if t + 1 < len(tasks):
                yield 64 * blocks[t + 1], 20 * blocks[t + 1]
                scores(t + 1)
            if t:
                yield 32 * blocks[t - 1], 10
                weighted_values(t - 1)
            yield from softmax(t)
        yield 32 * blocks[-1], 10
        weighted_values(len(tasks) - 1)

    blocks = sum(qi + 1 for _, qi in tasks)
    return program(), (96.0 * blocks, 83.0 * blocks + 10.0 * len(tasks))


def _gla_constants():
    u = np.arange(TILE)[:, None]
    t = np.arange(TILE)[None, :]
    seg = np.concatenate([(u <= t).astype(np.float32)] * GLA_PARTS, axis=0)
    j = np.arange(TILE)[:, None]
    i = np.arange(TILE)[None, :]
    level = np.full((TILE, TILE), -1, np.int32)
    level[(j // GLA_BLOCK) == (i // GLA_BLOCK)] = GLA_BAND
    for idx, m in enumerate(GLA_LEVELS):
        same = (j // (2 * m)) == (i // (2 * m))
        split = ((j % (2 * m)) < m) & ((i % (2 * m)) >= m)
        level[same & split & (j < i)] = idx
    return seg, level[:HALF, :HALF]


def _gla_band(qf, kf, decay, band_ref):
    heads = qf.shape[0] // GLA_HEAD_K
    rel = (lax.broadcasted_iota(jnp.int32, (GLA_BLOCK, TILE), 1) % GLA_BLOCK
           - lax.broadcasted_iota(jnp.int32, (GLA_BLOCK, TILE), 0))
    w = [kf]
    span = decay
    step = 1
    while step < GLA_BLOCK:
        yield 0, 4 * heads * (step + 1)
        w += [span * pltpu.roll(w[d], step, 1) for d in range(step)]
        span = span * pltpu.roll(span, step, 1)
        step *= 2
    bands = [jnp.zeros((GLA_BLOCK, TILE), F32) for _ in range(heads)]
    for d in range(GLA_BLOCK):
        yield 0, 12 * heads
        prod = qf * w[d]
        for hh in range(heads):
            diag = jnp.sum(prod[hh * GLA_HEAD_K:(hh + 1) * GLA_HEAD_K], axis=0, keepdims=True)
            bands[hh] = jnp.where(rel == d, diag, bands[hh])
    for hh in range(heads):
        band_ref[hh] = bands[hh]


def _gla_stages(qt_ref, kt_ref, vt_ref, glow_ref, wg_ref, bg_ref, seg_ref, level_ref, gn_ref,
                o_ref, state_ref, fact_ref, band_ref):
    heads = qt_ref.shape[1] // GLA_HEAD_K
    state_ref[...] = jnp.zeros_like(state_ref)
    level = level_ref[...]
    lane = lax.broadcasted_iota(jnp.int32, (GLA_HEAD_K, TILE), 1)

    def prepare(tile):
        ts = slice(tile * TILE, (tile + 1) * TILE)
        slot = tile % 2
        yield 8 + 24 * heads, 60 * heads
        z = _dot(wg_ref[...], glow_ref[0, :, ts]) + bg_ref[...]
        g = -(jnp.maximum(-z, 0.0) + jnp.log1p(jnp.exp(-jnp.abs(z)))) * (1.0 / GLA_GATE_NORMALIZER)
        parts, rest = [], g
        for _ in range(GLA_PARTS):
            parts.append(rest.astype(BF16))
            rest = rest - parts[-1].astype(F32)
        csum = _dot(jnp.concatenate(parts, axis=1), seg_ref[...])
        fact_ref[slot, 0] = jnp.exp(csum)
        fact_ref[slot, 1] = jnp.exp(csum[:, TILE - 1:TILE] - csum)
        lanes = lax.broadcasted_iota(jnp.int32, csum.shape, 1)
        for idx, m in enumerate(GLA_LEVELS):
            yield 0, 8 * heads
            mid = csum[:, m - 1:m]
            for c in range(3 * m - 1, TILE, 2 * m):
                mid = jnp.where(lanes >= c - m + 1, csum[:, c:c + 1], mid)
            fact_ref[slot, 2 + idx] = jnp.exp(jnp.where((lanes // m) % 2 == 1, csum - mid, mid - csum))
        yield from _gla_band(qt_ref[0, :, ts].astype(F32), kt_ref[0, :, ts].astype(F32), jnp.exp(g),
                             band_ref.at[slot])

    def attend(tile, hh):
        ts = slice(tile * TILE, (tile + 1) * TILE)
        slot = tile % 2
        rk = slice(hh * GLA_HEAD_K, (hh + 1) * GLA_HEAD_K)
        rv = slice(hh * GLA_HEAD_V, (hh + 1) * GLA_HEAD_V)
        yield 20, 25
        qf = qt_ref[0, rk, ts].astype(F32)
        kf = kt_ref[0, rk, ts].astype(F32)
        vt = vt_ref[0, rv, ts]
        f_in = fact_ref[slot, 0, rk, :]
        q_in = (qf * f_in).astype(BF16)
        k_out = (kf * fact_ref[slot, 1, rk, :]).astype(BF16)
        state = state_ref[hh]
        o = _dot_tn(state.astype(BF16), q_in)

        far = []
        for idx, m in enumerate(GLA_LEVELS):
            yield 64, 20
            f = fact_ref[slot, 2 + idx, rk, :]
            upper = (lane // m) % 2 == 1
            q_up = jnp.where(upper, qf * f, 0.0).astype(BF16)
            k_lo = jnp.where(upper, 0.0, kf * f).astype(BF16)
            far.append(_dot_tn(k_lo, q_up))
        yield 32, 60
        band = jnp.tile(band_ref[slot, hh], (HALF // GLA_BLOCK, 1))

        def diagonal(c):
            sl = slice(c * HALF, (c + 1) * HALF)
            blk = far[1][sl, sl]
            for idx in range(2, len(GLA_LEVELS)):
                blk = jnp.where(level == idx, far[idx][sl, sl], blk)
            return jnp.where(level == GLA_BAND, band[:, sl], blk)

        at = jnp.concatenate(
            [jnp.concatenate([diagonal(0), far[0][0:HALF, HALF:]], axis=1),
             jnp.concatenate([jnp.zeros((HALF, HALF), F32), diagonal(1)], axis=1)], axis=0)
        o = o + _dot(vt, at.astype(BF16))

        yield 16, 45
        ms = jnp.mean(o * o, axis=0, keepdims=True)
        o_ref[0, rv, ts] = (o * lax.rsqrt(ms + EPS) * gn_ref[...]).astype(BF16)

        state_ref[hh] = state * f_in[:, TILE - 1:TILE] + _dot_nt(k_out, vt)

    def program():
        tiles = qt_ref.shape[2] // TILE
        yield from prepare(0)
        for tile in range(tiles):
            if tile + 1 < tiles:
                yield from prepare(tile + 1)
            for hh in range(heads):
                yield from attend(tile, hh)

    tiles = qt_ref.shape[2] // TILE
    mxu = 8 + 24 * heads + heads * (20 + 64 * len(GLA_LEVELS) + 32 + 16)
    vpu = 60 * heads + 8 * heads * len(GLA_LEVELS) + 4 * heads * (GLA_BLOCK - 1 + GLA_BLOCK.bit_length() - 1) + 12 * heads * GLA_BLOCK \
        + heads * (25 + 20 * len(GLA_LEVELS) + 60 + 45)
    return program(), (float(tiles * mxu), float(tiles * vpu))


def _mixers_kernel(q_ref, k_ref, vtm_ref, gq_ref, gk_ref, gv_ref, glow_ref, wg_ref, bg_ref, seg_ref,
                   level_ref, gn_ref, om_ref, og_ref,
                   s_ref, p_ref, vx_ref, state_ref, fact_ref, band_ref):
    mla, mla_totals = _mla_stages(q_ref, k_ref, vtm_ref, om_ref, s_ref, p_ref, vx_ref)
    gla, gla_totals = _gla_stages(gq_ref, gk_ref, gv_ref, glow_ref, wg_ref, bg_ref, seg_ref, level_ref,
                                  gn_ref, og_ref, state_ref, fact_ref, band_ref)
    _interleave([mla, gla], [mla_totals, gla_totals])


def _interleave(programs, totals):
    all_mxu = sum(t[0] for t in totals)
    all_vpu = sum(t[1] for t in totals)
    pending = [next(p) for p in programs]
    spent = [0.0] * len(programs)
    mxu = vpu = 0.0
    live = set(range(len(programs)))
    while live:
        def cost(i):
            m, v = pending[i]
            skew = abs((mxu + m) / all_mxu - (vpu + v) / all_vpu)
            lag = spent[i] / sum(totals[i]) - min(spent[j] / sum(totals[j]) for j in live)
            return skew + lag
        i = min(live, key=cost)
        m, v = pending[i]
        mxu, vpu, spent[i] = mxu + m, vpu + v, spent[i] + m + v
        try:
            pending[i] = next(programs[i])
        except StopIteration:
            live.discard(i)


def _mixers(q, k, vtm, gq, gk, gv, glow, wg, bg, seg, level, gn):
    b, heads, s, _ = q.shape
    hp = heads // MIX_SPLIT
    gh = GLA_HEADS // MIX_SPLIT
    gk_rows, gv_rows = gh * GLA_HEAD_K, gh * GLA_HEAD_V

    def full(a):
        return pl.BlockSpec(a.shape, lambda i, j: (0,) * a.ndim)

    def head_rows(n):
        return pl.BlockSpec((1, n, s), lambda i, j: (i, j, 0))

    tok = pl.BlockSpec((1, hp, s, HEAD_LANES), lambda i, j: (i, j, 0, 0))
    trm = pl.BlockSpec((1, hp, MLA_V, s), lambda i, j: (i, j, 0, 0))
    in_specs = [tok, tok, trm, head_rows(gk_rows), head_rows(gk_rows), head_rows(gv_rows),
                pl.BlockSpec((1, GLA_GATE_RANK, s), lambda i, j: (i, 0, 0)),
                pl.BlockSpec((gk_rows, GLA_GATE_RANK), lambda i, j: (j, 0)),
                pl.BlockSpec((gk_rows, 1), lambda i, j: (j, 0)),
                full(seg), full(level), full(gn)]
    return pl.pallas_call(
        _mixers_kernel,
        grid=(b, MIX_SPLIT),
        in_specs=in_specs,
        out_specs=(trm, head_rows(gv_rows)),
        out_shape=(jax.ShapeDtypeStruct((b, heads, MLA_V, s), BF16),
                   jax.ShapeDtypeStruct((b, GLA_DV, s), BF16)),
        scratch_shapes=[pltpu.VMEM((2, s, TILE), F32),
                        pltpu.VMEM((2, s, TILE), BF16),
                        pltpu.VMEM((hp, MLA_VX, s), BF16),
                        pltpu.VMEM((gh, GLA_HEAD_K, GLA_HEAD_V), F32),
                        pltpu.VMEM((2, GLA_NFACT, gk_rows, TILE), F32),
                        pltpu.VMEM((2, gh, GLA_BLOCK, TILE), F32)],
        compiler_params=pltpu.CompilerParams(
            dimension_semantics=("parallel", "parallel"), vmem_limit_bytes=VMEM_LIMIT),
        name="mixers",
    )(q, k, vtm, gq, gk, gv, glow, wg, bg, seg, level, gn)


def _out_kernel(x_ref, p_ref, om_ref, gm_ref, og_ref, gg_ref, ma_ref, mb_ref, wa_ref, wb_ref,
                wout_ref, wple_ref, gple_ref, ggate_ref, wgate_ref, gfin_ref, o_ref, *, final):
    def chunk(r):
        n = r.stop - r.start

        def branch(o_t, gh, w_ref):
            zt = o_t * (gh * (jnp.tanh(gh) + 1.0))
            return _dot_tn(zt, w_ref[...])

        ya = branch(om_ref[0, :, :, r].reshape(MLA_WIDTH, n), gm_ref[0, :, r], wa_ref)
        yield
        yb = branch(og_ref[0, :, r], gg_ref[0, :, r], wb_ref)
        yield
        merged2 = ((jnp.tanh(ma_ref[0, r, :]) + 1.0) * ya.astype(BF16)
                   + (jnp.tanh(mb_ref[0, r, :]) + 1.0) * yb.astype(BF16))
        x1 = x_ref[0, r, :] + _dot(merged2, wout_ref[...])
        yield
        e_half = _rms_rows(_dot(p_ref[0, r, :].astype(BF16), wple_ref[...]), gple_ref[...])
        yield
        gate2 = jnp.tanh(_dot(_rms_rows(x1, ggate_ref[...]).astype(BF16), wgate_ref[...])) + 1.0
        yield
        x2 = x1 + gate2 * e_half
        o_ref[0, r, :] = _rms_rows(x2, gfin_ref[...]) if final else x2

    half = OUT_ROWS // 2
    chunks = [chunk(slice(0, half)), chunk(slice(half, OUT_ROWS))]
    while chunks:
        for c in list(chunks):
            if next(c, c) is c:
                chunks.remove(c)


def _output(x, p, om, gm, og, gg, ma, mb, wa, wb, wout, wple, gple, ggate, wgate, gfin, final):
    b, s, d = x.shape
    rows = OUT_ROWS

    def full(a):
        return pl.BlockSpec(a.shape, lambda i, j: (0,) * a.ndim)

    def tok(width):
        return pl.BlockSpec((1, rows, width), lambda i, j: (i, j, 0))

    def tr(n):
        return pl.BlockSpec((1, n, rows), lambda i, j: (i, 0, j))

    heads_tr = pl.BlockSpec((1, MLA_HEADS, MLA_V, rows), lambda i, j: (i, 0, 0, j))
    in_specs = [tok(d), tok(p.shape[-1]), heads_tr, tr(MLA_WIDTH), tr(GLA_DV), tr(GLA_DV),
                tok(d), tok(d), full(wa), full(wb), full(wout), full(wple), full(gple), full(ggate),
                full(wgate), full(gfin)]
    return pl.pallas_call(
        functools.partial(_out_kernel, final=final),
        grid=(b, s // rows),
        in_specs=in_specs,
        out_specs=tok(d),
        out_shape=jax.ShapeDtypeStruct((b, s, d), F32),
        compiler_params=pltpu.CompilerParams(
            dimension_semantics=("parallel", "parallel"), vmem_limit_bytes=VMEM_LIMIT),
        name="out",
    )(x, p, om, gm, og, gg, ma, mb, wa, wb, wout, wple, gple, ggate, wgate, gfin)


def _swap_halves(w):
    half = w.shape[-1] // 2
    return jnp.concatenate([-w[..., half:], w[..., :half]], axis=-1)


def _rope_kernel(pos_ref, freq_ref, phase_ref, o_ref):
    pos = pos_ref[0].astype(F32)
    rows = pos.shape[0]
    slot = lax.broadcasted_iota(jnp.int32, (rows, ROPE_PACK * MLA_ROPE), 1) // MLA_ROPE
    posx = pos[:, 0:1]
    for t in range(1, ROPE_PACK):
        posx = jnp.where(slot == t, pos[:, t:t + 1], posx)
    dense = jnp.cos(posx * freq_ref[...] - phase_ref[...])
    for t in range(ROPE_PACK):
        o_ref[0, t * rows:(t + 1) * rows, :] = dense[:, t * MLA_ROPE:(t + 1) * MLA_ROPE]


def _rope_table(positions):
    b, s = positions.shape
    inv_freq = 1.0 / (ROPE_THETA ** (jnp.arange(0, MLA_ROPE, 2, dtype=F32) / MLA_ROPE))
    half = MLA_ROPE // 2
    freq = jnp.tile(jnp.concatenate([inv_freq, inv_freq]), ROPE_PACK)[None, :]
    phase = jnp.tile(jnp.concatenate([jnp.zeros((half,), F32), jnp.full((half,), np.pi / 2, F32)]),
                     ROPE_PACK)[None, :]
    rows = s // ROPE_PACK
    return pl.pallas_call(
        _rope_kernel,
        grid=(b,),
        in_specs=[pl.BlockSpec((1, rows, ROPE_PACK), lambda i: (i, 0, 0)),
                  pl.BlockSpec((1, ROPE_PACK * MLA_ROPE), lambda i: (0, 0)),
                  pl.BlockSpec((1, ROPE_PACK * MLA_ROPE), lambda i: (0, 0))],
        out_specs=pl.BlockSpec((1, s, MLA_ROPE), lambda i: (i, 0, 0)),
        out_shape=jax.ShapeDtypeStruct((b, s, MLA_ROPE), F32),
        compiler_params=pltpu.CompilerParams(dimension_semantics=("parallel",)),
        name="rope",
    )(positions.reshape(b, ROPE_PACK, rows).transpose(0, 2, 1), freq, phase)


def kernel(x, p, positions, norm_in_g, w_in, q_norm_g, w_uq, kv_norm_g, w_ukv, w_gk_up, b_gk,
           gla_norm_g, w_mla_br, w_gla_br, w_out, w_ple, ple_norm_g, ple_gate_norm_g, w_ple_gate,
           final_norm_g):
    b, s, d = x.shape
    depth = w_in.shape[0]
    assert s % PROJ_ROWS == 0 and s % OUT_ROWS == 0 and s % TILE == 0
    cs = _rope_table(positions)
    seg_np, level_np = _gla_constants()
    seg = jnp.asarray(seg_np, BF16)
    level = jnp.asarray(level_np)

    sizes = (MLA_Q_RANK, MLA_KV_RANK, MLA_ROPE, MLA_WIDTH, GLA_DK, GLA_DK, GLA_DV, GLA_GATE_RANK,
             GLA_DV, d, d)
    offs = np.concatenate([[0], np.cumsum(sizes)])

    def cols(w, idx):
        return w[:, int(offs[idx]):int(offs[idx + 1])]

    for l in range(depth):
        w = w_in[l].astype(BF16)
        wkr = cols(w, 2)
        wstd = jnp.concatenate(
            [cols(w, 0), cols(w, 1), jnp.zeros((d, MLA_NOPE), BF16), wkr, _swap_halves(wkr),
             cols(w, 9) * 0.5, cols(w, 10) * 0.5], axis=1)
        wt = jnp.concatenate(
            [cols(w, 3) * 0.5, cols(w, 4) * (GLA_HEAD_K ** -0.5), cols(w, 5), cols(w, 6),
             cols(w, 8) * 0.5, cols(w, 7)], axis=1).T
        uq = w_uq[l].reshape(MLA_Q_RANK, MLA_HEADS, MLA_NOPE + MLA_ROPE)
        uq_r = uq[..., MLA_NOPE:]
        wq = jnp.concatenate([uq[..., :MLA_NOPE], uq_r, _swap_halves(uq_r)], axis=-1)
        wq = wq.reshape(MLA_Q_RANK, MLA_HEADS * HEAD_LANES).astype(BF16)
        ukv = w_ukv[l].reshape(MLA_KV_RANK, MLA_HEADS, MLA_NOPE + MLA_V)
        wk = jnp.concatenate([ukv[..., :MLA_NOPE], jnp.zeros_like(ukv[..., :MLA_NOPE])], axis=-1)
        wk = wk.reshape(MLA_KV_RANK, MLA_HEADS * HEAD_LANES).astype(BF16)
        wvt = ukv[..., MLA_NOPE:].reshape(MLA_KV_RANK, MLA_WIDTH).T.astype(BF16)

        (q, k, vt, gmla, gq, gk, gv, ggla, glow, ma, mb) = _projections(
            x, cs, norm_in_g[l][None, :], wstd, wt, q_norm_g[l][None, :], wq,
            kv_norm_g[l][None, :], wk, wvt)

        o_mla, o_gla = _mixers(q, k, vt, gq, gk, gv, glow, w_gk_up[l].T.astype(BF16), b_gk[l][:, None],
                               seg, level, gla_norm_g[l][:, None])

        x = _output(x, p[l], o_mla, gmla, o_gla, ggla, ma, mb, w_mla_br[l].astype(BF16),
                    w_gla_br[l].astype(BF16), (w_out[l] * 0.5).astype(BF16), w_ple[l].astype(BF16),
                    ple_norm_g[l][None, :] * 0.5, ple_gate_norm_g[l][None, :],
                    (w_ple_gate[l] * 0.5).astype(BF16),
                    final_norm_g[None, :], l == depth - 1)
    return x
```

```python
import functools

import numpy as np
import jax
import jax.numpy as jnp
from jax import lax
from jax.experimental import pallas as pl
from jax.experimental.pallas import tpu as pltpu

F32 = jnp.float32
BF16 = jnp.bfloat16

MLA_HEADS = 8
MLA_Q_RANK = 384
MLA_KV_RANK = 256
MLA_NOPE = 64
MLA_ROPE = 32
MLA_V = 64
MLA_WIDTH = MLA_HEADS * MLA_V
ROPE_THETA = 10000.0
GLA_HEADS = 4
GLA_HEAD_K = 64
GLA_HEAD_V = 128
GLA_DK = GLA_HEADS * GLA_HEAD_K
GLA_DV = GLA_HEADS * GLA_HEAD_V
GLA_GATE_RANK = 16
GLA_GATE_NORMALIZER = 16.0
EPS = 1e-6
LOG2E = 1.4426950408889634

HEAD_LANES = 128
TILE = 256
PROJ_ROWS = 512
OUT_ROWS = 1024
MIX_SPLIT = 2
MLA_VX = MLA_V + 16
ROPE_PACK = 128 // MLA_ROPE
VMEM_LIMIT = 56 * 1024 * 1024

GLA_BLOCK = 16
GLA_PARTS = 3
HALF = TILE // 2
GLA_LEVELS = (HALF, 64, 32, 16)
GLA_NFACT = 2 + len(GLA_LEVELS)
GLA_BAND = len(GLA_LEVELS)

NT = (((1,), (1,)), ((), ()))
TN = (((0,), (0,)), ((), ()))


def _dot(a, b):
    return jnp.dot(a, b, preferred_element_type=F32)


def _dot_nt(a, b):
    return lax.dot_general(a, b, NT, preferred_element_type=F32)


def _dot_tn(a, b):
    return lax.dot_general(a, b, TN, preferred_element_type=F32)


def _rms_rows(x, g):
    return x * lax.rsqrt(jnp.mean(x * x, axis=-1, keepdims=True) + EPS) * g


def _proj_kernel(x_ref, cs_ref, gin_ref, wstd_ref, wt_ref, gq_ref, wq_ref,
                 gkv_ref, wk_ref, wvt_ref,
                 q_out, k_out, vt_out, gmla_out, gq_out, gk_out, gv_out, ggla_out, glow_out,
                 ma_out, mb_out):
    h = _rms_rows(x_ref[0], gin_ref[...]).astype(BF16)

    ma_out[0] = _dot(h, wstd_ref[:, 768:1792]).astype(BF16)
    mb_out[0] = _dot(h, wstd_ref[:, 1792:2816]).astype(BF16)
    ht = _dot_nt(wt_ref[...], h)
    for out, lo, hi in ((gmla_out, 0, 512), (gq_out, 512, 768), (gk_out, 768, 1024),
                        (gv_out, 1024, 1536), (ggla_out, 1536, 2048), (glow_out, 2048, 2064)):
        out[0] = ht[lo:hi, :].astype(BF16)

    cos, sin = cs_ref[0, :, 0:MLA_ROPE // 2], cs_ref[0, :, MLA_ROPE // 2:MLA_ROPE]
    table = jnp.concatenate([jnp.ones((PROJ_ROWS, MLA_NOPE), F32), cos, cos, sin, sin], axis=1)

    cq = _rms_rows(_dot(h, wstd_ref[:, 0:384]), gq_ref[...]).astype(BF16)
    q = _dot(cq, wq_ref[...])
    qscale = table * ((MLA_NOPE + MLA_ROPE) ** -0.5 * LOG2E)
    for hh in range(MLA_HEADS):
        q_out[0, hh] = (q[:, hh * HEAD_LANES:(hh + 1) * HEAD_LANES] * qscale).astype(BF16)

    ckv = _rms_rows(_dot(h, wstd_ref[:, 384:640]), gkv_ref[...]).astype(BF16)
    u = _dot(h, wstd_ref[:, 640:768]) * table
    lane = lax.broadcasted_iota(jnp.int32, u.shape, 1)
    krot = jnp.where(lane >= MLA_NOPE, u + pltpu.roll(u, 32, 1) + pltpu.roll(u, 96, 1), 0.0)
    kk = _dot(ckv, wk_ref[...])
    for hh in range(MLA_HEADS):
        k_out[0, hh] = (kk[:, hh * HEAD_LANES:(hh + 1) * HEAD_LANES] + krot).astype(BF16)

    vt = _dot_nt(wvt_ref[...], ckv)
    for hh in range(MLA_HEADS):
        vt_out[0, hh] = vt[hh * MLA_V:(hh + 1) * MLA_V, :].astype(BF16)


def _projections(x, cs, gin, wstd, wt, gq, wq, gkv, wk, wvt):
    b, s, d = x.shape
    rows = PROJ_ROWS

    def full(a):
        return pl.BlockSpec(a.shape, lambda i, j: (0,) * a.ndim)

    def tok(width):
        return pl.BlockSpec((1, rows, width), lambda i, j: (i, j, 0))

    def tr(n):
        return pl.BlockSpec((1, n, rows), lambda i, j: (i, 0, j))

    def tr_shape(n):
        return jax.ShapeDtypeStruct((b, n, s), BF16)

    heads_tok = pl.BlockSpec((1, MLA_HEADS, rows, HEAD_LANES), lambda i, j: (i, 0, j, 0))
    heads_tr = pl.BlockSpec((1, MLA_HEADS, MLA_V, rows), lambda i, j: (i, 0, 0, j))
    out_shape = (
        jax.ShapeDtypeStruct((b, MLA_HEADS, s, HEAD_LANES), BF16),
        jax.ShapeDtypeStruct((b, MLA_HEADS, s, HEAD_LANES), BF16),
        jax.ShapeDtypeStruct((b, MLA_HEADS, MLA_V, s), BF16),
        tr_shape(MLA_WIDTH),
        tr_shape(GLA_DK),
        tr_shape(GLA_DK),
        tr_shape(GLA_DV),
        tr_shape(GLA_DV),
        tr_shape(GLA_GATE_RANK),
        jax.ShapeDtypeStruct((b, s, d), BF16),
        jax.ShapeDtypeStruct((b, s, d), BF16),
    )
    out_specs = (heads_tok, heads_tok, heads_tr, tr(MLA_WIDTH),
                 tr(GLA_DK), tr(GLA_DK), tr(GLA_DV), tr(GLA_DV), tr(GLA_GATE_RANK), tok(d), tok(d))
    in_specs = [tok(d), tok(MLA_ROPE), full(gin), full(wstd), full(wt), full(gq),
                full(wq), full(gkv), full(wk), full(wvt)]
    return pl.pallas_call(
        _proj_kernel,
        grid=(b, s // rows),
        in_specs=in_specs,
        out_specs=out_specs,
        out_shape=out_shape,
        compiler_params=pltpu.CompilerParams(
            dimension_semantics=("parallel", "parallel"), vmem_limit_bytes=VMEM_LIMIT),
        name="proj",
    )(x, cs, gin, wstd, wt, gq, wq, gkv, wk, wvt)


def _mla_stages(q_ref, k_ref, vt_ref, o_ref, s_ref, p_ref, vx_ref):
    heads, seq = q_ref.shape[1], q_ref.shape[2]
    row = lax.broadcasted_iota(jnp.int32, (TILE, TILE), 0)
    col = lax.broadcasted_iota(jnp.int32, (TILE, TILE), 1)
    causal = row <= col
    tasks = [(hh, qi) for hh in range(heads) for qi in range(seq // TILE)]

    for hh in range(heads):
        vx_ref[hh, 0:MLA_V, :] = vt_ref[0, hh]
        vx_ref[hh, MLA_V:, :] = jnp.ones((MLA_VX - MLA_V, seq), BF16)

    def scores(t):
        hh, qi = tasks[t]
        lo, hi = qi * TILE, (qi + 1) * TILE
        qt = q_ref[0, hh, lo:hi, :]
        if qi:
            s_ref[t % 2, 0:lo, :] = _dot_nt(k_ref[0, hh, 0:lo, :], qt)
        s_ref[t % 2, lo:hi, :] = jnp.where(causal, _dot_nt(k_ref[0, hh, lo:hi, :], qt), -jnp.inf)

    def softmax(t):
        _, qi = tasks[t]
        m = jnp.full((1, TILE), -jnp.inf, F32)
        for kj in range(qi + 1):
            yield 0, 18
            m = jnp.maximum(m, jnp.max(s_ref[t % 2, kj * TILE:(kj + 1) * TILE, :], axis=0, keepdims=True))
        for kj in range(qi + 1):
            yield 0, 45
            blk = slice(kj * TILE, (kj + 1) * TILE)
            p_ref[t % 2, blk, :] = jnp.exp2(s_ref[t % 2, blk, :] - m).astype(BF16)

    def weighted_values(t):
        hh, qi = tasks[t]
        lo, hi = qi * TILE, (qi + 1) * TILE
        acc = _dot(vx_ref[hh, :, 0:hi], p_ref[t % 2, 0:hi, :])
        o_ref[0, hh, :, lo:hi] = (acc[0:MLA_V] / acc[MLA_V:MLA_V + 1]).astype(BF16)

    def program():
        blocks = [qi + 1 for _, qi in tasks]
        yield 64 * blocks[0], 20 * blocks[0]
        scores(0)
        for t in range(len(tasks)):
            if t + 1 < len(tasks):
                yield 64 * blocks[t + 1], 20 * blocks[t + 1]
                scores(t + 1)
            if t:
                yield 32 * blocks[t - 1], 10
                weighted_values(t - 1)
            yield from softmax(t)
        yield 32 * blocks[-1], 10
        weighted_values(len(tasks) - 1)

    blocks = sum(qi + 1 for _, qi in tasks)
    return program(), (96.0 * blocks, 83.0 * blocks + 10.0 * len(tasks))


def _gla_constants():
    u = np.arange(TILE)[:, None]
    t = np.arange(TILE)[None, :]
    seg = np.concatenate([(u <= t).astype(np.float32)] * GLA_PARTS, axis=0)
    j = np.arange(TILE)[:, None]
    i = np.arange(TILE)[None, :]
    level = np.full((TILE, TILE), -1, np.int32)
    level[(j // GLA_BLOCK) == (i // GLA_BLOCK)] = GLA_BAND
    for idx, m in enumerate(GLA_LEVELS):
        same = (j // (2 * m)) == (i // (2 * m))
        split = ((j % (2 * m)) < m) & ((i % (2 * m)) >= m)
        level[same & split & (j < i)] = idx
    return seg, level[:HALF, :HALF]


def _gla_band(qf, kf, decay, band_ref):
    heads = qf.shape[0] // GLA_HEAD_K
    rel = (lax.broadcasted_iota(jnp.int32, (GLA_BLOCK, TILE), 1) % GLA_BLOCK
           - lax.broadcasted_iota(jnp.int32, (GLA_BLOCK, TILE), 0))
    group = 8
    for hh in range(heads):
        acc = [None] * GLA_BLOCK
        for rg in range(GLA_HEAD_K // group):
            yield 0, 30
            rows = slice(hh * GLA_HEAD_K + rg * group, hh * GLA_HEAD_K + (rg + 1) * group)
            w = [kf[rows]]
            span = decay[rows]
            step = 1
            while step < GLA_BLOCK:
                w += [span * pltpu.roll(w[d], step, 1) for d in range(step)]
                span = span * pltpu.roll(span, step, 1)
                step *= 2
            q8 = qf[rows]
            for d in range(GLA_BLOCK):
                prod = q8 * w[d]
                acc[d] = prod if acc[d] is None else acc[d] + prod
        yield 0, 28
        band = jnp.zeros((GLA_BLOCK, TILE), F32)
        for d in range(GLA_BLOCK):
            diag = jnp.sum(acc[d], axis=0, keepdims=True)
            band = jnp.where(rel == d, diag, band)
        band_ref[hh] = band


def _gla_stages(qt_ref, kt_ref, vt_ref, glow_ref, wg_ref, bg_ref, seg_ref, level_ref, gn_ref,
                o_ref, state_ref, fact_ref, band_ref):
    heads = qt_ref.shape[1] // GLA_HEAD_K
    state_ref[...] = jnp.zeros_like(state_ref)
    level = level_ref[...]
    lane = lax.broadcasted_iota(jnp.int32, (GLA_HEAD_K, TILE), 1)

    def prepare(tile):
        ts = slice(tile * TILE, (tile + 1) * TILE)
        slot = tile % 2
        yield 8 + 24 * heads, 60 * heads
        z = _dot(wg_ref[...], glow_ref[0, :, ts]) + bg_ref[...]
        g = -(jnp.maximum(-z, 0.0) + jnp.log1p(jnp.exp(-jnp.abs(z)))) * (1.0 / GLA_GATE_NORMALIZER)
        parts, rest = [], g
        for _ in range(GLA_PARTS):
            parts.append(rest.astype(BF16))
            rest = rest - parts[-1].astype(F32)
        csum = _dot(jnp.concatenate(parts, axis=1), seg_ref[...])
        fact_ref[slot, 0] = jnp.exp(csum)
        fact_ref[slot, 1] = jnp.exp(csum[:, TILE - 1:TILE] - csum)
        lanes = lax.broadcasted_iota(jnp.int32, csum.shape, 1)
        for idx, m in enumerate(GLA_LEVELS):
            yield 0, 8 * heads
            mid = csum[:, m - 1:m]
            for c in range(3 * m - 1, TILE, 2 * m):
                mid = jnp.where(lanes >= c - m + 1, csum[:, c:c + 1], mid)
            fact_ref[slot, 2 + idx] = jnp.exp(jnp.where((lanes // m) % 2 == 1, csum - mid, mid - csum))
        yield from _gla_band(qt_ref[0, :, ts].astype(F32), kt_ref[0, :, ts].astype(F32), jnp.exp(g),
                             band_ref.at[slot])

    def attend(tile, hh):
        ts = slice(tile * TILE, (tile + 1) * TILE)
        slot = tile % 2
        rk = slice(hh * GLA_HEAD_K, (hh + 1) * GLA_HEAD_K)
        rv = slice(hh * GLA_HEAD_V, (hh + 1) * GLA_HEAD_V)
        yield 20, 25
        qf = qt_ref[0, rk, ts].astype(F32)
        kf = kt_ref[0, rk, ts].astype(F32)
        vt = vt_ref[0, rv, ts]
        f_in = fact_ref[slot, 0, rk, :]
        q_in = (qf * f_in).astype(BF16)
        k_out = (kf * fact_ref[slot, 1, rk, :]).astype(BF16)
        state = state_ref[hh]
        o = _dot_tn(state.astype(BF16), q_in)

        far = []
        for idx, m in enumerate(GLA_LEVELS):
            yield 64, 20
            f = fact_ref[slot, 2 + idx, rk, :]
            upper = (lane // m) % 2 == 1
            q_up = jnp.where(upper, qf * f, 0.0).astype(BF16)
            k_lo = jnp.where(upper, 0.0, kf * f).astype(BF16)
            far.append(_dot_tn(k_lo, q_up))
        yield 32, 60
        band = jnp.tile(band_ref[slot, hh], (HALF // GLA_BLOCK, 1))

        def diagonal(c):
            sl = slice(c * HALF, (c + 1) * HALF)
            blk = far[1][sl, sl]
            for idx in range(2, len(GLA_LEVELS)):
                blk = jnp.where(level == idx, far[idx][sl, sl], blk)
            return jnp.where(level == GLA_BAND, band[:, sl], blk)

        at = jnp.concatenate(
            [jnp.concatenate([diagonal(0), far[0][0:HALF, HALF:]], axis=1),
             jnp.concatenate([jnp.zeros((HALF, HALF), F32), diagonal(1)], axis=1)], axis=0)
        o = o + _dot(vt, at.astype(BF16))

        yield 16, 45
        ms = jnp.mean(o * o, axis=0, keepdims=True)
        o_ref[0, rv, ts] = (o * lax.rsqrt(ms + EPS) * gn_ref[...]).astype(BF16)

        state_ref[hh] = state * f_in[:, TILE - 1:TILE] + _dot_nt(k_out, vt)

    def program():
        tiles = qt_ref.shape[2] // TILE
        yield from prepare(0)
        for tile in range(tiles):
            if tile + 1 < tiles:
                yield from prepare(tile + 1)
            for hh in range(heads):
                yield from attend(tile, hh)

    tiles = qt_ref.shape[2] // TILE
    mxu = 8 + 24 * heads + heads * (20 + 64 * len(GLA_LEVELS) + 32 + 16)
    vpu = 60 * heads + 8 * heads * len(GLA_LEVELS) + 4 * heads * (GLA_BLOCK - 1 + GLA_BLOCK.bit_length() - 1) + 12 * heads * GLA_BLOCK \
        + heads * (25 + 20 * len(GLA_LEVELS) + 60 + 45)
    return program(), (float(tiles * mxu), float(tiles * vpu))


def _mixers_kernel(q_ref, k_ref, vtm_ref, gq_ref, gk_ref, gv_ref, glow_ref, wg_ref, bg_ref, seg_ref,
                   level_ref, gn_ref, om_ref, og_ref,
                   s_ref, p_ref, vx_ref, state_ref, fact_ref, band_ref):
    mla, mla_totals = _mla_stages(q_ref, k_ref, vtm_ref, om_ref, s_ref, p_ref, vx_ref)
    gla, gla_totals = _gla_stages(gq_ref, gk_ref, gv_ref, glow_ref, wg_ref, bg_ref, seg_ref, level_ref,
                                  gn_ref, og_ref, state_ref, fact_ref, band_ref)
    _interleave([mla, gla], [mla_totals, gla_totals])


def _interleave(programs, totals):
    all_mxu = sum(t[0] for t in totals)
    all_vpu = sum(t[1] for t in totals)
    pending = [next(p) for p in programs]
    spent = [0.0] * len(programs)
    mxu = vpu = 0.0
    live = set(range(len(programs)))
    while live:
        def cost(i):
            m, v = pending[i]
            skew = abs((mxu + m) / all_mxu - (vpu + v) / all_vpu)
            lag = spent[i] / sum(totals[i]) - min(spent[j] / sum(totals[j]) for j in live)
            return skew + lag
        i = min(live, key=cost)
        m, v = pending[i]
        mxu, vpu, spent[i] = mxu + m, vpu + v, spent[i] + m + v
        try:
            pending[i] = next(programs[i])
        except StopIteration:
            live.discard(i)


def _mixers(q, k, vtm, gq, gk, gv, glow, wg, bg, seg, level, gn):
    b, heads, s, _ = q.shape
    hp = heads // MIX_SPLIT
    gh = GLA_HEADS // MIX_SPLIT
    gk_rows, gv_rows = gh * GLA_HEAD_K, gh * GLA_HEAD_V

    def full(a):
        return pl.BlockSpec(a.shape, lambda i, j: (0,) * a.ndim)

    def head_rows(n):
        return pl.BlockSpec((1, n, s), lambda i, j: (i, j, 0))

    tok = pl.BlockSpec((1, hp, s, HEAD_LANES), lambda i, j: (i, j, 0, 0))
    trm = pl.BlockSpec((1, hp, MLA_V, s), lambda i, j: (i, j, 0, 0))
    in_specs = [tok, tok, trm, head_rows(gk_rows), head_rows(gk_rows), head_rows(gv_rows),
                pl.BlockSpec((1, GLA_GATE_RANK, s), lambda i, j: (i, 0, 0)),
                pl.BlockSpec((gk_rows, GLA_GATE_RANK), lambda i, j: (j, 0)),
                pl.BlockSpec((gk_rows, 1), lambda i, j: (j, 0)),
                full(seg), full(level), full(gn)]
    return pl.pallas_call(
        _mixers_kernel,
        grid=(b, MIX_SPLIT),
        in_specs=in_specs,
        out_specs=(trm, head_rows(gv_rows)),
        out_shape=(jax.ShapeDtypeStruct((b, heads, MLA_V, s), BF16),
                   jax.ShapeDtypeStruct((b, GLA_DV, s), BF16)),
        scratch_shapes=[pltpu.VMEM((2, s, TILE), F32),
                        pltpu.VMEM((2, s, TILE), BF16),
                        pltpu.VMEM((hp, MLA_VX, s), BF16),
                        pltpu.VMEM((gh, GLA_HEAD_K, GLA_HEAD_V), F32),
                        pltpu.VMEM((2, GLA_NFACT, gk_rows, TILE), F32),
                        pltpu.VMEM((2, gh, GLA_BLOCK, TILE), F32)],
        compiler_params=pltpu.CompilerParams(
            dimension_semantics=("parallel", "parallel"), vmem_limit_bytes=VMEM_LIMIT),
        name="mixers",
    )(q, k, vtm, gq, gk, gv, glow, wg, bg, seg, level, gn)


def _out_kernel(x_ref, p_ref, om_ref, gm_ref, og_ref, gg_ref, ma_ref, mb_ref, wa_ref, wb_ref,
                wout_ref, wple_ref, gple_ref, ggate_ref, wgate_ref, gfin_ref, o_ref, *, final):
    def chunk(r):
        n = r.stop - r.start

        def branch(o_t, gh, w_ref):
            zt = o_t * (gh * (jnp.tanh(gh) + 1.0))
            return _dot_tn(zt, w_ref[...])

        ya = branch(om_ref[0, :, :, r].reshape(MLA_WIDTH, n), gm_ref[0, :, r], wa_ref)
        yield
        yb = branch(og_ref[0, :, r], gg_ref[0, :, r], wb_ref)
        yield
        merged2 = ((jnp.tanh(ma_ref[0, r, :]) + 1.0) * ya.astype(BF16)
                   + (jnp.tanh(mb_ref[0, r, :]) + 1.0) * yb.astype(BF16))
        x1 = x_ref[0, r, :] + _dot(merged2, wout_ref[...])
        yield
        e_half = _rms_rows(_dot(p_ref[0, r, :].astype(BF16), wple_ref[...]), gple_ref[...])
        yield
        gate2 = jnp.tanh(_dot(_rms_rows(x1, ggate_ref[...]).astype(BF16), wgate_ref[...])) + 1.0
        yield
        x2 = x1 + gate2 * e_half
        o_ref[0, r, :] = _rms_rows(x2, gfin_ref[...]) if final else x2

    half = OUT_ROWS // 2
    chunks = [chunk(slice(0, half)), chunk(slice(half, OUT_ROWS))]
    while chunks:
        for c in list(chunks):
            if next(c, c) is c:
                chunks.remove(c)


def _output(x, p, om, gm, og, gg, ma, mb, wa, wb, wout, wple, gple, ggate, wgate, gfin, final):
    b, s, d = x.shape
    rows = OUT_ROWS

    def full(a):
        return pl.BlockSpec(a.shape, lambda i, j: (0,) * a.ndim)

    def tok(width):
        return pl.BlockSpec((1, rows, width), lambda i, j: (i, j, 0))

    def tr(n):
        return pl.BlockSpec((1, n, rows), lambda i, j: (i, 0, j))

    heads_tr = pl.BlockSpec((1, MLA_HEADS, MLA_V, rows), lambda i, j: (i, 0, 0, j))
    in_specs = [tok(d), tok(p.shape[-1]), heads_tr, tr(MLA_WIDTH), tr(GLA_DV), tr(GLA_DV),
                tok(d), tok(d), full(wa), full(wb), full(wout), full(wple), full(gple), full(ggate),
                full(wgate), full(gfin)]
    return pl.pallas_call(
        functools.partial(_out_kernel, final=final),
        grid=(b, s // rows),
        in_specs=in_specs,
        out_specs=tok(d),
        out_shape=jax.ShapeDtypeStruct((b, s, d), F32),
        compiler_params=pltpu.CompilerParams(
            dimension_semantics=("parallel", "parallel"), vmem_limit_bytes=VMEM_LIMIT),
        name="out",
    )(x, p, om, gm, og, gg, ma, mb, wa, wb, wout, wple, gple, ggate, wgate, gfin)


def _swap_halves(w):
    half = w.shape[-1] // 2
    return jnp.concatenate([-w[..., half:], w[..., :half]], axis=-1)


def _rope_kernel(pos_ref, freq_ref, phase_ref, o_ref):
    pos = pos_ref[0].astype(F32)
    rows = pos.shape[0]
    slot = lax.broadcasted_iota(jnp.int32, (rows, ROPE_PACK * MLA_ROPE), 1) // MLA_ROPE
    posx = pos[:, 0:1]
    for t in range(1, ROPE_PACK):
        posx = jnp.where(slot == t, pos[:, t:t + 1], posx)
    dense = jnp.cos(posx * freq_ref[...] - phase_ref[...])
    for t in range(ROPE_PACK):
        o_ref[0, t * rows:(t + 1) * rows, :] = dense[:, t * MLA_ROPE:(t + 1) * MLA_ROPE]


def _rope_table(positions):
    b, s = positions.shape
    inv_freq = 1.0 / (ROPE_THETA ** (jnp.arange(0, MLA_ROPE, 2, dtype=F32) / MLA_ROPE))
    half = MLA_ROPE // 2
    freq = jnp.tile(jnp.concatenate([inv_freq, inv_freq]), ROPE_PACK)[None, :]
    phase = jnp.tile(jnp.concatenate([jnp.zeros((half,), F32), jnp.full((half,), np.pi / 2, F32)]),
                     ROPE_PACK)[None, :]
    rows = s // ROPE_PACK
    return pl.pallas_call(
        _rope_kernel,
        grid=(b,),
        in_specs=[pl.BlockSpec((1, rows, ROPE_PACK), lambda i: (i, 0, 0)),
                  pl.BlockSpec((1, ROPE_PACK * MLA_ROPE), lambda i: (0, 0)),
                  pl.BlockSpec((1, ROPE_PACK * MLA_ROPE), lambda i: (0, 0))],
        out_specs=pl.BlockSpec((1, s, MLA_ROPE), lambda i: (i, 0, 0)),
        out_shape=jax.ShapeDtypeStruct((b, s, MLA_ROPE), F32),
        compiler_params=pltpu.CompilerParams(dimension_semantics=("parallel",)),
        name="rope",
    )(positions.reshape(b, ROPE_PACK, rows).transpose(0, 2, 1), freq, phase)


def kernel(x, p, positions, norm_in_g, w_in, q_norm_g, w_uq, kv_norm_g, w_ukv, w_gk_up, b_gk,
           gla_norm_g, w_mla_br, w_gla_br, w_out, w_ple, ple_norm_g, ple_gate_norm_g, w_ple_gate,
           final_norm_g):
    b, s, d = x.shape
    depth = w_in.shape[0]
    assert s % PROJ_ROWS == 0 and s % OUT_ROWS == 0 and s % TILE == 0
    cs = _rope_table(positions)
    seg_np, level_np = _gla_constants()
    seg = jnp.asarray(seg_np, BF16)
    level = jnp.asarray(level_np)

    sizes = (MLA_Q_RANK, MLA_KV_RANK, MLA_ROPE, MLA_WIDTH, GLA_DK, GLA_DK, GLA_DV, GLA_GATE_RANK,
             GLA_DV, d, d)
    offs = np.concatenate([[0], np.cumsum(sizes)])

    def cols(w, idx):
        return w[:, int(offs[idx]):int(offs[idx + 1])]

    for l in range(depth):
        w = w_in[l].astype(BF16)
        wkr = cols(w, 2)
        wstd = jnp.concatenate(
            [cols(w, 0), cols(w, 1), jnp.zeros((d, MLA_NOPE), BF16), wkr, _swap_halves(wkr),
             cols(w, 9) * 0.5, cols(w, 10) * 0.5], axis=1)
        wt = jnp.concatenate(
            [cols(w, 3) * 0.5, cols(w, 4) * (GLA_HEAD_K ** -0.5), cols(w, 5), cols(w, 6),
             cols(w, 8) * 0.5, cols(w, 7)], axis=1).T
        uq = w_uq[l].reshape(MLA_Q_RANK, MLA_HEADS, MLA_NOPE + MLA_ROPE)
        uq_r = uq[..., MLA_NOPE:]
        wq = jnp.concatenate([uq[..., :MLA_NOPE], uq_r, _swap_halves(uq_r)], axis=-1)
        wq = wq.reshape(MLA_Q_RANK, MLA_HEADS * HEAD_LANES).astype(BF16)
        ukv = w_ukv[l].reshape(MLA_KV_RANK, MLA_HEADS, MLA_NOPE + MLA_V)
        wk = jnp.concatenate([ukv[..., :MLA_NOPE], jnp.zeros_like(ukv[..., :MLA_NOPE])], axis=-1)
        wk = wk.reshape(MLA_KV_RANK, MLA_HEADS * HEAD_LANES).astype(BF16)
        wvt = ukv[..., MLA_NOPE:].reshape(MLA_KV_RANK, MLA_WIDTH).T.astype(BF16)

        (q, k, vt, gmla, gq, gk, gv, ggla, glow, ma, mb) = _projections(
            x, cs, norm_in_g[l][None, :], wstd, wt, q_norm_g[l][None, :], wq,
            kv_norm_g[l][None, :], wk, wvt)

        o_mla, o_gla = _mixers(q, k, vt, gq, gk, gv, glow, w_gk_up[l].T.astype(BF16), b_gk[l][:, None],
                               seg, level, gla_norm_g[l][:, None])

        x = _output(x, p[l], o_mla, gmla, o_gla, ggla, ma, mb, w_mla_br[l].astype(BF16),
                    w_gla_br[l].astype(BF16), (w_out[l] * 0.5).astype(BF16), w_ple[l].astype(BF16),
                    ple_norm_g[l][None, :] * 0.5, ple_gate_norm_g[l][None, :],
                    (w_ple_gate[l] * 0.5).astype(BF16),
                    final_norm_g[None, :], l == depth - 1)
    return x
```
